```python
import jax, jax.numpy as jnp
from jax import lax
import numpy as np

D_MODEL = 1024
BATCH = 8
SEQ = 8192
DEPTH = 2
DEC_BATCH = 16
DEC_SEQ = 2048
PAST_LEN = 128

MIX_WIDTH = D_MODEL
N_MIXERS = 4
GROUP = MIX_WIDTH // N_MIXERS
HEADS_PER_GROUP = 4
HEAD_DIM = GROUP // HEADS_PER_GROUP
CONV_WIDTH_B = 31
SHORT_CONV_WIDTH = 3
POOL_WINDOWS = (2, 4, 8, 16)
N_POOL_GROUPS = len(POOL_WINDOWS)
IN_COLS = 7 * GROUP
N_EXPERTS = 32
TOP_K = 4
D_FF = D_MODEL
SWIGLU_ALPHA = 1.702
SWIGLU_LIMIT = 7.0
MOE_BLOCK = 256
RMS_EPS = 1e-6
GN_EPS = 1e-5

kernel_name = "hybrid_parallel_group_encoder_moe"


def rmsnorm(x, g):
    xf = x.astype(jnp.float32)
    y = xf * lax.rsqrt(jnp.mean(xf * xf, axis=-1, keepdims=True) + RMS_EPS)
    return (y * g.astype(jnp.float32)).astype(x.dtype)


def depthwise_conv(x, w):
    k = w.shape[0]
    pad = k // 2
    return lax.conv_general_dilated(
        x, w[:, None, :].astype(x.dtype), window_strides=(1,), padding=[(pad, pad)],
        dimension_numbers=('NWC', 'WIO', 'NWC'), feature_group_count=x.shape[-1])


def fourier_mixer(a, w):
    bsz, s, _ = a.shape
    ah = a.reshape(bsz, s, HEADS_PER_GROUP, HEAD_DIM).astype(jnp.float32)
    f = jnp.fft.fft2(ah, axes=(1, 3), norm='ortho').real
    return f.reshape(bsz, s, GROUP).astype(a.dtype) @ w


def conformer_conv(v, g, w_dw, b_dw, gn_g, gn_b, w_pw):
    z = v * jax.nn.sigmoid(g)
    z = depthwise_conv(z, w_dw) + b_dw
    bsz, s, _ = z.shape
    zh = z.reshape(bsz, s, HEADS_PER_GROUP, HEAD_DIM).astype(jnp.float32)
    mu = jnp.mean(zh, axis=-1, keepdims=True)
    var = jnp.mean(jnp.square(zh - mu), axis=-1, keepdims=True)
    zn = ((zh - mu) * lax.rsqrt(var + GN_EPS)).reshape(bsz, s, GROUP)
    zn = zn * gn_g.astype(jnp.float32) + gn_b.astype(jnp.float32)
    return jax.nn.silu(zn).astype(v.dtype) @ w_pw


def pool_mixer(p, w_pool, scale):
    bsz, s, _ = p.shape
    pf = p.astype(jnp.float32).reshape(bsz, s, N_POOL_GROUPS, HEAD_DIM)
    csum = jnp.pad(jnp.cumsum(pf, axis=1), ((0, 0), (1, 0), (0, 0), (0, 0)))
    t = jnp.arange(s)
    outs = []
    for gi, w in enumerate(POOL_WINDOWS):
        lo = jnp.clip(t - w // 2, 0, s)
        hi = jnp.clip(t - w // 2 + w, 0, s)
        win = csum[:, hi, gi] - csum[:, lo, gi]
        mean = win / (hi - lo).astype(jnp.float32)[None, :, None]
        outs.append(mean - pf[:, :, gi])
    d = jnp.stack(outs, axis=2).astype(p.dtype)
    y = jnp.einsum('bsgc,gcd->bsgd', d, w_pool).reshape(bsz, s, GROUP)
    return y * scale


def short_conv_mixer(bg, cg, xv, w3, w_out):
    return (bg * depthwise_conv(cg * xv, w3)) @ w_out


def moe(h, router_w, router_b, gate_w, gate_b, up_w, up_b, down_w, down_b):
    bsz, s, d = h.shape
    t_tok = bsz * s
    n_assign = t_tok * TOP_K
    n_blocks = -(-n_assign // MOE_BLOCK) + N_EXPERTS
    n_slots = n_blocks * MOE_BLOCK
    xt = h.reshape(t_tok, d)
    logits = (xt @ router_w + router_b).astype(jnp.float32)
    top_val, top_idx = lax.top_k(logits, TOP_K)
    gates = jax.nn.softmax(top_val, axis=-1).reshape(n_assign)
    flat_e = top_idx.reshape(n_assign).astype(jnp.int32)
    order = jnp.argsort(flat_e).astype(jnp.int32)
    sorted_e = flat_e[order]
    sorted_tok = order // TOP_K
    counts = jnp.bincount(flat_e, length=N_EXPERTS).astype(jnp.int32)
    padded = (counts + MOE_BLOCK - 1) // MOE_BLOCK * MOE_BLOCK
    start = jnp.cumsum(counts) - counts
    pend = jnp.cumsum(padded)
    pstart = pend - padded
    dest = pstart[sorted_e] + jnp.arange(n_assign, dtype=jnp.int32) - start[sorted_e]
    slot_tok = jnp.full((n_slots,), t_tok, jnp.int32).at[dest].set(sorted_tok)
    slot_gate = jnp.zeros((n_slots,), jnp.float32).at[dest].set(gates[order])
    block_start = jnp.arange(n_blocks, dtype=jnp.int32) * MOE_BLOCK
    block_e = jnp.minimum(jnp.searchsorted(pend, block_start, side='right'), N_EXPERTS - 1)
    xpad = jnp.concatenate([xt, jnp.zeros((1, d), xt.dtype)], axis=0)
    xs = xpad[slot_tok].reshape(n_blocks, MOE_BLOCK, d)

    def expert_block(args):
        xb, e = args
        gt = jnp.minimum(xb @ gate_w[e] + gate_b[e], SWIGLU_LIMIT)
        up = jnp.clip(xb @ up_w[e] + up_b[e], -SWIGLU_LIMIT, SWIGLU_LIMIT)
        act = (up + 1) * (gt * jax.nn.sigmoid(SWIGLU_ALPHA * gt))
        return act @ down_w[e] + down_b[e]

    ys = lax.map(expert_block, (xs, block_e)).reshape(n_slots, d)
    ys = ys * slot_gate[:, None].astype(ys.dtype)
    out = jnp.zeros((t_tok + 1, d), ys.dtype).at[slot_tok].add(ys)[:t_tok]
    return out.reshape(bsz, s, d)


def encoder(x, c, weights):
    (norm1_g, norm2_g, ada_w, ada_b, w_in, fourier_w, conv31_w, conv31_b, gn_g, gn_b, pw_w,
     pool_w, pool_scale, conv3_w, sconv_out_w, w_o, router_w, router_b,
     exp_gate_w, exp_gate_b, exp_up_w, exp_up_b, exp_down_w, exp_down_b, final_g) = weights
    cs = jax.nn.silu(c)
    for l in range(DEPTH):
        mod = cs @ ada_w[l] + ada_b[l]
        sh1, sc1, gt1, sh2, sc2, gt2 = jnp.split(mod[:, None, :], 6, axis=-1)
        h = rmsnorm(x, norm1_g[l]) * (1 + sc1) + sh1
        u = h @ w_in[l]
        a, v, g, p, bg, cg, xv = jnp.split(u, [GROUP * i for i in range(1, 7)], axis=-1)
        ya = fourier_mixer(a, fourier_w[l])
        yb = conformer_conv(v, g, conv31_w[l], conv31_b[l], gn_g[l], gn_b[l], pw_w[l])
        yc = pool_mixer(p, pool_w[l], pool_scale[l])
        yd = short_conv_mixer(bg, cg, xv, conv3_w[l], sconv_out_w[l])
        mix = jnp.concatenate([ya, yb, yc, yd], axis=-1) @ w_o[l]
        x = x + gt1 * mix
        h2 = rmsnorm(x, norm2_g[l]) * (1 + sc2) + sh2
        x = x + gt2 * moe(h2, router_w[l], router_b[l], exp_gate_w[l], exp_gate_b[l],
                          exp_up_w[l], exp_up_b[l], exp_down_w[l], exp_down_b[l])
    return rmsnorm(x, final_g)


def setup_inputs(seed: int = 0) -> dict:
    key = jax.random.key(seed)
    ks = jax.random.split(key, 32)
    f32 = jnp.float32
    G = GROUP
    D = D_MODEL
    E = N_EXPERTS
    F = D_FF
    L = DEPTH

    def nrm(k, shape, scale):
        return jax.random.normal(k, shape, f32) * scale

    return {
        'x_prompt': nrm(ks[0], (BATCH, SEQ, D), 1.0),
        'x_sample': nrm(ks[1], (DEC_BATCH, DEC_SEQ, D), 1.0),
        'c_prompt': nrm(ks[2], (BATCH, D), 1.0),
        'c_sample': nrm(ks[3], (DEC_BATCH, D), 1.0),
        'norm1_g': 1.0 + nrm(ks[4], (L, D), 0.02),
        'norm2_g': 1.0 + nrm(ks[5], (L, D), 0.02),
        'ada_w': nrm(ks[6], (L, D, 6 * D), D ** -0.5),
        'ada_b': nrm(ks[7], (L, 6 * D), 0.02),
        'w_in': nrm(ks[8], (L, D, IN_COLS), D ** -0.5),
        'fourier_w': nrm(ks[9], (L, G, G), G ** -0.5),
        'conv31_w': nrm(ks[10], (L, CONV_WIDTH_B, G), CONV_WIDTH_B ** -0.5),
        'conv31_b': nrm(ks[11], (L, G), 0.02),
        'gn_g': 1.0 + nrm(ks[12], (L, G), 0.02),
        'gn_b': nrm(ks[13], (L, G), 0.02),
        'pw_w': nrm(ks[14], (L, G, G), G ** -0.5),
        'pool_w': nrm(ks[15], (L, N_POOL_GROUPS, HEAD_DIM, HEAD_DIM), HEAD_DIM ** -0.5),
        'pool_scale': 1.0 + nrm(ks[16], (L, G), 0.02),
        'conv3_w': nrm(ks[17], (L, SHORT_CONV_WIDTH, G), SHORT_CONV_WIDTH ** -0.5),
        'sconv_out_w': nrm(ks[18], (L, G, G), G ** -0.5),
        'w_o': nrm(ks[19], (L, D, D), D ** -0.5),
        'router_w': nrm(ks[20], (L, D, E), D ** -0.5),
        'router_b': nrm(ks[21], (L, E), 0.01),
        'exp_gate_w': nrm(ks[22], (L, E, D, F), D ** -0.5),
        'exp_gate_b': nrm(ks[23], (L, E, F), 0.01),
        'exp_up_w': nrm(ks[24], (L, E, D, F), D ** -0.5),
        'exp_up_b': nrm(ks[25], (L, E, F), 0.01),
        'exp_down_w': nrm(ks[26], (L, E, F, D), F ** -0.5),
        'exp_down_b': nrm(ks[27], (L, E, D), 0.01),
        'final_g': 1.0 + nrm(ks[28], (D,), 0.02),
    }


def reference(x_prompt, x_sample, c_prompt, c_sample, norm1_g, norm2_g, ada_w, ada_b, w_in,
              fourier_w, conv31_w, conv31_b, gn_g, gn_b, pw_w, pool_w, pool_scale, conv3_w,
              sconv_out_w, w_o, router_w, router_b, exp_gate_w, exp_gate_b, exp_up_w, exp_up_b,
              exp_down_w, exp_down_b, final_g):
    weights = (norm1_g, norm2_g, ada_w, ada_b, w_in, fourier_w, conv31_w, conv31_b, gn_g, gn_b,
               pw_w, pool_w, pool_scale, conv3_w, sconv_out_w, w_o, router_w, router_b,
               exp_gate_w, exp_gate_b, exp_up_w, exp_up_b, exp_down_w, exp_down_b, final_g)
    y_prompt = encoder(x_prompt, c_prompt, weights)
    y_sample = encoder(x_sample, c_sample, weights)
    return (y_prompt, y_sample)
```

```python
import functools
import math

import jax
import jax.numpy as jnp
from jax import lax
from jax.experimental import pallas as pl
from jax.experimental.pallas import tpu as pltpu

D_MODEL = 1024
GROUP = 256
HEAD_DIM = 64
CONV31 = 31
POOL_WINDOWS = (2, 4, 8, 16)
N_EXPERTS = 32
TOP_K = 4
D_FF = 1024
SWIGLU_ALPHA = 1.702
SWIGLU_LIMIT = 7.0
MOE_BLOCK = 256
RMS_EPS = 1e-6
GN_EPS = 1e-5
HALO = 16
LANES = 128
VMEM_LIMIT = 48 * 1024 * 1024

F32 = jnp.float32
BF16 = jnp.bfloat16


def _cparams(sem):
    return pltpu.CompilerParams(dimension_semantics=sem, vmem_limit_bytes=VMEM_LIMIT)


def _dot(a, b):
    return jnp.dot(a, b, preferred_element_type=F32)


def _split_bf16(x):
    hi = x.astype(BF16)
    lo = (x - hi.astype(F32)).astype(BF16)
    return hi, lo


def _dot_hilo(x, w_bf16):
    hi, lo = _split_bf16(x)
    return _dot(hi, w_bf16) + _dot(lo, w_bf16)


def _sigmoid(x):
    return 1.0 / (1.0 + jnp.exp(-x))


def _ada_kernel(c_ref, w_ref, b_ref, o_ref):
    c = c_ref[...]
    cs = c * _sigmoid(c)
    c_hi, c_lo = _split_bf16(cs)
    w = w_ref[...]
    w_hi, w_lo = _split_bf16(w)
    o_ref[...] = _dot(c_hi, w_hi) + _dot(c_lo, w_hi) + _dot(c_hi, w_lo) + b_ref[...]


def _ada(c, ada_w_l, ada_b_l):
    bsz = c.shape[0]
    n_out = ada_w_l.shape[1]
    return pl.pallas_call(
        _ada_kernel,
        out_shape=jax.ShapeDtypeStruct((bsz, n_out), F32),
        grid=(n_out // D_MODEL,),
        in_specs=[
            pl.BlockSpec((bsz, D_MODEL), lambda j: (0, 0)),
            pl.BlockSpec((D_MODEL, D_MODEL), lambda j: (0, j)),
            pl.BlockSpec((1, D_MODEL), lambda j: (0, j)),
        ],
        out_specs=pl.BlockSpec((bsz, D_MODEL), lambda j: (0, j)),
        compiler_params=_cparams(("arbitrary",)),
        name="ada",
    )(c, ada_w_l, ada_b_l.reshape(1, n_out))


def _rms_mod(x, g, scale, shift):
    ms = jnp.mean(x * x, axis=-1, keepdims=True)
    y = x * lax.rsqrt(ms + RMS_EPS) * g
    return y * (1.0 + scale) + shift


def _inproj_kernel(x_ref, mod_ref, g_ref, w_ref, cs_ref, p_ref, u_ref):
    x = x_ref[0]
    h = _rms_mod(x, g_ref[...], mod_ref[0, 1:2, :], mod_ref[0, 0:1, :]).astype(BF16)
    a = _dot(h, w_ref[:, 0:GROUP]).astype(BF16)
    p_ref[0] = _dot(a, cs_ref[...]).astype(BF16)
    for j in range(6):
        u_ref[0, :, j * GROUP:(j + 1) * GROUP] = _dot(
            h, w_ref[:, (j + 1) * GROUP:(j + 2) * GROUP]).astype(BF16)


def _inproj(x, mod, g, w_in, chan_dft, tm):
    bsz, s, _ = x.shape
    return pl.pallas_call(
        _inproj_kernel,
        out_shape=(jax.ShapeDtypeStruct((bsz, s, 2 * GROUP), BF16),
                   jax.ShapeDtypeStruct((bsz, s, 6 * GROUP), BF16)),
        grid=(bsz, s // tm),
        in_specs=[
            pl.BlockSpec((1, tm, D_MODEL), lambda b, i: (b, i, 0)),
            pl.BlockSpec((1, 8, D_MODEL), lambda b, i: (b, 0, 0)),
            pl.BlockSpec((1, D_MODEL), lambda b, i: (0, 0)),
            pl.BlockSpec((D_MODEL, 7 * GROUP), lambda b, i: (0, 0)),
            pl.BlockSpec((GROUP, 2 * GROUP), lambda b, i: (0, 0)),
        ],
        out_specs=(pl.BlockSpec((1, tm, 2 * GROUP), lambda b, i: (b, i, 0)),
                   pl.BlockSpec((1, tm, 6 * GROUP), lambda b, i: (b, i, 0))),
        compiler_params=_cparams(("arbitrary", "arbitrary")),
        name="inproj",
    )(x, mod, g, w_in, chan_dft)


def _seqdft_kernel(c_ref, s_ref, p_ref, o_ref, acc_ref):
    k = pl.program_id(1)
    b = pl.program_id(2)
    contrib = _dot(c_ref[...], p_ref[0, :, 0:GROUP]) + _dot(s_ref[...], p_ref[0, :, GROUP:2 * GROUP])

    @pl.when(k == 0)
    def _():
        acc_ref[b] = contrib

    @pl.when(k > 0)
    def _():
        acc_ref[b] = acc_ref[b] + contrib

    @pl.when(k == pl.num_programs(1) - 1)
    def _():
        o_ref[b] = acc_ref[b].astype(BF16)


def _seqdft(cos_t, nsin_t, p, tm, tk):
    bsz, s, _ = p.shape
    return pl.pallas_call(
        _seqdft_kernel,
        out_shape=jax.ShapeDtypeStruct((bsz, s, GROUP), BF16),
        grid=(s // tm, s // tk, bsz),
        in_specs=[
            pl.BlockSpec((tm, tk), lambda i, k, b: (i, k)),
            pl.BlockSpec((tm, tk), lambda i, k, b: (i, k)),
            pl.BlockSpec((1, tk, 2 * GROUP), lambda i, k, b: (b, k, 0)),
        ],
        out_specs=pl.BlockSpec((bsz, tm, GROUP), lambda i, k, b: (0, i, 0)),
        scratch_shapes=[pltpu.VMEM((bsz, tm, GROUP), F32)],
        compiler_params=_cparams(("arbitrary", "arbitrary", "arbitrary")),
        name="seqdft",
    )(cos_t, nsin_t, p)


U_V, U_G, U_P, U_BG, U_CG, U_XV = (j * GROUP for j in range(6))


def _mix_kernel(y_ref, u_ref, up_ref, un_ref, x_ref, mod_ref,
                fw_ref, c31w_ref, c31b_ref, gng_ref, gnb_ref, avg_ref, pww_ref,
                poolw_ref, pscale_ref, c3w_ref, sow_ref, wo_ref, n2g_ref, rw_ref, rb_ref,
                x1_ref, h2_ref, lg_ref,
                zext, pext, s2, s4, s8, s16, qext, cat, *, tq, seq_len):
    i = pl.program_id(1)
    has_prev = (i > 0).astype(F32)
    has_next = (i < pl.num_programs(1) - 1).astype(F32)
    rows = tq + 2 * HALO

    def cols(ref, c0):
        return ref[0, :, c0:c0 + GROUP].astype(F32)

    def fill_ext(dst, fn):
        dst[0:HALO, :] = fn(up_ref) * has_prev
        dst[HALO:HALO + tq, :] = fn(u_ref)
        dst[HALO + tq:rows, :] = fn(un_ref) * has_next

    cat[:, 0:GROUP] = _dot(y_ref[0], fw_ref[...]).astype(BF16)

    fill_ext(zext, lambda r: cols(r, U_V) * _sigmoid(cols(r, U_G)))
    conv = jnp.broadcast_to(c31b_ref[...], (tq, GROUP))
    for j in range(CONV31):
        conv = conv + c31w_ref[j:j + 1, :] * zext[pl.ds(HALO - CONV31 // 2 + j, tq), :]
    mu = _dot_hilo(conv, avg_ref[...])
    dev = conv - mu
    var = _dot_hilo(dev * dev, avg_ref[...])
    zn = dev * lax.rsqrt(var + GN_EPS) * gng_ref[...] + gnb_ref[...]
    cat[:, GROUP:2 * GROUP] = _dot((zn * _sigmoid(zn)).astype(BF16), pww_ref[...]).astype(BF16)

    fill_ext(pext, lambda r: cols(r, U_P))
    n = rows - 8
    zeros8 = jnp.zeros((8, GROUP), F32)
    s2[0:n, :] = pext[0:n, :] + pext[pl.ds(1, n), :]
    s2[n:rows, :] = zeros8
    s4[0:n, :] = s2[0:n, :] + s2[pl.ds(2, n), :]
    s4[n:rows, :] = zeros8
    s8[0:n, :] = s4[0:n, :] + s4[pl.ds(4, n), :]
    s8[n:rows, :] = zeros8
    s16[0:n, :] = s8[0:n, :] + s8[pl.ds(8, n), :]
    lane = lax.broadcasted_iota(jnp.int32, (tq, GROUP), 1)
    pos = lax.broadcasted_iota(jnp.int32, (tq, GROUP), 0) + i * tq
    win = jnp.where(lane < HEAD_DIM, s2[pl.ds(HALO - 1, tq), :],
                    jnp.where(lane < 2 * HEAD_DIM, s4[pl.ds(HALO - 2, tq), :],
                              jnp.where(lane < 3 * HEAD_DIM, s8[pl.ds(HALO - 4, tq), :],
                                        s16[pl.ds(HALO - 8, tq), :])))
    half = jnp.where(lane < HEAD_DIM, 1,
                     jnp.where(lane < 2 * HEAD_DIM, 2, jnp.where(lane < 3 * HEAD_DIM, 4, 8)))
    cnt = jnp.minimum(pos + half, seq_len) - jnp.maximum(pos - half, 0)
    dpool = win / cnt.astype(F32) - pext[HALO:HALO + tq, :]
    cat[:, 2 * GROUP:3 * GROUP] = (_dot(dpool.astype(BF16), poolw_ref[...]) * pscale_ref[...]).astype(BF16)

    fill_ext(qext, lambda r: cols(r, U_CG) * cols(r, U_XV))
    c3 = (c3w_ref[0:1, :] * qext[pl.ds(HALO - 1, tq), :]
          + c3w_ref[1:2, :] * qext[HALO:HALO + tq, :]
          + c3w_ref[2:3, :] * qext[pl.ds(HALO + 1, tq), :])
    cat[:, 3 * GROUP:4 * GROUP] = _dot((cols(u_ref, U_BG) * c3).astype(BF16), sow_ref[...]).astype(BF16)

    mixed = _dot(cat[...], wo_ref[...])
    x1 = x_ref[0] + mod_ref[0, 2:3, :] * mixed
    x1_ref[0] = x1
    h2 = _rms_mod(x1, n2g_ref[...], mod_ref[0, 4:5, :], mod_ref[0, 3:4, :])
    h2_ref[0] = h2
    lg_ref[0] = _dot(h2.astype(BF16), rw_ref[...]) + rb_ref[...]


def _mix(y, u, x, mod, wl, tq):
    bsz, s, _ = x.shape
    rows = tq + 2 * HALO
    hb = tq // HALO
    n_hblk = s // HALO
    full = lambda shape: pl.BlockSpec(shape, lambda b, i: tuple(0 for _ in shape))
    kern = functools.partial(_mix_kernel, tq=tq, seq_len=s)
    return pl.pallas_call(
        kern,
        out_shape=(jax.ShapeDtypeStruct((bsz, s, D_MODEL), F32),
                   jax.ShapeDtypeStruct((bsz, s, D_MODEL), F32),
                   jax.ShapeDtypeStruct((bsz, s, LANES), F32)),
        grid=(bsz, s // tq),
        in_specs=[
            pl.BlockSpec((1, tq, GROUP), lambda b, i: (b, i, 0)),
            pl.BlockSpec((1, tq, 6 * GROUP), lambda b, i: (b, i, 0)),
            pl.BlockSpec((1, HALO, 6 * GROUP), lambda b, i: (b, jnp.maximum(i * hb - 1, 0), 0)),
            pl.BlockSpec((1, HALO, 6 * GROUP), lambda b, i: (b, jnp.minimum((i + 1) * hb, n_hblk - 1), 0)),
            pl.BlockSpec((1, tq, D_MODEL), lambda b, i: (b, i, 0)),
            pl.BlockSpec((1, 8, D_MODEL), lambda b, i: (b, 0, 0)),
            full((GROUP, GROUP)),
            full((32, GROUP)),
            full((1, GROUP)),
            full((1, GROUP)),
            full((1, GROUP)),
            full((GROUP, GROUP)),
            full((GROUP, GROUP)),
            full((GROUP, GROUP)),
            full((1, GROUP)),
            full((8, GROUP)),
            full((GROUP, GROUP)),
            full((D_MODEL, D_MODEL)),
            full((1, D_MODEL)),
            full((D_MODEL, LANES)),
            full((1, LANES)),
        ],
        out_specs=(pl.BlockSpec((1, tq, D_MODEL), lambda b, i: (b, i, 0)),
                   pl.BlockSpec((1, tq, D_MODEL), lambda b, i: (b, i, 0)),
                   pl.BlockSpec((1, tq, LANES), lambda b, i: (b, i, 0))),
        scratch_shapes=[pltpu.VMEM((rows, GROUP), F32) for _ in range(7)]
        + [pltpu.VMEM((tq, D_MODEL), BF16)],
        compiler_params=_cparams(("arbitrary", "arbitrary")),
        name="mix",
    )(y, u, u, u, x, mod, wl["fourier_w"], wl["conv31_w"], wl["conv31_b"], wl["gn_g"], wl["gn_b"],
      wl["avg"], wl["pw_w"], wl["pool_w"], wl["pool_scale"], wl["conv3_w"], wl["sconv_out_w"],
      wl["w_o"], wl["norm2_g"], wl["router_w"], wl["router_b"])


def _route_kernel(lg_ref, e_ref, rank_ref, gate_ref, cnt_ref, carry, *, tr):
    i = pl.program_id(0)

    @pl.when(i == 0)
    def _():
        carry[...] = jnp.zeros_like(carry)

    lane = lax.broadcasted_iota(jnp.int32, (tr, LANES), 1).astype(F32)
    neg = jnp.float32(-jnp.inf)
    work = jnp.where(lane < N_EXPERTS, lg_ref[...], neg)
    vals, idxs, hots = [], [], []
    for _ in range(TOP_K):
        m = jnp.max(work, axis=1, keepdims=True)
        idx = jnp.min(jnp.where(work == m, lane, float(LANES)), axis=1, keepdims=True)
        hot = lane == idx
        vals.append(m)
        idxs.append(idx.astype(jnp.int32))
        hots.append(hot)
        work = jnp.where(hot, neg, work)
    exps = [jnp.exp(v - vals[0]) for v in vals]
    denom = exps[0] + exps[1] + exps[2] + exps[3]
    member = sum(h.astype(F32) for h in hots)
    r_io = lax.broadcasted_iota(jnp.int32, (tr, tr), 0)
    c_io = lax.broadcasted_iota(jnp.int32, (tr, tr), 1)
    lower = (c_io < r_io).astype(BF16)
    before = _dot(lower, member.astype(BF16)) + carry[...]
    col = lax.broadcasted_iota(jnp.int32, (tr, TOP_K), 1)
    e_out = jnp.zeros((tr, TOP_K), jnp.int32)
    r_out = jnp.zeros((tr, TOP_K), jnp.int32)
    g_out = jnp.zeros((tr, TOP_K), F32)
    for k in range(TOP_K):
        rk = jnp.sum(jnp.where(hots[k], before, 0.0), axis=1, keepdims=True).astype(jnp.int32)
        e_out = jnp.where(col == k, idxs[k], e_out)
        r_out = jnp.where(col == k, rk, r_out)
        g_out = jnp.where(col == k, exps[k] / denom, g_out)
    e_ref[...] = e_out
    rank_ref[...] = r_out
    gate_ref[...] = g_out
    carry[...] = carry[...] + jnp.sum(member, axis=0, keepdims=True)
    cnt_ref[...] = carry[...]


def _route(logits, tr):
    t_tok = logits.shape[0]
    kern = functools.partial(_route_kernel, tr=tr)
    tile = lambda w: pl.BlockSpec((tr, w), lambda i: (i, 0))
    return pl.pallas_call(
        kern,
        out_shape=(jax.ShapeDtypeStruct((t_tok, TOP_K), jnp.int32),
                   jax.ShapeDtypeStruct((t_tok, TOP_K), jnp.int32),
                   jax.ShapeDtypeStruct((t_tok, TOP_K), F32),
                   jax.ShapeDtypeStruct((1, LANES), F32)),
        grid=(t_tok // tr,),
        in_specs=[tile(LANES)],
        out_specs=(tile(TOP_K), tile(TOP_K), tile(TOP_K), pl.BlockSpec((1, LANES), lambda i: (0, 0))),
        scratch_shapes=[pltpu.VMEM((1, LANES), F32)],
        compiler_params=_cparams(("arbitrary",)),
        name="route",
    )(logits)


def _dispatch_kernel(pend_ref, npad_ref, dest_hbm, h_ref, xs_hbm, idx, zeros, sem_idx, sem_z, sem_row, *, td):
    i = pl.program_id(0)
    n_idx = td * TOP_K
    idx_cp = pltpu.make_async_copy(dest_hbm.at[pl.ds(i * n_idx, n_idx)], idx, sem_idx)
    idx_cp.start()

    def zero_cp(e):
        start = pl.multiple_of(pend_ref[e] - MOE_BLOCK, MOE_BLOCK)
        return pltpu.make_async_copy(zeros, xs_hbm.at[pl.ds(start, MOE_BLOCK)], sem_z)

    @pl.when(i == 0)
    def _():
        zeros[...] = jnp.zeros_like(zeros)

        def start(e, c):
            @pl.when(npad_ref[e] > 0)
            def _():
                zero_cp(e).start()
            return c

        def wait(e, c):
            @pl.when(npad_ref[e] > 0)
            def _():
                zero_cp(e).wait()
            return c

        lax.fori_loop(0, N_EXPERTS, start, 0)
        lax.fori_loop(0, N_EXPERTS, wait, 0)

    idx_cp.wait()

    def row_cp(t, k):
        return pltpu.make_async_copy(h_ref.at[pl.ds(t, 1)], xs_hbm.at[pl.ds(idx[t * TOP_K + k], 1)], sem_row)

    def issue(t, c):
        for k in range(TOP_K):
            row_cp(t, k).start()
        return c

    def drain(t, c):
        for k in range(TOP_K):
            row_cp(t, k).wait()
        return c

    lax.fori_loop(0, td, issue, 0)
    lax.fori_loop(0, td, drain, 0)


def _dispatch(pend, npad, dest, h2, n_slots, td):
    t_tok = h2.shape[0]
    kern = functools.partial(_dispatch_kernel, td=td)
    return pl.pallas_call(
        kern,
        out_shape=jax.ShapeDtypeStruct((n_slots, D_MODEL), F32),
        grid_spec=pltpu.PrefetchScalarGridSpec(
            num_scalar_prefetch=2,
            grid=(t_tok // td,),
            in_specs=[pl.BlockSpec(memory_space=pl.ANY),
                      pl.BlockSpec((td, D_MODEL), lambda i, *_: (i, 0))],
            out_specs=pl.BlockSpec(memory_space=pl.ANY),
            scratch_shapes=[pltpu.SMEM((td * TOP_K,), jnp.int32),
                            pltpu.VMEM((MOE_BLOCK, D_MODEL), F32),
                            pltpu.SemaphoreType.DMA,
                            pltpu.SemaphoreType.DMA,
                            pltpu.SemaphoreType.DMA]),
        compiler_params=_cparams(("arbitrary",)),
        name="dispatch",
    )(pend, npad, dest, h2)


def _experts_kernel(be_ref, nu_ref, xs_ref, wg_ref, bg_ref, wu_ref, bu_ref, wd_ref, bd_ref, o_ref):
    i = pl.program_id(0)

    @pl.when(i < nu_ref[0])
    def _():
        x = xs_ref[...].astype(BF16)
        y = jnp.broadcast_to(bd_ref[0], (MOE_BLOCK, D_MODEL))
        for f in range(D_FF // GROUP):
            fs = slice(f * GROUP, (f + 1) * GROUP)
            gt = jnp.minimum(_dot(x, wg_ref[0, :, fs]) + bg_ref[0, :, fs], SWIGLU_LIMIT)
            up = jnp.clip(_dot(x, wu_ref[0, :, fs]) + bu_ref[0, :, fs], -SWIGLU_LIMIT, SWIGLU_LIMIT)
            act = (up + 1.0) * (gt * _sigmoid(SWIGLU_ALPHA * gt))
            y = y + _dot(act.astype(BF16), wd_ref[0, fs, :])
        o_ref[...] = y

    @pl.when(i >= nu_ref[0])
    def _():
        o_ref[...] = jnp.zeros_like(o_ref)


def _experts(block_e, n_used, xs, wl):
    n_slots = xs.shape[0]
    n_blocks = n_slots // MOE_BLOCK

    def blk(i, be, nu):
        return jnp.minimum(i, nu[0] - 1)

    wspec = lambda: pl.BlockSpec((1, D_MODEL, D_FF), lambda i, be, nu: (be[blk(i, be, nu)], 0, 0))
    bspec = lambda: pl.BlockSpec((1, 1, D_FF), lambda i, be, nu: (be[blk(i, be, nu)], 0, 0))
    return pl.pallas_call(
        _experts_kernel,
        out_shape=jax.ShapeDtypeStruct((n_slots, D_MODEL), F32),
        grid_spec=pltpu.PrefetchScalarGridSpec(
            num_scalar_prefetch=2,
            grid=(n_blocks,),
            in_specs=[pl.BlockSpec((MOE_BLOCK, D_MODEL), lambda i, be, nu: (blk(i, be, nu), 0)),
                      wspec(), bspec(), wspec(), bspec(), wspec(), bspec()],
            out_specs=pl.BlockSpec((MOE_BLOCK, D_MODEL), lambda i, be, nu: (i, 0))),
        compiler_params=_cparams(("arbitrary",)),
        name="experts",
    )(block_e, n_used, xs, wl["exp_gate_w"], wl["exp_gate_b"], wl["exp_up_w"], wl["exp_up_b"],
      wl["exp_down_w"], wl["exp_down_b"])


def _combine_kernel(dest_hbm, gate_ref, x1_ref, mod_ref, fg_ref, ys_hbm, o_ref, idx, buf, sem_idx, sem_row,
                    *, tc, final):
    i = pl.program_id(0)
    n_idx = tc * TOP_K
    idx_cp = pltpu.make_async_copy(dest_hbm.at[pl.ds(i * n_idx, n_idx)], idx, sem_idx)
    idx_cp.start()
    idx_cp.wait()

    def row_cp(t, k):
        return pltpu.make_async_copy(ys_hbm.at[pl.ds(idx[t * TOP_K + k], 1)], buf.at[k, pl.ds(t, 1)], sem_row)

    def issue(t, c):
        for k in range(TOP_K):
            row_cp(t, k).start()
        return c

    def drain(t, c):
        for k in range(TOP_K):
            row_cp(t, k).wait()
        return c

    lax.fori_loop(0, tc, issue, 0)
    lax.fori_loop(0, tc, drain, 0)

    gates = gate_ref[...]
    moe = gates[:, 0:1] * buf[0]
    for k in range(1, TOP_K):
        moe = moe + gates[:, k:k + 1] * buf[k]
    x2 = x1_ref[...] + mod_ref[0, 5:6, :] * moe
    if final:
        ms = jnp.mean(x2 * x2, axis=-1, keepdims=True)
        x2 = x2 * lax.rsqrt(ms + RMS_EPS) * fg_ref[...]
    o_ref[...] = x2


def _combine(dest, gates, x1, mod, final_g, ys, seq_len, tc, final):
    t_tok = x1.shape[0]
    kern = functools.partial(_combine_kernel, tc=tc, final=final)
    return pl.pallas_call(
        kern,
        out_shape=jax.ShapeDtypeStruct((t_tok, D_MODEL), F32),
        grid=(t_tok // tc,),
        in_specs=[pl.BlockSpec(memory_space=pl.ANY),
                  pl.BlockSpec((tc, TOP_K), lambda i: (i, 0)),
                  pl.BlockSpec((tc, D_MODEL), lambda i: (i, 0)),
                  pl.BlockSpec((1, 8, D_MODEL), lambda i: ((i * tc) // seq_len, 0, 0)),
                  pl.BlockSpec((1, D_MODEL), lambda i: (0, 0)),
                  pl.BlockSpec(memory_space=pl.ANY)],
        out_specs=pl.BlockSpec((tc, D_MODEL), lambda i: (i, 0)),
        scratch_shapes=[pltpu.SMEM((tc * TOP_K,), jnp.int32),
                        pltpu.VMEM((TOP_K, tc, D_MODEL), F32),
                        pltpu.SemaphoreType.DMA,
                        pltpu.SemaphoreType.DMA],
        compiler_params=_cparams(("arbitrary",)),
        name="combine",
    )(dest, gates, x1, mod, final_g, ys)


def _dft_tables(n):
    j = jnp.arange(n, dtype=jnp.int32)
    ang = ((j[:, None] * j[None, :]) % n).astype(F32) * (2.0 * math.pi / n)
    scale = 1.0 / math.sqrt(n)
    return jnp.cos(ang) * scale, jnp.sin(ang) * scale


def _block_diag(blocks):
    n, r, c = blocks.shape
    out = jnp.zeros((n * r, n * c), blocks.dtype)
    for g in range(n):
        out = lax.dynamic_update_slice(out, blocks[g], (g * r, g * c))
    return out


def _prep_weights(w):
    depth = w["w_in"].shape[0]
    c64, s64 = _dft_tables(HEAD_DIM)
    n_heads = GROUP // HEAD_DIM
    chan_dft = jnp.concatenate([_block_diag(jnp.stack([c64] * n_heads)),
                                _block_diag(jnp.stack([s64] * n_heads))], axis=1).astype(BF16)
    avg = _block_diag(jnp.full((n_heads, HEAD_DIM, HEAD_DIM), 1.0 / HEAD_DIM, F32)).astype(BF16)
    layers = []
    for l in range(depth):
        layers.append(dict(
            norm1_g=w["norm1_g"][l].reshape(1, D_MODEL),
            norm2_g=w["norm2_g"][l].reshape(1, D_MODEL),
            w_in=w["w_in"][l].astype(BF16),
            fourier_w=w["fourier_w"][l].astype(BF16),
            conv31_w=jnp.pad(w["conv31_w"][l], ((0, 1), (0, 0))),
            conv31_b=w["conv31_b"][l].reshape(1, GROUP),
            gn_g=w["gn_g"][l].reshape(1, GROUP),
            gn_b=w["gn_b"][l].reshape(1, GROUP),
            avg=avg,
            pw_w=w["pw_w"][l].astype(BF16),
            pool_w=_block_diag(w["pool_w"][l]).astype(BF16),
            pool_scale=w["pool_scale"][l].reshape(1, GROUP),
            conv3_w=jnp.pad(w["conv3_w"][l], ((0, 5), (0, 0))),
            sconv_out_w=w["sconv_out_w"][l].astype(BF16),
            w_o=w["w_o"][l].astype(BF16),
            router_w=jnp.pad(w["router_w"][l], ((0, 0), (0, LANES - N_EXPERTS))).astype(BF16),
            router_b=jnp.pad(w["router_b"][l], (0, LANES - N_EXPERTS)).reshape(1, LANES),
            exp_gate_w=w["exp_gate_w"][l].astype(BF16),
            exp_up_w=w["exp_up_w"][l].astype(BF16),
            exp_down_w=w["exp_down_w"][l].astype(BF16),
            exp_gate_b=w["exp_gate_b"][l].reshape(N_EXPERTS, 1, D_FF),
            exp_up_b=w["exp_up_b"][l].reshape(N_EXPERTS, 1, D_FF),
            exp_down_b=w["exp_down_b"][l].reshape(N_EXPERTS, 1, D_MODEL),
        ))
    return chan_dft, layers


def _tiles(bsz, s):
    t_seq = min(512, s)
    t_dft_m = min(s, max(128, (8 * 1024 * 1024) // (bsz * GROUP * 4)))
    t_dft_k = min(1024, s)
    t_tok = min(256, bsz * s)
    return dict(seq=t_seq, dft_m=t_dft_m, dft_k=t_dft_k, route=min(512, bsz * s), tok=t_tok)


def _moe(h2, logits, x1, mod, final_g, wl, seq_len, tiles, final):
    t_tok = h2.shape[0]
    n_assign = t_tok * TOP_K
    n_blocks = -(-n_assign // MOE_BLOCK) + N_EXPERTS
    n_slots = n_blocks * MOE_BLOCK
    e_idx, rank, gates, counts = _route(logits, tiles["route"])
    counts = counts[0, :N_EXPERTS].astype(jnp.int32)
    padded = (counts + MOE_BLOCK - 1) // MOE_BLOCK * MOE_BLOCK
    pend = jnp.cumsum(padded)
    pstart = pend - padded
    npad = padded - counts
    hot = e_idx[..., None] == jnp.arange(N_EXPERTS, dtype=jnp.int32)
    dest = (jnp.sum(jnp.where(hot, pstart, 0), axis=-1) + rank).reshape(n_assign)
    block_start = jnp.arange(n_blocks, dtype=jnp.int32) * MOE_BLOCK
    block_e = jnp.minimum(jnp.searchsorted(pend, block_start, side="right"), N_EXPERTS - 1).astype(jnp.int32)
    n_used = (pend[-1:] // MOE_BLOCK).astype(jnp.int32)
    xs = _dispatch(pend.astype(jnp.int32), npad.astype(jnp.int32), dest, h2, n_slots, tiles["tok"])
    ys = _experts(block_e, n_used, xs, wl)
    return _combine(dest, gates, x1, mod, final_g, ys, seq_len, tiles["tok"], final)


def _encoder(x, c, w, chan_dft, layers, tables, tiles=None):
    bsz, s, _ = x.shape
    tiles = tiles or _tiles(bsz, s)
    cos_t, nsin_t = tables
    depth = len(layers)
    final_g = w["final_g"].reshape(1, D_MODEL)
    for l, wl in enumerate(layers):
        mod = _ada(c, w["ada_w"][l], w["ada_b"][l]).reshape(bsz, 6, D_MODEL)
        mod = jnp.pad(mod, ((0, 0), (0, 2), (0, 0)))
        p, u = _inproj(x, mod, wl["norm1_g"], wl["w_in"], chan_dft, tiles["seq"])
        y = _seqdft(cos_t, nsin_t, p, tiles["dft_m"], tiles["dft_k"])
        x1, h2, logits = _mix(y, u, x, mod, wl, tiles["seq"])
        x = _moe(h2.reshape(bsz * s, D_MODEL), logits.reshape(bsz * s, LANES),
                 x1.reshape(bsz * s, D_MODEL), mod, final_g, wl, s, tiles,
                 final=(l == depth - 1)).reshape(bsz, s, D_MODEL)
    return x


def _seq_tables(s):
    c, sn = _dft_tables(s)
    return c.astype(BF16), (-sn).astype(BF16)


def kernel(x_prompt, x_sample, c_prompt, c_sample, norm1_g, norm2_g, ada_w, ada_b, w_in, fourier_w, conv31_w, conv31_b, gn_g, gn_b, pw_w, pool_w, pool_scale, conv3_w, sconv_out_w, w_o, router_w, router_b, exp_gate_w, exp_gate_b, exp_up_w, exp_up_b, exp_down_w, exp_down_b, final_g):
    w = dict(norm1_g=norm1_g, norm2_g=norm2_g, ada_w=ada_w, ada_b=ada_b, w_in=w_in, fourier_w=fourier_w,
             conv31_w=conv31_w, conv31_b=conv31_b, gn_g=gn_g, gn_b=gn_b, pw_w=pw_w, pool_w=pool_w,
             pool_scale=pool_scale, conv3_w=conv3_w, sconv_out_w=sconv_out_w, w_o=w_o, router_w=router_w,
             router_b=router_b, exp_gate_w=exp_gate_w, exp_gate_b=exp_gate_b, exp_up_w=exp_up_w,
             exp_up_b=exp_up_b, exp_down_w=exp_down_w, exp_down_b=exp_down_b, final_g=final_g)
    chan_dft, layers = _prep_weights(w)
    y_prompt = _encoder(x_prompt, c_prompt, w, chan_dft, layers, _seq_tables(x_prompt.shape[1]))
    y_sample = _encoder(x_sample, c_sample, w, chan_dft, layers, _seq_tables(x_sample.shape[1]))
    return (y_prompt, y_sample)
```

```python
import functools
import math

import jax
import jax.numpy as jnp
from jax import lax
from jax.experimental import pallas as pl
from jax.experimental.pallas import tpu as pltpu

D_MODEL = 1024
GROUP = 256
HEAD_DIM = 64
CONV31 = 31
POOL_WINDOWS = (2, 4, 8, 16)
N_EXPERTS = 32
TOP_K = 4
D_FF = 1024
SWIGLU_ALPHA = 1.702
SWIGLU_LIMIT = 7.0
MOE_BLOCK = 512
RMS_EPS = 1e-6
GN_EPS = 1e-5
HALO = 16
LANES = 128
VMEM_LIMIT = 48 * 1024 * 1024

F32 = jnp.float32
BF16 = jnp.bfloat16


def _cparams(sem):
    return pltpu.CompilerParams(dimension_semantics=sem, vmem_limit_bytes=VMEM_LIMIT)


def _dot(a, b):
    return jnp.dot(a, b, preferred_element_type=F32)


def _split_bf16(x):
    hi = x.astype(BF16)
    lo = (x - hi.astype(F32)).astype(BF16)
    return hi, lo


def _dot_hilo(x, w_bf16):
    hi, lo = _split_bf16(x)
    return _dot(hi, w_bf16) + _dot(lo, w_bf16)


def _sigmoid(x):
    return 1.0 / (1.0 + jnp.exp(-x))


def _ada_kernel(c_ref, w_ref, b_ref, o_ref):
    c = c_ref[...]
    cs = c * _sigmoid(c)
    c_hi, c_lo = _split_bf16(cs)
    w_hi, w_lo = _split_bf16(w_ref[0])
    o_ref[...] = _dot(c_hi, w_hi) + _dot(c_lo, w_hi) + _dot(c_hi, w_lo) + b_ref[0]


def _ada(c, ada_w, ada_b, layer):
    bsz = c.shape[0]
    n_out = ada_w.shape[2]
    return pl.pallas_call(
        _ada_kernel,
        out_shape=jax.ShapeDtypeStruct((bsz, n_out), F32),
        grid=(n_out // D_MODEL,),
        in_specs=[
            pl.BlockSpec((bsz, D_MODEL), lambda j: (0, 0)),
            pl.BlockSpec((1, D_MODEL, D_MODEL), lambda j: (layer, 0, j)),
            pl.BlockSpec((1, 1, D_MODEL), lambda j: (layer, 0, j)),
        ],
        out_specs=pl.BlockSpec((bsz, D_MODEL), lambda j: (0, j)),
        compiler_params=_cparams(("arbitrary",)),
        name="ada",
    )(c, ada_w, ada_b)


def _rms_mod(x, g, scale, shift):
    ms = jnp.mean(x * x, axis=-1, keepdims=True)
    y = x * lax.rsqrt(ms + RMS_EPS) * g
    return y * (1.0 + scale) + shift


def _inproj_kernel(x_ref, mod_ref, g_ref, w_ref, cs_ref, p_ref, u_ref):
    x = x_ref[0]
    h = _rms_mod(x, g_ref[...], mod_ref[0, 1:2, :], mod_ref[0, 0:1, :]).astype(BF16)
    a = _dot(h, w_ref[:, 0:GROUP]).astype(BF16)
    p_ref[0] = _dot(a, cs_ref[...]).astype(BF16)
    for j in range(6):
        u_ref[0, :, j * GROUP:(j + 1) * GROUP] = _dot(
            h, w_ref[:, (j + 1) * GROUP:(j + 2) * GROUP]).astype(BF16)


def _inproj(x, mod, g, w_in, chan_dft, tm):
    bsz, s, _ = x.shape
    return pl.pallas_call(
        _inproj_kernel,
        out_shape=(jax.ShapeDtypeStruct((bsz, s, 2 * GROUP), BF16),
                   jax.ShapeDtypeStruct((bsz, s, 6 * GROUP), BF16)),
        grid=(bsz, s // tm),
        in_specs=[
            pl.BlockSpec((1, tm, D_MODEL), lambda b, i: (b, i, 0)),
            pl.BlockSpec((1, 8, D_MODEL), lambda b, i: (b, 0, 0)),
            pl.BlockSpec((1, D_MODEL), lambda b, i: (0, 0)),
            pl.BlockSpec((D_MODEL, 7 * GROUP), lambda b, i: (0, 0)),
            pl.BlockSpec((GROUP, 2 * GROUP), lambda b, i: (0, 0)),
        ],
        out_specs=(pl.BlockSpec((1, tm, 2 * GROUP), lambda b, i: (b, i, 0)),
                   pl.BlockSpec((1, tm, 6 * GROUP), lambda b, i: (b, i, 0))),
        compiler_params=_cparams(("arbitrary", "arbitrary")),
        name="inproj",
    )(x, mod, g, w_in, chan_dft)


def _seqdft_kernel(c_ref, s_ref, p_ref, o_ref, acc_ref):
    k = pl.program_id(1)
    b = pl.program_id(2)
    contrib = _dot(c_ref[...], p_ref[0, :, 0:GROUP]) + _dot(s_ref[...], p_ref[0, :, GROUP:2 * GROUP])

    @pl.when(k == 0)
    def _():
        acc_ref[b] = contrib

    @pl.when(k > 0)
    def _():
        acc_ref[b] = acc_ref[b] + contrib

    @pl.when(k == pl.num_programs(1) - 1)
    def _():
        o_ref[b] = acc_ref[b].astype(BF16)


def _seqdft(cos_t, nsin_t, p, tm, tk):
    bsz, s, _ = p.shape
    return pl.pallas_call(
        _seqdft_kernel,
        out_shape=jax.ShapeDtypeStruct((bsz, s, GROUP), BF16),
        grid=(s // tm, s // tk, bsz),
        in_specs=[
            pl.BlockSpec((tm, tk), lambda i, k, b: (i, k)),
            pl.BlockSpec((tm, tk), lambda i, k, b: (i, k)),
            pl.BlockSpec((1, tk, 2 * GROUP), lambda i, k, b: (b, k, 0)),
        ],
        out_specs=pl.BlockSpec((bsz, tm, GROUP), lambda i, k, b: (0, i, 0)),
        scratch_shapes=[pltpu.VMEM((bsz, tm, GROUP), F32)],
        compiler_params=_cparams(("arbitrary", "arbitrary", "arbitrary")),
        name="seqdft",
    )(cos_t, nsin_t, p)


U_V, U_G, U_P, U_BG, U_CG, U_XV = (j * GROUP for j in range(6))


def _mix_kernel(y_ref, u_ref, up_ref, un_ref, x_ref, mod_ref,
                fw_ref, c31w_ref, c31b_ref, gng_ref, gnb_ref, avg_ref, pww_ref,
                poolw_ref, pscale_ref, c3w_ref, sow_ref, wo_ref, n2g_ref, rw_ref, rb_ref,
                x1_ref, h2_ref, lg_ref,
                zext, pext, s2, s4, s8, s16, qext, cat, *, tq, seq_len):
    i = pl.program_id(1)
    has_prev = (i > 0).astype(F32)
    has_next = (i < pl.num_programs(1) - 1).astype(F32)
    rows = tq + 2 * HALO

    def cols(ref, c0):
        return ref[0, :, c0:c0 + GROUP].astype(F32)

    def fill_ext(dst, fn):
        dst[0:HALO, :] = fn(up_ref) * has_prev
        dst[HALO:HALO + tq, :] = fn(u_ref)
        dst[HALO + tq:rows, :] = fn(un_ref) * has_next

    cat[:, 0:GROUP] = _dot(y_ref[0], fw_ref[...]).astype(BF16)

    fill_ext(zext, lambda r: cols(r, U_V) * _sigmoid(cols(r, U_G)))
    conv = jnp.broadcast_to(c31b_ref[...], (tq, GROUP))
    for j in range(CONV31):
        conv = conv + c31w_ref[j:j + 1, :] * zext[pl.ds(HALO - CONV31 // 2 + j, tq), :]
    mu = _dot_hilo(conv, avg_ref[...])
    dev = conv - mu
    var = _dot_hilo(dev * dev, avg_ref[...])
    zn = dev * lax.rsqrt(var + GN_EPS) * gng_ref[...] + gnb_ref[...]
    cat[:, GROUP:2 * GROUP] = _dot((zn * _sigmoid(zn)).astype(BF16), pww_ref[...]).astype(BF16)

    fill_ext(pext, lambda r: cols(r, U_P))
    n = rows - 8
    zeros8 = jnp.zeros((8, GROUP), F32)
    s2[0:n, :] = pext[0:n, :] + pext[pl.ds(1, n), :]
    s2[n:rows, :] = zeros8
    s4[0:n, :] = s2[0:n, :] + s2[pl.ds(2, n), :]
    s4[n:rows, :] = zeros8
    s8[0:n, :] = s4[0:n, :] + s4[pl.ds(4, n), :]
    s8[n:rows, :] = zeros8
    s16[0:n, :] = s8[0:n, :] + s8[pl.ds(8, n), :]
    lane = lax.broadcasted_iota(jnp.int32, (tq, GROUP), 1)
    pos = lax.broadcasted_iota(jnp.int32, (tq, GROUP), 0) + i * tq
    win = jnp.where(lane < HEAD_DIM, s2[pl.ds(HALO - 1, tq), :],
                    jnp.where(lane < 2 * HEAD_DIM, s4[pl.ds(HALO - 2, tq), :],
                              jnp.where(lane < 3 * HEAD_DIM, s8[pl.ds(HALO - 4, tq), :],
                                        s16[pl.ds(HALO - 8, tq), :])))
    half = jnp.where(lane < HEAD_DIM, 1,
                     jnp.where(lane < 2 * HEAD_DIM, 2, jnp.where(lane < 3 * HEAD_DIM, 4, 8)))
    cnt = jnp.minimum(pos + half, seq_len) - jnp.maximum(pos - half, 0)
    dpool = win / cnt.astype(F32) - pext[HALO:HALO + tq, :]
    cat[:, 2 * GROUP:3 * GROUP] = (_dot(dpool.astype(BF16), poolw_ref[...]) * pscale_ref[...]).astype(BF16)

    fill_ext(qext, lambda r: cols(r, U_CG) * cols(r, U_XV))
    c3 = (c3w_ref[0:1, :] * qext[pl.ds(HALO - 1, tq), :]
          + c3w_ref[1:2, :] * qext[HALO:HALO + tq, :]
          + c3w_ref[2:3, :] * qext[pl.ds(HALO + 1, tq), :])
    cat[:, 3 * GROUP:4 * GROUP] = _dot((cols(u_ref, U_BG) * c3).astype(BF16), sow_ref[...]).astype(BF16)

    mixed = _dot(cat[...], wo_ref[...])
    x1 = x_ref[0] + mod_ref[0, 2:3, :] * mixed
    x1_ref[0] = x1
    h2 = _rms_mod(x1, n2g_ref[...], mod_ref[0, 4:5, :], mod_ref[0, 3:4, :])
    h2_ref[0] = h2
    lg_ref[0] = _dot(h2.astype(BF16), rw_ref[...]) + rb_ref[...]


def _mix(y, u, x, mod, wl, tq):
    bsz, s, _ = x.shape
    rows = tq + 2 * HALO
    hb = tq // HALO
    n_hblk = s // HALO
    full = lambda shape: pl.BlockSpec(shape, lambda b, i: tuple(0 for _ in shape))
    kern = functools.partial(_mix_kernel, tq=tq, seq_len=s)
    return pl.pallas_call(
        kern,
        out_shape=(jax.ShapeDtypeStruct((bsz, s, D_MODEL), F32),
                   jax.ShapeDtypeStruct((bsz, s, D_MODEL), F32),
                   jax.ShapeDtypeStruct((bsz, s, LANES), F32)),
        grid=(bsz, s // tq),
        in_specs=[
            pl.BlockSpec((1, tq, GROUP), lambda b, i: (b, i, 0)),
            pl.BlockSpec((1, tq, 6 * GROUP), lambda b, i: (b, i, 0)),
            pl.BlockSpec((1, HALO, 6 * GROUP), lambda b, i: (b, jnp.maximum(i * hb - 1, 0), 0)),
            pl.BlockSpec((1, HALO, 6 * GROUP), lambda b, i: (b, jnp.minimum((i + 1) * hb, n_hblk - 1), 0)),
            pl.BlockSpec((1, tq, D_MODEL), lambda b, i: (b, i, 0)),
            pl.BlockSpec((1, 8, D_MODEL), lambda b, i: (b, 0, 0)),
            full((GROUP, GROUP)),
            full((32, GROUP)),
            full((1, GROUP)),
            full((1, GROUP)),
            full((1, GROUP)),
            full((GROUP, GROUP)),
            full((GROUP, GROUP)),
            full((GROUP, GROUP)),
            full((1, GROUP)),
            full((8, GROUP)),
            full((GROUP, GROUP)),
            full((D_MODEL, D_MODEL)),
            full((1, D_MODEL)),
            full((D_MODEL, LANES)),
            full((1, LANES)),
        ],
        out_specs=(pl.BlockSpec((1, tq, D_MODEL), lambda b, i: (b, i, 0)),
                   pl.BlockSpec((1, tq, D_MODEL), lambda b, i: (b, i, 0)),
                   pl.BlockSpec((1, tq, LANES), lambda b, i: (b, i, 0))),
        scratch_shapes=[pltpu.VMEM((rows, GROUP), F32) for _ in range(7)]
        + [pltpu.VMEM((tq, D_MODEL), BF16)],
        compiler_params=_cparams(("arbitrary", "arbitrary")),
        name="mix",
    )(y, u, u, u, x, mod, wl["fourier_w"], wl["conv31_w"], wl["conv31_b"], wl["gn_g"], wl["gn_b"],
      wl["avg"], wl["pw_w"], wl["pool_w"], wl["pool_scale"], wl["conv3_w"], wl["sconv_out_w"],
      wl["w_o"], wl["norm2_g"], wl["router_w"], wl["router_b"])


def _route_kernel(lg_ref, e_ref, rank_ref, gate_ref, base_ref, tcnt_ref, cnt_ref, carry, *, tr):
    i = pl.program_id(0)

    @pl.when(i == 0)
    def _():
        carry[...] = jnp.zeros_like(carry)

    lane = lax.broadcasted_iota(jnp.int32, (tr, LANES), 1).astype(F32)
    neg = jnp.float32(-jnp.inf)
    work = jnp.where(lane < N_EXPERTS, lg_ref[...], neg)
    vals, idxs, hots = [], [], []
    for _ in range(TOP_K):
        m = jnp.max(work, axis=1, keepdims=True)
        idx = jnp.min(jnp.where(work == m, lane, float(LANES)), axis=1, keepdims=True)
        hot = lane == idx
        vals.append(m)
        idxs.append(idx.astype(jnp.int32))
        hots.append(hot)
        work = jnp.where(hot, neg, work)
    exps = [jnp.exp(v - vals[0]) for v in vals]
    denom = exps[0] + exps[1] + exps[2] + exps[3]
    member = sum(h.astype(F32) for h in hots)
    r_io = lax.broadcasted_iota(jnp.int32, (tr, tr), 0)
    c_io = lax.broadcasted_iota(jnp.int32, (tr, tr), 1)
    lower = (c_io < r_io).astype(BF16)
    before = _dot(lower, member.astype(BF16)) + carry[...]
    col = lax.broadcasted_iota(jnp.int32, (tr, TOP_K), 1)
    e_out = jnp.zeros((tr, TOP_K), jnp.int32)
    r_out = jnp.zeros((tr, TOP_K), jnp.int32)
    g_out = jnp.zeros((tr, TOP_K), F32)
    for k in range(TOP_K):
        rk = jnp.sum(jnp.where(hots[k], before, 0.0), axis=1, keepdims=True).astype(jnp.int32)
        e_out = jnp.where(col == k, idxs[k], e_out)
        r_out = jnp.where(col == k, rk, r_out)
        g_out = jnp.where(col == k, exps[k] / denom, g_out)
    e_ref[...] = e_out
    rank_ref[...] = r_out
    gate_ref[...] = g_out
    tile_cnt = jnp.sum(member, axis=0, keepdims=True)
    base_ref[0] = carry[...].astype(jnp.int32)
    tcnt_ref[0] = tile_cnt.astype(jnp.int32)
    carry[...] = carry[...] + tile_cnt
    cnt_ref[...] = carry[...]


def _route(logits, tr):
    t_tok = logits.shape[0]
    n_tiles = t_tok // tr
    kern = functools.partial(_route_kernel, tr=tr)
    tile = lambda w: pl.BlockSpec((tr, w), lambda i: (i, 0))
    per_tile = pl.BlockSpec((1, 1, LANES), lambda i: (i, 0, 0))
    return pl.pallas_call(
        kern,
        out_shape=(jax.ShapeDtypeStruct((t_tok, TOP_K), jnp.int32),
                   jax.ShapeDtypeStruct((t_tok, TOP_K), jnp.int32),
                   jax.ShapeDtypeStruct((t_tok, TOP_K), F32),
                   jax.ShapeDtypeStruct((n_tiles, 1, LANES), jnp.int32),
                   jax.ShapeDtypeStruct((n_tiles, 1, LANES), jnp.int32),
                   jax.ShapeDtypeStruct((1, LANES), F32)),
        grid=(n_tiles,),
        in_specs=[tile(LANES)],
        out_specs=(tile(TOP_K), tile(TOP_K), tile(TOP_K), per_tile, per_tile,
                   pl.BlockSpec((1, LANES), lambda i: (0, 0))),
        scratch_shapes=[pltpu.VMEM((1, LANES), F32)],
        compiler_params=_cparams(("arbitrary",)),
        name="route",
    )(logits)


def _dispatch_kernel(pend_ref, npad_ref, dest_hbm, h_ref, xs_hbm, idx, zeros, sem_idx, sem_z, sem_row, *, td):
    i = pl.program_id(0)
    n_steps = pl.num_programs(0)
    slot = i % 2
    n_idx = td * TOP_K

    def idx_cp(step, sl):
        return pltpu.make_async_copy(dest_hbm.at[pl.ds(step * n_idx, n_idx)], idx.at[sl], sem_idx.at[sl])

    def zero_blk(row):
        return pltpu.make_async_copy(zeros, xs_hbm.at[pl.ds(pl.multiple_of(row, MOE_BLOCK), MOE_BLOCK)], sem_z)

    @pl.when(i == 0)
    def _():
        idx_cp(0, 0).start()
        zeros[...] = jnp.zeros_like(zeros)
        first_unused = pend_ref[N_EXPERTS - 1] // MOE_BLOCK
        n_blocks = xs_hbm.shape[0] // MOE_BLOCK

        def start(e, c):
            @pl.when(npad_ref[e] > 0)
            def _():
                zero_blk(pend_ref[e] - MOE_BLOCK).start()
            return c

        def wait(e, c):
            @pl.when(npad_ref[e] > 0)
            def _():
                zero_blk(pend_ref[e] - MOE_BLOCK).wait()
            return c

        def start_tail(b, c):
            zero_blk(b * MOE_BLOCK).start()
            return c

        def wait_tail(b, c):
            zero_blk(b * MOE_BLOCK).wait()
            return c

        lax.fori_loop(0, N_EXPERTS, start, 0)
        lax.fori_loop(first_unused, n_blocks, start_tail, 0)
        lax.fori_loop(0, N_EXPERTS, wait, 0)
        lax.fori_loop(first_unused, n_blocks, wait_tail, 0)

    @pl.when(i + 1 < n_steps)
    def _():
        idx_cp(i + 1, 1 - slot).start()

    idx_cp(i, slot).wait()

    def issue(t, c):
        for k in range(TOP_K):
            pltpu.make_async_copy(h_ref.at[pl.ds(t, 1)], xs_hbm.at[pl.ds(idx[slot, t * TOP_K + k], 1)],
                                  sem_row).start()
        return c

    lax.fori_loop(0, td, issue, 0, unroll=8)
    for _ in range(TOP_K):
        pltpu.make_async_copy(h_ref, xs_hbm.at[pl.ds(0, td)], sem_row).wait()


def _dispatch(pend, npad, dest, h2, n_slots, td):
    t_tok = h2.shape[0]
    kern = functools.partial(_dispatch_kernel, td=td)
    return pl.pallas_call(
        kern,
        out_shape=jax.ShapeDtypeStruct((n_slots, D_MODEL), F32),
        grid_spec=pltpu.PrefetchScalarGridSpec(
            num_scalar_prefetch=2,
            grid=(t_tok // td,),
            in_specs=[pl.BlockSpec(memory_space=pl.ANY),
                      pl.BlockSpec((td, D_MODEL), lambda i, *_: (i, 0))],
            out_specs=pl.BlockSpec(memory_space=pl.ANY),
            scratch_shapes=[pltpu.SMEM((2, td * TOP_K), jnp.int32),
                            pltpu.VMEM((MOE_BLOCK, D_MODEL), F32),
                            pltpu.SemaphoreType.DMA((2,)),
                            pltpu.SemaphoreType.DMA,
                            pltpu.SemaphoreType.DMA]),
        compiler_params=_cparams(("arbitrary",)),
        name="dispatch",
    )(pend, npad, dest, h2)


def _experts_kernel(be_ref, nu_ref, xs_ref, wg_ref, bg_ref, wu_ref, bu_ref, wd_ref, bd_ref, o_ref, act):
    i = pl.program_id(0)

    @pl.when(i < nu_ref[0])
    def _():
        x = xs_ref[...].astype(BF16)
        for f in range(D_FF // GROUP):
            fs = slice(f * GROUP, (f + 1) * GROUP)
            gt = jnp.minimum(_dot(x, wg_ref[0, :, fs]) + bg_ref[0, :, fs], SWIGLU_LIMIT)
            up = jnp.clip(_dot(x, wu_ref[0, :, fs]) + bu_ref[0, :, fs], -SWIGLU_LIMIT, SWIGLU_LIMIT)
            act[:, fs] = ((up + 1.0) * (gt * _sigmoid(SWIGLU_ALPHA * gt))).astype(BF16)
        o_ref[...] = (_dot(act[...], wd_ref[0]) + bd_ref[0]).astype(BF16)

    @pl.when(i >= nu_ref[0])
    def _():
        o_ref[...] = jnp.zeros_like(o_ref)


def _experts(block_e, n_used, xs, wl, layer):
    n_slots = xs.shape[0]
    n_blocks = n_slots // MOE_BLOCK

    def expert(i, be, nu):
        return layer * N_EXPERTS + be[jnp.minimum(i, nu[0] - 1)]

    wspec = lambda: pl.BlockSpec((1, D_MODEL, D_FF), lambda i, be, nu: (expert(i, be, nu), 0, 0))
    bspec = lambda: pl.BlockSpec((1, 1, D_FF), lambda i, be, nu: (expert(i, be, nu), 0, 0))
    return pl.pallas_call(
        _experts_kernel,
        out_shape=jax.ShapeDtypeStruct((n_slots, D_MODEL), BF16),
        grid_spec=pltpu.PrefetchScalarGridSpec(
            num_scalar_prefetch=2,
            grid=(n_blocks,),
            in_specs=[pl.BlockSpec((MOE_BLOCK, D_MODEL), lambda i, be, nu: (jnp.minimum(i, nu[0] - 1), 0)),
                      wspec(), bspec(), wspec(), bspec(), wspec(), bspec()],
            out_specs=pl.BlockSpec((MOE_BLOCK, D_MODEL), lambda i, be, nu: (i, 0)),
            scratch_shapes=[pltpu.VMEM((MOE_BLOCK, D_FF), BF16)]),
        compiler_params=_cparams(("arbitrary",)),
        name="experts",
    )(block_e, n_used, xs, wl["exp_gate_w"], wl["exp_gate_b"], wl["exp_up_w"], wl["exp_up_b"],
      wl["exp_down_w"], wl["exp_down_b"])


SEG_CAP = 48
SEG_WIN = SEG_CAP + 16


def _combine_kernel(seg_ref, rounds_ref, e_ref, rank_ref, gate_ref, base_ref, pstart_ref, x1_ref, mod_ref,
                    fg_ref, ys_hbm, o_ref, segbuf, acc, sems, *, tc, n_slots, final):
    j = pl.program_id(0)
    n_tiles = pl.num_programs(0)
    slot = j % 2

    def window_start(s):
        return jnp.minimum(s - (s & 15), n_slots - SEG_WIN)

    def seg_cp(tile, e, r, sl):
        a = pl.multiple_of(window_start(seg_ref[tile * N_EXPERTS + e] + r * SEG_CAP), 16)
        return pltpu.make_async_copy(ys_hbm.at[pl.ds(a, SEG_WIN)], segbuf.at[sl, e], sems.at[sl])

    def issue(tile, r, sl):
        for e in range(N_EXPERTS):
            seg_cp(tile, e, r, sl).start()

    def drain(tile, r, sl):
        for e in range(N_EXPERTS):
            seg_cp(tile, e, r, sl).wait()

    @pl.when(j == 0)
    def _():
        issue(0, 0, 0)

    @pl.when(j + 1 < n_tiles)
    def _():
        issue(j + 1, 0, 1 - slot)

    lane_row = lax.broadcasted_iota(jnp.int32, (1, LANES), 1)
    lane_tc = lax.broadcasted_iota(jnp.int32, (tc, LANES), 1)
    col_io = lax.broadcasted_iota(jnp.int32, (tc, N_EXPERTS * SEG_WIN), 1)
    pstart_v = pstart_ref[...]
    base_v = base_ref[0]

    def contribution(r, sl):
        lo_v = base_v + r * SEG_CAP
        off_v = (lane_row * SEG_WIN - window_start(pstart_v + lo_v) + pstart_v).astype(F32)
        lo_f = lo_v.astype(F32)
        g = jnp.zeros((tc, N_EXPERTS * SEG_WIN), F32)
        for k in range(TOP_K):
            sel = lane_tc == e_ref[:, k:k + 1]
            off_k = jnp.sum(jnp.where(sel, off_v, 0.0), axis=1, keepdims=True).astype(jnp.int32)
            lo_k = jnp.sum(jnp.where(sel, lo_f, 0.0), axis=1, keepdims=True).astype(jnp.int32)
            rk = rank_ref[:, k:k + 1]
            col_k = jnp.where((rk >= lo_k) & (rk < lo_k + SEG_CAP), rk + off_k, -1)
            g = g + jnp.where(col_io == col_k, gate_ref[:, k:k + 1], 0.0)
        seg = segbuf[sl].reshape(N_EXPERTS * SEG_WIN, D_MODEL)
        return _dot(g.astype(BF16), seg)

    drain(j, 0, slot)
    acc[...] = contribution(0, slot)

    def extra_round(r, c):
        issue(j, r, slot)
        drain(j, r, slot)
        acc[...] = acc[...] + contribution(r, slot)
        return c

    lax.fori_loop(1, rounds_ref[j], extra_round, 0)

    x2 = x1_ref[...] + mod_ref[0, 5:6, :] * acc[...]
    if final:
        ms = jnp.mean(x2 * x2, axis=-1, keepdims=True)
        x2 = x2 * lax.rsqrt(ms + RMS_EPS) * fg_ref[...]
    o_ref[...] = x2


def _combine(seg_start, rounds, e_idx, rank, gates, base, pstart_v, x1, mod, final_g, ys, seq_len, tc, final):
    t_tok = x1.shape[0]
    n_slots = ys.shape[0]
    kern = functools.partial(_combine_kernel, tc=tc, n_slots=n_slots, final=final)
    tok4 = lambda: pl.BlockSpec((tc, TOP_K), lambda i, *_: (i, 0))
    return pl.pallas_call(
        kern,
        out_shape=jax.ShapeDtypeStruct((t_tok, D_MODEL), F32),
        grid_spec=pltpu.PrefetchScalarGridSpec(
            num_scalar_prefetch=2,
            grid=(t_tok // tc,),
            in_specs=[tok4(), tok4(), tok4(),
                      pl.BlockSpec((1, 1, LANES), lambda i, *_: (i, 0, 0)),
                      pl.BlockSpec((1, LANES), lambda i, *_: (0, 0)),
                      pl.BlockSpec((tc, D_MODEL), lambda i, *_: (i, 0)),
                      pl.BlockSpec((1, 8, D_MODEL), lambda i, *_: ((i * tc) // seq_len, 0, 0)),
                      pl.BlockSpec((1, D_MODEL), lambda i, *_: (0, 0)),
                      pl.BlockSpec(memory_space=pl.ANY)],
            out_specs=pl.BlockSpec((tc, D_MODEL), lambda i, *_: (i, 0)),
            scratch_shapes=[pltpu.VMEM((2, N_EXPERTS, SEG_WIN, D_MODEL), BF16),
                            pltpu.VMEM((tc, D_MODEL), F32),
                            pltpu.SemaphoreType.DMA((2,))]),
        compiler_params=_cparams(("arbitrary",)),
        name="combine",
    )(seg_start, rounds, e_idx, rank, gates, base, pstart_v, x1, mod, final_g, ys)


def _dft_tables(n):
    j = jnp.arange(n, dtype=jnp.int32)
    ang = ((j[:, None] * j[None, :]) % n).astype(F32) * (2.0 * math.pi / n)
    scale = 1.0 / math.sqrt(n)
    return jnp.cos(ang) * scale, jnp.sin(ang) * scale


def _block_diag(blocks):
    n, r, c = blocks.shape
    rows = lax.broadcasted_iota(jnp.int32, (n * r, n * c), 0) // r
    cols = lax.broadcasted_iota(jnp.int32, (n * r, n * c), 1) // c
    return jnp.where(rows == cols, jnp.tile(blocks.reshape(n * r, c), (1, n)), 0)


def _prep_weights(w):
    depth = w["w_in"].shape[0]
    c64, s64 = _dft_tables(HEAD_DIM)
    n_heads = GROUP // HEAD_DIM
    chan_dft = jnp.concatenate([_block_diag(jnp.stack([c64] * n_heads)),
                                _block_diag(jnp.stack([s64] * n_heads))], axis=1).astype(BF16)
    avg = _block_diag(jnp.full((n_heads, HEAD_DIM, HEAD_DIM), 1.0 / HEAD_DIM, F32)).astype(BF16)
    stacked = dict(
        exp_gate_w=w["exp_gate_w"].astype(BF16).reshape(depth * N_EXPERTS, D_MODEL, D_FF),
        exp_up_w=w["exp_up_w"].astype(BF16).reshape(depth * N_EXPERTS, D_MODEL, D_FF),
        exp_down_w=w["exp_down_w"].astype(BF16).reshape(depth * N_EXPERTS, D_FF, D_MODEL),
        exp_gate_b=w["exp_gate_b"].reshape(depth * N_EXPERTS, 1, D_FF),
        exp_up_b=w["exp_up_b"].reshape(depth * N_EXPERTS, 1, D_FF),
        exp_down_b=w["exp_down_b"].reshape(depth * N_EXPERTS, 1, D_MODEL))
    layers = []
    for l in range(depth):
        layers.append(dict(
            stacked,
            norm1_g=w["norm1_g"][l].reshape(1, D_MODEL),
            norm2_g=w["norm2_g"][l].reshape(1, D_MODEL),
            w_in=w["w_in"][l].astype(BF16),
            fourier_w=w["fourier_w"][l].astype(BF16),
            conv31_w=jnp.pad(w["conv31_w"][l], ((0, 1), (0, 0))),
            conv31_b=w["conv31_b"][l].reshape(1, GROUP),
            gn_g=w["gn_g"][l].reshape(1, GROUP),
            gn_b=w["gn_b"][l].reshape(1, GROUP),
            avg=avg,
            pw_w=w["pw_w"][l].astype(BF16),
            pool_w=_block_diag(w["pool_w"][l]).astype(BF16),
            pool_scale=w["pool_scale"][l].reshape(1, GROUP),
            conv3_w=jnp.pad(w["conv3_w"][l], ((0, 5), (0, 0))),
            sconv_out_w=w["sconv_out_w"][l].astype(BF16),
            w_o=w["w_o"][l].astype(BF16),
            router_w=jnp.pad(w["router_w"][l], ((0, 0), (0, LANES - N_EXPERTS))).astype(BF16),
            router_b=jnp.pad(w["router_b"][l], (0, LANES - N_EXPERTS)).reshape(1, LANES),
        ))
    return chan_dft, layers


def _tiles(bsz, s):
    t_seq = min(512, s)
    t_dft_m = min(s, max(128, (8 * 1024 * 1024) // (bsz * GROUP * 4)))
    t_dft_k = min(1024, s)
    return dict(seq=t_seq, dft_m=t_dft_m, dft_k=t_dft_k, tok=min(256, bsz * s))


def _moe(h2, logits, x1, mod, final_g, wl, layer, seq_len, tiles, final):
    t_tok = h2.shape[0]
    tok = tiles["tok"]
    n_assign = t_tok * TOP_K
    n_blocks = -(-n_assign // MOE_BLOCK) + N_EXPERTS
    n_slots = n_blocks * MOE_BLOCK
    e_idx, rank, gates, base, tile_cnt, counts = _route(logits, tok)
    counts = counts[0, :N_EXPERTS].astype(jnp.int32)
    padded = (counts + MOE_BLOCK - 1) // MOE_BLOCK * MOE_BLOCK
    pend = jnp.cumsum(padded)
    pstart = pend - padded
    npad = padded - counts
    hot = e_idx[..., None] == jnp.arange(N_EXPERTS, dtype=jnp.int32)
    dest = (jnp.sum(jnp.where(hot, pstart, 0), axis=-1) + rank).reshape(n_assign)
    block_start = jnp.arange(n_blocks, dtype=jnp.int32) * MOE_BLOCK
    block_e = jnp.minimum(jnp.sum(block_start[:, None] >= pend[None, :], axis=1), N_EXPERTS - 1).astype(jnp.int32)
    n_used = (pend[-1:] // MOE_BLOCK).astype(jnp.int32)
    seg_start = (pstart[None, :] + base[:, 0, :N_EXPERTS]).reshape(-1).astype(jnp.int32)
    rounds = jnp.maximum(-(-jnp.max(tile_cnt[:, 0, :N_EXPERTS], axis=1) // SEG_CAP), 1).astype(jnp.int32)
    pstart_v = jnp.pad(pstart, (0, LANES - N_EXPERTS)).reshape(1, LANES).astype(jnp.int32)
    xs = _dispatch(pend.astype(jnp.int32), npad.astype(jnp.int32), dest, h2, n_slots, tok)
    ys = _experts(block_e, n_used, xs, wl, layer)
    return _combine(seg_start, rounds, e_idx, rank, gates, base, pstart_v, x1, mod, final_g, ys,
                    seq_len, tok, final)


def _encoder(x, c, w, chan_dft, layers, tables, tiles=None):
    bsz, s, _ = x.shape
    tiles = tiles or _tiles(bsz, s)
    cos_t, nsin_t = tables
    depth = len(layers)
    final_g = w["final_g"].reshape(1, D_MODEL)
    ada_b = w["ada_b"].reshape(depth, 1, 6 * D_MODEL)
    for l, wl in enumerate(layers):
        mod = _ada(c, w["ada_w"], ada_b, l).reshape(bsz, 6, D_MODEL)
        mod = jnp.pad(mod, ((0, 0), (0, 2), (0, 0)))
        p, u = _inproj(x, mod, wl["norm1_g"], wl["w_in"], chan_dft, tiles["seq"])
        y = _seqdft(cos_t, nsin_t, p, tiles["dft_m"], tiles["dft_k"])
        x1, h2, logits = _mix(y, u, x, mod, wl, tiles["seq"])
        x = _moe(h2.reshape(bsz * s, D_MODEL), logits.reshape(bsz * s, LANES),
                 x1.reshape(bsz * s, D_MODEL), mod, final_g, wl, l, s, tiles,
                 final=(l == depth - 1)).reshape(bsz, s, D_MODEL)
    return x


def _seq_tables(s):
    c, sn = _dft_tables(s)
    return c.astype(BF16), (-sn).astype(BF16)


def kernel(x_prompt, x_sample, c_prompt, c_sample, norm1_g, norm2_g, ada_w, ada_b, w_in, fourier_w, conv31_w, conv31_b, gn_g, gn_b, pw_w, pool_w, pool_scale, conv3_w, sconv_out_w, w_o, router_w, router_b, exp_gate_w, exp_gate_b, exp_up_w, exp_up_b, exp_down_w, exp_down_b, final_g):
    w = dict(norm1_g=norm1_g, norm2_g=norm2_g, ada_w=ada_w, ada_b=ada_b, w_in=w_in, fourier_w=fourier_w,
             conv31_w=conv31_w, conv31_b=conv31_b, gn_g=gn_g, gn_b=gn_b, pw_w=pw_w, pool_w=pool_w,
             pool_scale=pool_scale, conv3_w=conv3_w, sconv_out_w=sconv_out_w, w_o=w_o, router_w=router_w,
             router_b=router_b, exp_gate_w=exp_gate_w, exp_gate_b=exp_gate_b, exp_up_w=exp_up_w,
             exp_up_b=exp_up_b, exp_down_w=exp_down_w, exp_down_b=exp_down_b, final_g=final_g)
    chan_dft, layers = _prep_weights(w)
    y_prompt = _encoder(x_prompt, c_prompt, w, chan_dft, layers, _seq_tables(x_prompt.shape[1]))
    y_sample = _encoder(x_sample, c_sample, w, chan_dft, layers, _seq_tables(x_sample.shape[1]))
    return (y_prompt, y_sample)
```

```python
import functools
import math

import jax
import jax.numpy as jnp
from jax import lax
from jax.experimental import pallas as pl
from jax.experimental.pallas import tpu as pltpu

D_MODEL = 1024
GROUP = 256
HEAD_DIM = 64
CONV31 = 31
POOL_WINDOWS = (2, 4, 8, 16)
N_EXPERTS = 32
TOP_K = 4
D_FF = 1024
SWIGLU_ALPHA = 1.702
SWIGLU_LIMIT = 7.0
MOE_BLOCK = 512
RMS_EPS = 1e-6
GN_EPS = 1e-5
HALO = 16
LANES = 128
VMEM_LIMIT = 48 * 1024 * 1024

F32 = jnp.float32
BF16 = jnp.bfloat16


def _cparams(sem):
    return pltpu.CompilerParams(dimension_semantics=sem, vmem_limit_bytes=VMEM_LIMIT)


def _dot(a, b):
    return jnp.dot(a, b, preferred_element_type=F32)


def _split_bf16(x):
    hi = x.astype(BF16)
    lo = (x - hi.astype(F32)).astype(BF16)
    return hi, lo


def _dot_hilo(x, w_bf16):
    hi, lo = _split_bf16(x)
    return _dot(hi, w_bf16) + _dot(lo, w_bf16)


def _sigmoid(x):
    return 1.0 / (1.0 + jnp.exp(-x))


def _ada_kernel(c_ref, w_ref, b_ref, o_ref):
    c = c_ref[...]
    cs = c * _sigmoid(c)
    c_hi, c_lo = _split_bf16(cs)
    w_hi, w_lo = _split_bf16(w_ref[0])
    o_ref[...] = _dot(c_hi, w_hi) + _dot(c_lo, w_hi) + _dot(c_hi, w_lo) + b_ref[0]


def _ada(c, ada_w, ada_b, layer):
    bsz = c.shape[0]
    n_out = ada_w.shape[2]
    return pl.pallas_call(
        _ada_kernel,
        out_shape=jax.ShapeDtypeStruct((bsz, n_out), F32),
        grid=(n_out // D_MODEL,),
        in_specs=[
            pl.BlockSpec((bsz, D_MODEL), lambda j: (0, 0)),
            pl.BlockSpec((1, D_MODEL, D_MODEL), lambda j: (layer, 0, j)),
            pl.BlockSpec((1, 1, D_MODEL), lambda j: (layer, 0, j)),
        ],
        out_specs=pl.BlockSpec((bsz, D_MODEL), lambda j: (0, j)),
        compiler_params=_cparams(("arbitrary",)),
        name="ada",
    )(c, ada_w, ada_b)


def _rms_mod(x, g, scale, shift):
    ms = jnp.mean(x * x, axis=-1, keepdims=True)
    y = x * lax.rsqrt(ms + RMS_EPS) * g
    return y * (1.0 + scale) + shift


def _inproj_kernel(x_ref, mod_ref, g_ref, w_ref, cs_ref, p_ref, u_ref):
    x = x_ref[0]
    h = _rms_mod(x, g_ref[...], mod_ref[0, 1:2, :], mod_ref[0, 0:1, :]).astype(BF16)
    a = _dot(h, w_ref[:, 0:GROUP]).astype(BF16)
    p_ref[0] = _dot(a, cs_ref[...]).astype(BF16)
    for j in range(6):
        u_ref[0, :, j * GROUP:(j + 1) * GROUP] = _dot(
            h, w_ref[:, (j + 1) * GROUP:(j + 2) * GROUP]).astype(BF16)


def _inproj(x, mod, g, w_in, chan_dft, tm):
    bsz, s, _ = x.shape
    return pl.pallas_call(
        _inproj_kernel,
        out_shape=(jax.ShapeDtypeStruct((bsz, s, 2 * GROUP), BF16),
                   jax.ShapeDtypeStruct((bsz, s, 6 * GROUP), BF16)),
        grid=(bsz, s // tm),
        in_specs=[
            pl.BlockSpec((1, tm, D_MODEL), lambda b, i: (b, i, 0)),
            pl.BlockSpec((1, 8, D_MODEL), lambda b, i: (b, 0, 0)),
            pl.BlockSpec((1, D_MODEL), lambda b, i: (0, 0)),
            pl.BlockSpec((D_MODEL, 7 * GROUP), lambda b, i: (0, 0)),
            pl.BlockSpec((GROUP, 2 * GROUP), lambda b, i: (0, 0)),
        ],
        out_specs=(pl.BlockSpec((1, tm, 2 * GROUP), lambda b, i: (b, i, 0)),
                   pl.BlockSpec((1, tm, 6 * GROUP), lambda b, i: (b, i, 0))),
        compiler_params=_cparams(("arbitrary", "arbitrary")),
        name="inproj",
    )(x, mod, g, w_in, chan_dft)


def _seqdft_kernel(c_ref, s_ref, p_ref, o_ref, acc_ref):
    k = pl.program_id(1)
    b = pl.program_id(2)
    contrib = _dot(c_ref[...], p_ref[0, :, 0:GROUP]) + _dot(s_ref[...], p_ref[0, :, GROUP:2 * GROUP])

    @pl.when(k == 0)
    def _():
        acc_ref[b] = contrib

    @pl.when(k > 0)
    def _():
        acc_ref[b] = acc_ref[b] + contrib

    @pl.when(k == pl.num_programs(1) - 1)
    def _():
        o_ref[b] = acc_ref[b].astype(BF16)


def _seqdft(cos_t, nsin_t, p, tm, tk):
    bsz, s, _ = p.shape
    return pl.pallas_call(
        _seqdft_kernel,
        out_shape=jax.ShapeDtypeStruct((bsz, s, GROUP), BF16),
        grid=(s // tm, s // tk, bsz),
        in_specs=[
            pl.BlockSpec((tm, tk), lambda i, k, b: (i, k)),
            pl.BlockSpec((tm, tk), lambda i, k, b: (i, k)),
            pl.BlockSpec((1, tk, 2 * GROUP), lambda i, k, b: (b, k, 0)),
        ],
        out_specs=pl.BlockSpec((bsz, tm, GROUP), lambda i, k, b: (0, i, 0)),
        scratch_shapes=[pltpu.VMEM((bsz, tm, GROUP), F32)],
        compiler_params=_cparams(("arbitrary", "arbitrary", "arbitrary")),
        name="seqdft",
    )(cos_t, nsin_t, p)


U_V, U_G, U_P, U_BG, U_CG, U_XV = (j * GROUP for j in range(6))


def _mix_kernel(y_ref, u_ref, up_ref, un_ref, x_ref, mod_ref,
                fw_ref, c31w_ref, c31b_ref, gng_ref, gnb_ref, avg_ref, pww_ref,
                poolw_ref, pscale_ref, c3w_ref, sow_ref, wo_ref, n2g_ref, rw_ref, rb_ref,
                x1_ref, h2_ref, lg_ref,
                zext, pext, s2, s4, s8, s16, qext, cat, *, tq, seq_len):
    i = pl.program_id(1)
    has_prev = (i > 0).astype(F32)
    has_next = (i < pl.num_programs(1) - 1).astype(F32)
    rows = tq + 2 * HALO

    def cols(ref, c0):
        return ref[0, :, c0:c0 + GROUP].astype(F32)

    def fill_ext(dst, fn):
        dst[0:HALO, :] = fn(up_ref) * has_prev
        dst[HALO:HALO + tq, :] = fn(u_ref)
        dst[HALO + tq:rows, :] = fn(un_ref) * has_next

    cat[:, 0:GROUP] = _dot(y_ref[0], fw_ref[...]).astype(BF16)

    fill_ext(zext, lambda r: cols(r, U_V) * _sigmoid(cols(r, U_G)))
    conv = jnp.broadcast_to(c31b_ref[...], (tq, GROUP))
    for j in range(CONV31):
        conv = conv + c31w_ref[j:j + 1, :] * zext[pl.ds(HALO - CONV31 // 2 + j, tq), :]
    mu = _dot_hilo(conv, avg_ref[...])
    dev = conv - mu
    var = _dot_hilo(dev * dev, avg_ref[...])
    zn = dev * lax.rsqrt(var + GN_EPS) * gng_ref[...] + gnb_ref[...]
    cat[:, GROUP:2 * GROUP] = _dot((zn * _sigmoid(zn)).astype(BF16), pww_ref[...]).astype(BF16)

    fill_ext(pext, lambda r: cols(r, U_P))
    n = rows - 8
    zeros8 = jnp.zeros((8, GROUP), F32)
    s2[0:n, :] = pext[0:n, :] + pext[pl.ds(1, n), :]
    s2[n:rows, :] = zeros8
    s4[0:n, :] = s2[0:n, :] + s2[pl.ds(2, n), :]
    s4[n:rows, :] = zeros8
    s8[0:n, :] = s4[0:n, :] + s4[pl.ds(4, n), :]
    s8[n:rows, :] = zeros8
    s16[0:n, :] = s8[0:n, :] + s8[pl.ds(8, n), :]
    lane = lax.broadcasted_iota(jnp.int32, (tq, GROUP), 1)
    pos = lax.broadcasted_iota(jnp.int32, (tq, GROUP), 0) + i * tq
    win = jnp.where(lane < HEAD_DIM, s2[pl.ds(HALO - 1, tq), :],
                    jnp.where(lane < 2 * HEAD_DIM, s4[pl.ds(HALO - 2, tq), :],
                              jnp.where(lane < 3 * HEAD_DIM, s8[pl.ds(HALO - 4, tq), :],
                                        s16[pl.ds(HALO - 8, tq), :])))
    half = jnp.where(lane < HEAD_DIM, 1,
                     jnp.where(lane < 2 * HEAD_DIM, 2, jnp.where(lane < 3 * HEAD_DIM, 4, 8)))
    cnt = jnp.minimum(pos + half, seq_len) - jnp.maximum(pos - half, 0)
    dpool = win / cnt.astype(F32) - pext[HALO:HALO + tq, :]
    cat[:, 2 * GROUP:3 * GROUP] = (_dot(dpool.astype(BF16), poolw_ref[...]) * pscale_ref[...]).astype(BF16)

    fill_ext(qext, lambda r: cols(r, U_CG) * cols(r, U_XV))
    c3 = (c3w_ref[0:1, :] * qext[pl.ds(HALO - 1, tq), :]
          + c3w_ref[1:2, :] * qext[HALO:HALO + tq, :]
          + c3w_ref[2:3, :] * qext[pl.ds(HALO + 1, tq), :])
    cat[:, 3 * GROUP:4 * GROUP] = _dot((cols(u_ref, U_BG) * c3).astype(BF16), sow_ref[...]).astype(BF16)

    mixed = _dot(cat[...], wo_ref[...])
    x1 = x_ref[0] + mod_ref[0, 2:3, :] * mixed
    x1_ref[0] = x1
    h2 = _rms_mod(x1, n2g_ref[...], mod_ref[0, 4:5, :], mod_ref[0, 3:4, :])
    h2_ref[0] = h2
    lg_ref[0] = _dot(h2.astype(BF16), rw_ref[...]) + rb_ref[...]


def _mix(y, u, x, mod, wl, tq):
    bsz, s, _ = x.shape
    rows = tq + 2 * HALO
    hb = tq // HALO
    n_hblk = s // HALO
    full = lambda shape: pl.BlockSpec(shape, lambda b, i: tuple(0 for _ in shape))
    kern = functools.partial(_mix_kernel, tq=tq, seq_len=s)
    return pl.pallas_call(
        kern,
        out_shape=(jax.ShapeDtypeStruct((bsz, s, D_MODEL), F32),
                   jax.ShapeDtypeStruct((bsz, s, D_MODEL), F32),
                   jax.ShapeDtypeStruct((bsz, s, LANES), F32)),
        grid=(bsz, s // tq),
        in_specs=[
            pl.BlockSpec((1, tq, GROUP), lambda b, i: (b, i, 0)),
            pl.BlockSpec((1, tq, 6 * GROUP), lambda b, i: (b, i, 0)),
            pl.BlockSpec((1, HALO, 6 * GROUP), lambda b, i: (b, jnp.maximum(i * hb - 1, 0), 0)),
            pl.BlockSpec((1, HALO, 6 * GROUP), lambda b, i: (b, jnp.minimum((i + 1) * hb, n_hblk - 1), 0)),
            pl.BlockSpec((1, tq, D_MODEL), lambda b, i: (b, i, 0)),
            pl.BlockSpec((1, 8, D_MODEL), lambda b, i: (b, 0, 0)),
            full((GROUP, GROUP)),
            full((32, GROUP)),
            full((1, GROUP)),
            full((1, GROUP)),
            full((1, GROUP)),
            full((GROUP, GROUP)),
            full((GROUP, GROUP)),
            full((GROUP, GROUP)),
            full((1, GROUP)),
            full((8, GROUP)),
            full((GROUP, GROUP)),
            full((D_MODEL, D_MODEL)),
            full((1, D_MODEL)),
            full((D_MODEL, LANES)),
            full((1, LANES)),
        ],
        out_specs=(pl.BlockSpec((1, tq, D_MODEL), lambda b, i: (b, i, 0)),
                   pl.BlockSpec((1, tq, D_MODEL), lambda b, i: (b, i, 0)),
                   pl.BlockSpec((1, tq, LANES), lambda b, i: (b, i, 0))),
        scratch_shapes=[pltpu.VMEM((rows, GROUP), F32) for _ in range(7)]
        + [pltpu.VMEM((tq, D_MODEL), BF16)],
        compiler_params=_cparams(("arbitrary", "arbitrary")),
        name="mix",
    )(y, u, u, u, x, mod, wl["fourier_w"], wl["conv31_w"], wl["conv31_b"], wl["gn_g"], wl["gn_b"],
      wl["avg"], wl["pw_w"], wl["pool_w"], wl["pool_scale"], wl["conv3_w"], wl["sconv_out_w"],
      wl["w_o"], wl["norm2_g"], wl["router_w"], wl["router_b"])


def _route_kernel(lg_ref, e_ref, rank_ref, gate_ref, base_ref, tcnt_ref, cnt_ref, carry, *, tr):
    i = pl.program_id(0)

    @pl.when(i == 0)
    def _():
        carry[...] = jnp.zeros_like(carry)

    lane = lax.broadcasted_iota(jnp.int32, (tr, LANES), 1).astype(F32)
    neg = jnp.float32(-jnp.inf)
    work = jnp.where(lane < N_EXPERTS, lg_ref[...], neg)
    vals, idxs, hots = [], [], []
    for _ in range(TOP_K):
        m = jnp.max(work, axis=1, keepdims=True)
        idx = jnp.min(jnp.where(work == m, lane, float(LANES)), axis=1, keepdims=True)
        hot = lane == idx
        vals.append(m)
        idxs.append(idx.astype(jnp.int32))
        hots.append(hot)
        work = jnp.where(hot, neg, work)
    exps = [jnp.exp(v - vals[0]) for v in vals]
    denom = exps[0] + exps[1] + exps[2] + exps[3]
    member = sum(h.astype(F32) for h in hots)
    r_io = lax.broadcasted_iota(jnp.int32, (tr, tr), 0)
    c_io = lax.broadcasted_iota(jnp.int32, (tr, tr), 1)
    lower = (c_io < r_io).astype(BF16)
    before = _dot(lower, member.astype(BF16)) + carry[...]
    col = lax.broadcasted_iota(jnp.int32, (tr, TOP_K), 1)
    e_out = jnp.zeros((tr, TOP_K), jnp.int32)
    r_out = jnp.zeros((tr, TOP_K), jnp.int32)
    g_out = jnp.zeros((tr, TOP_K), F32)
    for k in range(TOP_K):
        rk = jnp.sum(jnp.where(hots[k], before, 0.0), axis=1, keepdims=True).astype(jnp.int32)
        e_out = jnp.where(col == k, idxs[k], e_out)
        r_out = jnp.where(col == k, rk, r_out)
        g_out = jnp.where(col == k, exps[k] / denom, g_out)
    e_ref[...] = e_out
    rank_ref[...] = r_out
    gate_ref[...] = g_out
    tile_cnt = jnp.sum(member, axis=0, keepdims=True)
    base_ref[0] = carry[...].astype(jnp.int32)
    tcnt_ref[0] = tile_cnt.astype(jnp.int32)
    carry[...] = carry[...] + tile_cnt
    cnt_ref[...] = carry[...]


def _route(logits, tr):
    t_tok = logits.shape[0]
    n_tiles = t_tok // tr
    kern = functools.partial(_route_kernel, tr=tr)
    tile = lambda w: pl.BlockSpec((tr, w), lambda i: (i, 0))
    per_tile = pl.BlockSpec((1, 1, LANES), lambda i: (i, 0, 0))
    return pl.pallas_call(
        kern,
        out_shape=(jax.ShapeDtypeStruct((t_tok, TOP_K), jnp.int32),
                   jax.ShapeDtypeStruct((t_tok, TOP_K), jnp.int32),
                   jax.ShapeDtypeStruct((t_tok, TOP_K), F32),
                   jax.ShapeDtypeStruct((n_tiles, 1, LANES), jnp.int32),
                   jax.ShapeDtypeStruct((n_tiles, 1, LANES), jnp.int32),
                   jax.ShapeDtypeStruct((1, LANES), F32)),
        grid=(n_tiles,),
        in_specs=[tile(LANES)],
        out_specs=(tile(TOP_K), tile(TOP_K), tile(TOP_K), per_tile, per_tile,
                   pl.BlockSpec((1, LANES), lambda i: (0, 0))),
        scratch_shapes=[pltpu.VMEM((1, LANES), F32)],
        compiler_params=_cparams(("arbitrary",)),
        name="route",
    )(logits)


def _dispatch_kernel(pend_ref, npad_ref, dest_hbm, h_ref, xs_hbm, idx, zeros, sem_idx, sem_z, sem_row, *, td):
    i = pl.program_id(0)
    n_steps = pl.num_programs(0)
    slot = i % 2
    n_idx = td * TOP_K

    def idx_cp(step, sl):
        return pltpu.make_async_copy(dest_hbm.at[pl.ds(step * n_idx, n_idx)], idx.at[sl], sem_idx.at[sl])

    def zero_blk(row):
        return pltpu.make_async_copy(zeros, xs_hbm.at[pl.ds(pl.multiple_of(row, MOE_BLOCK), MOE_BLOCK)], sem_z)

    @pl.when(i == 0)
    def _():
        idx_cp(0, 0).start()
        zeros[...] = jnp.zeros_like(zeros)
        first_unused = pend_ref[N_EXPERTS - 1] // MOE_BLOCK
        n_blocks = xs_hbm.shape[0] // MOE_BLOCK

        def start(e, c):
            @pl.when(npad_ref[e] > 0)
            def _():
                zero_blk(pend_ref[e] - MOE_BLOCK).start()
            return c

        def wait(e, c):
            @pl.when(npad_ref[e] > 0)
            def _():
                zero_blk(pend_ref[e] - MOE_BLOCK).wait()
            return c

        def start_tail(b, c):
            zero_blk(b * MOE_BLOCK).start()
            return c

        def wait_tail(b, c):
            zero_blk(b * MOE_BLOCK).wait()
            return c

        lax.fori_loop(0, N_EXPERTS, start, 0)
        lax.fori_loop(first_unused, n_blocks, start_tail, 0)
        lax.fori_loop(0, N_EXPERTS, wait, 0)
        lax.fori_loop(first_unused, n_blocks, wait_tail, 0)

    @pl.when(i + 1 < n_steps)
    def _():
        idx_cp(i + 1, 1 - slot).start()

    idx_cp(i, slot).wait()

    def issue(t, c):
        for k in range(TOP_K):
            pltpu.make_async_copy(h_ref.at[pl.ds(t, 1)], xs_hbm.at[pl.ds(idx[slot, t * TOP_K + k], 1)],
                                  sem_row).start(priority=k % 2)
        return c

    lax.fori_loop(0, td, issue, 0, unroll=8)
    for _ in range(TOP_K):
        pltpu.make_async_copy(h_ref, xs_hbm.at[pl.ds(0, td)], sem_row).wait()


def _dispatch(pend, npad, dest, h2, n_slots, td):
    t_tok = h2.shape[0]
    kern = functools.partial(_dispatch_kernel, td=td)
    return pl.pallas_call(
        kern,
        out_shape=jax.ShapeDtypeStruct((n_slots, D_MODEL), F32),
        grid_spec=pltpu.PrefetchScalarGridSpec(
            num_scalar_prefetch=2,
            grid=(t_tok // td,),
            in_specs=[pl.BlockSpec(memory_space=pl.ANY),
                      pl.BlockSpec((td, D_MODEL), lambda i, *_: (i, 0))],
            out_specs=pl.BlockSpec(memory_space=pl.ANY),
            scratch_shapes=[pltpu.SMEM((2, td * TOP_K), jnp.int32),
                            pltpu.VMEM((MOE_BLOCK, D_MODEL), F32),
                            pltpu.SemaphoreType.DMA((2,)),
                            pltpu.SemaphoreType.DMA,
                            pltpu.SemaphoreType.DMA]),
        compiler_params=_cparams(("arbitrary",)),
        name="dispatch",
    )(pend, npad, dest, h2)


def _experts_kernel(be_ref, nu_ref, xs_ref, wg_ref, bg_ref, wu_ref, bu_ref, wd_ref, bd_ref, o_ref, act):
    i = pl.program_id(0)

    @pl.when(i < nu_ref[0])
    def _():
        x = xs_ref[...].astype(BF16)
        for f in range(D_FF // GROUP):
            fs = slice(f * GROUP, (f + 1) * GROUP)
            gt = jnp.minimum(_dot(x, wg_ref[0, :, fs]) + bg_ref[0, :, fs], SWIGLU_LIMIT)
            up = jnp.clip(_dot(x, wu_ref[0, :, fs]) + bu_ref[0, :, fs], -SWIGLU_LIMIT, SWIGLU_LIMIT)
            act[:, fs] = ((up + 1.0) * (gt * _sigmoid(SWIGLU_ALPHA * gt))).astype(BF16)
        o_ref[...] = (_dot(act[...], wd_ref[0]) + bd_ref[0]).astype(BF16)

    @pl.when(i >= nu_ref[0])
    def _():
        o_ref[...] = jnp.zeros_like(o_ref)


def _experts(block_e, n_used, xs, wl, layer):
    n_slots = xs.shape[0]
    n_blocks = n_slots // MOE_BLOCK

    def expert(i, be, nu):
        return layer * N_EXPERTS + be[jnp.minimum(i, nu[0] - 1)]

    wspec = lambda: pl.BlockSpec((1, D_MODEL, D_FF), lambda i, be, nu: (expert(i, be, nu), 0, 0))
    bspec = lambda: pl.BlockSpec((1, 1, D_FF), lambda i, be, nu: (expert(i, be, nu), 0, 0))
    return pl.pallas_call(
        _experts_kernel,
        out_shape=jax.ShapeDtypeStruct((n_slots, D_MODEL), BF16),
        grid_spec=pltpu.PrefetchScalarGridSpec(
            num_scalar_prefetch=2,
            grid=(n_blocks,),
            in_specs=[pl.BlockSpec((MOE_BLOCK, D_MODEL), lambda i, be, nu: (jnp.minimum(i, nu[0] - 1), 0)),
                      wspec(), bspec(), wspec(), bspec(), wspec(), bspec()],
            out_specs=pl.BlockSpec((MOE_BLOCK, D_MODEL), lambda i, be, nu: (i, 0)),
            scratch_shapes=[pltpu.VMEM((MOE_BLOCK, D_FF), BF16)]),
        compiler_params=_cparams(("arbitrary",)),
        name="experts",
    )(block_e, n_used, xs, wl["exp_gate_w"], wl["exp_gate_b"], wl["exp_up_w"], wl["exp_up_b"],
      wl["exp_down_w"], wl["exp_down_b"])


ROW_TILE = 16
SEG_ROWS = TOP_K * 256 + N_EXPERTS * 2 * ROW_TILE


def _combine_kernel(src_ref, dst_ref, nch_ref, e_ref, rank_ref, gate_ref, off_ref, x1_ref, mod_ref,
                    fg_ref, ys_hbm, o_ref, segbuf, gmat, sems, *, tc, final):
    j = pl.program_id(0)
    n_tiles = pl.num_programs(0)
    slot = j % 2

    def chunks(tile, sl, act):
        def per_expert(e, c):
            src0 = src_ref[tile * N_EXPERTS + e]
            dst0 = dst_ref[tile * N_EXPERTS + e]

            def per_chunk(ci, c2):
                src = pl.multiple_of(src0 + ci * ROW_TILE, ROW_TILE)
                dst = pl.multiple_of(dst0 + ci * ROW_TILE, ROW_TILE)
                act(pltpu.make_async_copy(ys_hbm.at[pl.ds(src, ROW_TILE)],
                                          segbuf.at[sl, pl.ds(dst, ROW_TILE)], sems.at[sl]))
                return c2

            lax.fori_loop(0, nch_ref[tile * N_EXPERTS + e], per_chunk, 0)
            return c

        lax.fori_loop(0, N_EXPERTS, per_expert, 0)

    @pl.when(j == 0)
    def _():
        segbuf[...] = jnp.zeros_like(segbuf)
        chunks(0, 0, lambda cp: cp.start())

    @pl.when(j + 1 < n_tiles)
    def _():
        chunks(j + 1, 1 - slot, lambda cp: cp.start())

    lane_tc = lax.broadcasted_iota(jnp.int32, (tc, LANES), 1)
    off_f = off_ref[0].astype(F32)
    cols = []
    for k in range(TOP_K):
        off_k = jnp.sum(jnp.where(lane_tc == e_ref[:, k:k + 1], off_f, 0.0), axis=1, keepdims=True)
        cols.append(rank_ref[:, k:k + 1] + off_k.astype(jnp.int32))
    colb = [jnp.broadcast_to(c, (tc, LANES)) for c in cols]
    gateb = [jnp.broadcast_to(gate_ref[:, k:k + 1], (tc, LANES)) for k in range(TOP_K)]
    for p in range(SEG_ROWS // LANES):
        col_io = lane_tc + p * LANES
        g = jnp.zeros((tc, LANES), F32)
        for k in reversed(range(TOP_K)):
            g = jnp.where(col_io == colb[k], gateb[k], g)
        gmat[:, p * LANES:(p + 1) * LANES] = g.astype(BF16)

    chunks(j, slot, lambda cp: cp.wait())
    moe = _dot(gmat[...], segbuf[slot])

    x2 = x1_ref[...] + mod_ref[0, 5:6, :] * moe
    if final:
        ms = jnp.mean(x2 * x2, axis=-1, keepdims=True)
        x2 = x2 * lax.rsqrt(ms + RMS_EPS) * fg_ref[...]
    o_ref[...] = x2


def _combine(chunk_src, chunk_dst, n_chunks, e_idx, rank, gates, col_off, x1, mod, final_g, ys, seq_len, tc, final):
    t_tok = x1.shape[0]
    assert SEG_ROWS >= TOP_K * tc + N_EXPERTS * 2 * (ROW_TILE - 1) and SEG_ROWS % GROUP == 0
    kern = functools.partial(_combine_kernel, tc=tc, final=final)
    tok4 = lambda: pl.BlockSpec((tc, TOP_K), lambda i, *_: (i, 0))
    return pl.pallas_call(
        kern,
        out_shape=jax.ShapeDtypeStruct((t_tok, D_MODEL), F32),
        grid_spec=pltpu.PrefetchScalarGridSpec(
            num_scalar_prefetch=3,
            grid=(t_tok // tc,),
            in_specs=[tok4(), tok4(), tok4(),
                      pl.BlockSpec((1, 1, LANES), lambda i, *_: (i, 0, 0)),
                      pl.BlockSpec((tc, D_MODEL), lambda i, *_: (i, 0)),
                      pl.BlockSpec((1, 8, D_MODEL), lambda i, *_: ((i * tc) // seq_len, 0, 0)),
                      pl.BlockSpec((1, D_MODEL), lambda i, *_: (0, 0)),
                      pl.BlockSpec(memory_space=pl.ANY)],
            out_specs=pl.BlockSpec((tc, D_MODEL), lambda i, *_: (i, 0)),
            scratch_shapes=[pltpu.VMEM((2, SEG_ROWS, D_MODEL), BF16),
                            pltpu.VMEM((tc, SEG_ROWS), BF16),
                            pltpu.SemaphoreType.DMA((2,))]),
        compiler_params=_cparams(("arbitrary",)),
        name="combine",
    )(chunk_src, chunk_dst, n_chunks, e_idx, rank, gates, col_off, x1, mod, final_g, ys)


def _dft_tables(n):
    j = jnp.arange(n, dtype=jnp.int32)
    ang = ((j[:, None] * j[None, :]) % n).astype(F32) * (2.0 * math.pi / n)
    scale = 1.0 / math.sqrt(n)
    return jnp.cos(ang) * scale, jnp.sin(ang) * scale


def _block_diag(blocks):
    n, r, c = blocks.shape
    rows = lax.broadcasted_iota(jnp.int32, (n * r, n * c), 0) // r
    cols = lax.broadcasted_iota(jnp.int32, (n * r, n * c), 1) // c
    return jnp.where(rows == cols, jnp.tile(blocks.reshape(n * r, c), (1, n)), 0)


def _prep_weights(w):
    depth = w["w_in"].shape[0]
    c64, s64 = _dft_tables(HEAD_DIM)
    n_heads = GROUP // HEAD_DIM
    chan_dft = jnp.concatenate([_block_diag(jnp.stack([c64] * n_heads)),
                                _block_diag(jnp.stack([s64] * n_heads))], axis=1).astype(BF16)
    avg = _block_diag(jnp.full((n_heads, HEAD_DIM, HEAD_DIM), 1.0 / HEAD_DIM, F32)).astype(BF16)
    stacked = dict(
        exp_gate_w=w["exp_gate_w"].astype(BF16).reshape(depth * N_EXPERTS, D_MODEL, D_FF),
        exp_up_w=w["exp_up_w"].astype(BF16).reshape(depth * N_EXPERTS, D_MODEL, D_FF),
        exp_down_w=w["exp_down_w"].astype(BF16).reshape(depth * N_EXPERTS, D_FF, D_MODEL),
        exp_gate_b=w["exp_gate_b"].reshape(depth * N_EXPERTS, 1, D_FF),
        exp_up_b=w["exp_up_b"].reshape(depth * N_EXPERTS, 1, D_FF),
        exp_down_b=w["exp_down_b"].reshape(depth * N_EXPERTS, 1, D_MODEL))
    layers = []
    for l in range(depth):
        layers.append(dict(
            stacked,
            norm1_g=w["norm1_g"][l].reshape(1, D_MODEL),
            norm2_g=w["norm2_g"][l].reshape(1, D_MODEL),
            w_in=w["w_in"][l].astype(BF16),
            fourier_w=w["fourier_w"][l].astype(BF16),
            conv31_w=jnp.pad(w["conv31_w"][l], ((0, 1), (0, 0))),
            conv31_b=w["conv31_b"][l].reshape(1, GROUP),
            gn_g=w["gn_g"][l].reshape(1, GROUP),
            gn_b=w["gn_b"][l].reshape(1, GROUP),
            avg=avg,
            pw_w=w["pw_w"][l].astype(BF16),
            pool_w=_block_diag(w["pool_w"][l]).astype(BF16),
            pool_scale=w["pool_scale"][l].reshape(1, GROUP),
            conv3_w=jnp.pad(w["conv3_w"][l], ((0, 5), (0, 0))),
            sconv_out_w=w["sconv_out_w"][l].astype(BF16),
            w_o=w["w_o"][l].astype(BF16),
            router_w=jnp.pad(w["router_w"][l], ((0, 0), (0, LANES - N_EXPERTS))).astype(BF16),
            router_b=jnp.pad(w["router_b"][l], (0, LANES - N_EXPERTS)).reshape(1, LANES),
        ))
    return chan_dft, layers


def _tiles(bsz, s):
    t_seq = min(512, s)
    t_dft_m = min(s, max(128, (8 * 1024 * 1024) // (bsz * GROUP * 4)))
    t_dft_k = min(1024, s)
    return dict(seq=t_seq, dft_m=t_dft_m, dft_k=t_dft_k, tok=min(256, bsz * s))


def _moe(h2, logits, x1, mod, final_g, wl, layer, seq_len, tiles, final):
    t_tok = h2.shape[0]
    tok = tiles["tok"]
    n_assign = t_tok * TOP_K
    n_blocks = -(-n_assign // MOE_BLOCK) + N_EXPERTS
    n_slots = n_blocks * MOE_BLOCK
    e_idx, rank, gates, base, tile_cnt, counts = _route(logits, tok)
    counts = counts[0, :N_EXPERTS].astype(jnp.int32)
    padded = (counts + MOE_BLOCK - 1) // MOE_BLOCK * MOE_BLOCK
    pend = jnp.cumsum(padded)
    pstart = pend - padded
    npad = padded - counts
    hot = e_idx[..., None] == jnp.arange(N_EXPERTS, dtype=jnp.int32)
    dest = (jnp.sum(jnp.where(hot, pstart, 0), axis=-1) + rank).reshape(n_assign)
    block_start = jnp.arange(n_blocks, dtype=jnp.int32) * MOE_BLOCK
    block_e = jnp.minimum(jnp.sum(block_start[:, None] >= pend[None, :], axis=1), N_EXPERTS - 1).astype(jnp.int32)
    n_used = (pend[-1:] // MOE_BLOCK).astype(jnp.int32)
    seg_lo = pstart[None, :] + base[:, 0, :N_EXPERTS]
    seg_n = tile_cnt[:, 0, :N_EXPERTS]
    win_lo = seg_lo - (seg_lo & (ROW_TILE - 1))
    win_rows = jnp.where(seg_n > 0, -(-(seg_lo + seg_n - win_lo) // ROW_TILE) * ROW_TILE, 0)
    win_dst = jnp.cumsum(win_rows, axis=1) - win_rows
    col_off = jnp.pad(win_dst - win_lo + pstart[None, :], ((0, 0), (0, LANES - N_EXPERTS)))
    col_off = col_off.reshape(-1, 1, LANES).astype(jnp.int32)
    flat = lambda a: a.reshape(-1).astype(jnp.int32)
    xs = _dispatch(pend.astype(jnp.int32), npad.astype(jnp.int32), dest, h2, n_slots, tok)
    ys = _experts(block_e, n_used, xs, wl, layer)
    return _combine(flat(win_lo), flat(win_dst), flat(win_rows // ROW_TILE), e_idx, rank, gates, col_off,
                    x1, mod, final_g, ys, seq_len, tok, final)


def _encoder(x, c, w, chan_dft, layers, tables, tiles=None):
    bsz, s, _ = x.shape
    tiles = tiles or _tiles(bsz, s)
    cos_t, nsin_t = tables
    depth = len(layers)
    final_g = w["final_g"].reshape(1, D_MODEL)
    ada_b = w["ada_b"].reshape(depth, 1, 6 * D_MODEL)
    for l, wl in enumerate(layers):
        mod = _ada(c, w["ada_w"], ada_b, l).reshape(bsz, 6, D_MODEL)
        mod = jnp.pad(mod, ((0, 0), (0, 2), (0, 0)))
        p, u = _inproj(x, mod, wl["norm1_g"], wl["w_in"], chan_dft, tiles["seq"])
        y = _seqdft(cos_t, nsin_t, p, tiles["dft_m"], tiles["dft_k"])
        x1, h2, logits = _mix(y, u, x, mod, wl, tiles["seq"])
        x = _moe(h2.reshape(bsz * s, D_MODEL), logits.reshape(bsz * s, LANES),
                 x1.reshape(bsz * s, D_MODEL), mod, final_g, wl, l, s, tiles,
                 final=(l == depth - 1)).reshape(bsz, s, D_MODEL)
    return x


def _seq_tables(s):
    c, sn = _dft_tables(s)
    return c.astype(BF16), (-sn).astype(BF16)


def kernel(x_prompt, x_sample, c_prompt, c_sample, norm1_g, norm2_g, ada_w, ada_b, w_in, fourier_w, conv31_w, conv31_b, gn_g, gn_b, pw_w, pool_w, pool_scale, conv3_w, sconv_out_w, w_o, router_w, router_b, exp_gate_w, exp_gate_b, exp_up_w, exp_up_b, exp_down_w, exp_down_b, final_g):
    w = dict(norm1_g=norm1_g, norm2_g=norm2_g, ada_w=ada_w, ada_b=ada_b, w_in=w_in, fourier_w=fourier_w,
             conv31_w=conv31_w, conv31_b=conv31_b, gn_g=gn_g, gn_b=gn_b, pw_w=pw_w, pool_w=pool_w,
             pool_scale=pool_scale, conv3_w=conv3_w, sconv_out_w=sconv_out_w, w_o=w_o, router_w=router_w,
             router_b=router_b, exp_gate_w=exp_gate_w, exp_gate_b=exp_gate_b, exp_up_w=exp_up_w,
             exp_up_b=exp_up_b, exp_down_w=exp_down_w, exp_down_b=exp_down_b, final_g=final_g)
    chan_dft, layers = _prep_weights(w)
    y_prompt = _encoder(x_prompt, c_prompt, w, chan_dft, layers, _seq_tables(x_prompt.shape[1]))
    y_sample = _encoder(x_sample, c_sample, w, chan_dft, layers, _seq_tables(x_sample.shape[1]))
    return (y_prompt, y_sample)
```

```python
import functools
import math

import jax
import jax.numpy as jnp
from jax import lax
from jax.experimental import pallas as pl
from jax.experimental.pallas import tpu as pltpu

D_MODEL = 1024
GROUP = 256
HEAD_DIM = 64
CONV31 = 31
POOL_WINDOWS = (2, 4, 8, 16)
N_EXPERTS = 32
TOP_K = 4
D_FF = 1024
SWIGLU_ALPHA = 1.702
SWIGLU_LIMIT = 7.0
MOE_BLOCK = 512
RMS_EPS = 1e-6
GN_EPS = 1e-5
HALO = 16
LANES = 128
VMEM_LIMIT = 48 * 1024 * 1024

F32 = jnp.float32
BF16 = jnp.bfloat16


def _cparams(sem):
    return pltpu.CompilerParams(dimension_semantics=sem, vmem_limit_bytes=VMEM_LIMIT)


def _dot(a, b):
    return jnp.dot(a, b, preferred_element_type=F32)


def _split_bf16(x):
    hi = x.astype(BF16)
    lo = (x - hi.astype(F32)).astype(BF16)
    return hi, lo


def _dot_hilo(x, w_bf16):
    hi, lo = _split_bf16(x)
    return _dot(hi, w_bf16) + _dot(lo, w_bf16)


def _sigmoid(x):
    return 1.0 / (1.0 + jnp.exp(-x))


def _ada_kernel(c_ref, w_ref, b_ref, o_ref):
    c = c_ref[...]
    cs = c * _sigmoid(c)
    c_hi, c_lo = _split_bf16(cs)
    w_hi, w_lo = _split_bf16(w_ref[0])
    o_ref[...] = _dot(c_hi, w_hi) + _dot(c_lo, w_hi) + _dot(c_hi, w_lo) + b_ref[0]


def _ada(c, ada_w, ada_b, layer):
    bsz = c.shape[0]
    n_out = ada_w.shape[2]
    return pl.pallas_call(
        _ada_kernel,
        out_shape=jax.ShapeDtypeStruct((bsz, n_out), F32),
        grid=(n_out // D_MODEL,),
        in_specs=[
            pl.BlockSpec((bsz, D_MODEL), lambda j: (0, 0)),
            pl.BlockSpec((1, D_MODEL, D_MODEL), lambda j: (layer, 0, j)),
            pl.BlockSpec((1, 1, D_MODEL), lambda j: (layer, 0, j)),
        ],
        out_specs=pl.BlockSpec((bsz, D_MODEL), lambda j: (0, j)),
        compiler_params=_cparams(("arbitrary",)),
        name="ada",
    )(c, ada_w, ada_b)


def _rms_mod(x, g, scale, shift):
    ms = jnp.mean(x * x, axis=-1, keepdims=True)
    y = x * lax.rsqrt(ms + RMS_EPS) * g
    return y * (1.0 + scale) + shift


def _inproj_kernel(x_ref, mod_ref, g_ref, w_ref, cs_ref, p_ref, u_ref):
    x = x_ref[0]
    h = _rms_mod(x, g_ref[...], mod_ref[0, 1:2, :], mod_ref[0, 0:1, :]).astype(BF16)
    a = _dot(h, w_ref[:, 0:GROUP]).astype(BF16)
    p_ref[0] = _dot(a, cs_ref[...]).astype(BF16)
    for j in range(6):
        u_ref[0, :, j * GROUP:(j + 1) * GROUP] = _dot(
            h, w_ref[:, (j + 1) * GROUP:(j + 2) * GROUP]).astype(BF16)


def _inproj(x, mod, g, w_in, chan_dft, tm):
    bsz, s, _ = x.shape
    return pl.pallas_call(
        _inproj_kernel,
        out_shape=(jax.ShapeDtypeStruct((bsz, s, 2 * GROUP), BF16),
                   jax.ShapeDtypeStruct((bsz, s, 6 * GROUP), BF16)),
        grid=(bsz, s // tm),
        in_specs=[
            pl.BlockSpec((1, tm, D_MODEL), lambda b, i: (b, i, 0)),
            pl.BlockSpec((1, 8, D_MODEL), lambda b, i: (b, 0, 0)),
            pl.BlockSpec((1, D_MODEL), lambda b, i: (0, 0)),
            pl.BlockSpec((D_MODEL, 7 * GROUP), lambda b, i: (0, 0)),
            pl.BlockSpec((GROUP, 2 * GROUP), lambda b, i: (0, 0)),
        ],
        out_specs=(pl.BlockSpec((1, tm, 2 * GROUP), lambda b, i: (b, i, 0)),
                   pl.BlockSpec((1, tm, 6 * GROUP), lambda b, i: (b, i, 0))),
        compiler_params=_cparams(("arbitrary", "arbitrary")),
        name="inproj",
    )(x, mod, g, w_in, chan_dft)


def _seqdft_kernel(c_ref, s_ref, p_ref, o_ref, acc_ref):
    k = pl.program_id(1)
    b = pl.program_id(2)
    contrib = _dot(c_ref[...], p_ref[0, :, 0:GROUP]) + _dot(s_ref[...], p_ref[0, :, GROUP:2 * GROUP])

    @pl.when(k == 0)
    def _():
        acc_ref[b] = contrib

    @pl.when(k > 0)
    def _():
        acc_ref[b] = acc_ref[b] + contrib

    @pl.when(k == pl.num_programs(1) - 1)
    def _():
        o_ref[b] = acc_ref[b].astype(BF16)


def _seqdft(cos_t, nsin_t, p, tm, tk):
    bsz, s, _ = p.shape
    return pl.pallas_call(
        _seqdft_kernel,
        out_shape=jax.ShapeDtypeStruct((bsz, s, GROUP), BF16),
        grid=(s // tm, s // tk, bsz),
        in_specs=[
            pl.BlockSpec((tm, tk), lambda i, k, b: (i, k)),
            pl.BlockSpec((tm, tk), lambda i, k, b: (i, k)),
            pl.BlockSpec((1, tk, 2 * GROUP), lambda i, k, b: (b, k, 0)),
        ],
        out_specs=pl.BlockSpec((bsz, tm, GROUP), lambda i, k, b: (0, i, 0)),
        scratch_shapes=[pltpu.VMEM((bsz, tm, GROUP), F32)],
        compiler_params=_cparams(("arbitrary", "arbitrary", "arbitrary")),
        name="seqdft",
    )(cos_t, nsin_t, p)


U_V, U_G, U_P, U_BG, U_CG, U_XV = (j * GROUP for j in range(6))


def _mix_kernel(y_ref, u_ref, up_ref, un_ref, x_ref, mod_ref,
                fw_ref, c31w_ref, c31b_ref, gng_ref, gnb_ref, avg_ref, pww_ref,
                poolw_ref, pscale_ref, c3w_ref, sow_ref, wo_ref, n2g_ref, rw_ref, rb_ref,
                x1_ref, h2_ref, lg_ref,
                zext, pext, s2, s4, s8, s16, qext, cat, *, tq, seq_len):
    i = pl.program_id(1)
    has_prev = (i > 0).astype(F32)
    has_next = (i < pl.num_programs(1) - 1).astype(F32)
    rows = tq + 2 * HALO

    def cols(ref, c0):
        return ref[0, :, c0:c0 + GROUP].astype(F32)

    def fill_ext(dst, fn):
        dst[0:HALO, :] = fn(up_ref) * has_prev
        dst[HALO:HALO + tq, :] = fn(u_ref)
        dst[HALO + tq:rows, :] = fn(un_ref) * has_next

    cat[:, 0:GROUP] = _dot(y_ref[0], fw_ref[...]).astype(BF16)

    fill_ext(zext, lambda r: cols(r, U_V) * _sigmoid(cols(r, U_G)))
    conv = jnp.broadcast_to(c31b_ref[...], (tq, GROUP))
    for j in range(CONV31):
        conv = conv + c31w_ref[j:j + 1, :] * zext[pl.ds(HALO - CONV31 // 2 + j, tq), :]
    mu = _dot_hilo(conv, avg_ref[...])
    dev = conv - mu
    var = _dot_hilo(dev * dev, avg_ref[...])
    zn = dev * lax.rsqrt(var + GN_EPS) * gng_ref[...] + gnb_ref[...]
    cat[:, GROUP:2 * GROUP] = _dot((zn * _sigmoid(zn)).astype(BF16), pww_ref[...]).astype(BF16)

    fill_ext(pext, lambda r: cols(r, U_P))
    n = rows - 8
    zeros8 = jnp.zeros((8, GROUP), F32)
    s2[0:n, :] = pext[0:n, :] + pext[pl.ds(1, n), :]
    s2[n:rows, :] = zeros8
    s4[0:n, :] = s2[0:n, :] + s2[pl.ds(2, n), :]
    s4[n:rows, :] = zeros8
    s8[0:n, :] = s4[0:n, :] + s4[pl.ds(4, n), :]
    s8[n:rows, :] = zeros8
    s16[0:n, :] = s8[0:n, :] + s8[pl.ds(8, n), :]
    lane = lax.broadcasted_iota(jnp.int32, (tq, GROUP), 1)
    pos = lax.broadcasted_iota(jnp.int32, (tq, GROUP), 0) + i * tq
    win = jnp.where(lane < HEAD_DIM, s2[pl.ds(HALO - 1, tq), :],
                    jnp.where(lane < 2 * HEAD_DIM, s4[pl.ds(HALO - 2, tq), :],
                              jnp.where(lane < 3 * HEAD_DIM, s8[pl.ds(HALO - 4, tq), :],
                                        s16[pl.ds(HALO - 8, tq), :])))
    half = jnp.where(lane < HEAD_DIM, 1,
                     jnp.where(lane < 2 * HEAD_DIM, 2, jnp.where(lane < 3 * HEAD_DIM, 4, 8)))
    cnt = jnp.minimum(pos + half, seq_len) - jnp.maximum(pos - half, 0)
    dpool = win / cnt.astype(F32) - pext[HALO:HALO + tq, :]
    cat[:, 2 * GROUP:3 * GROUP] = (_dot(dpool.astype(BF16), poolw_ref[...]) * pscale_ref[...]).astype(BF16)

    fill_ext(qext, lambda r: cols(r, U_CG) * cols(r, U_XV))
    c3 = (c3w_ref[0:1, :] * qext[pl.ds(HALO - 1, tq), :]
          + c3w_ref[1:2, :] * qext[HALO:HALO + tq, :]
          + c3w_ref[2:3, :] * qext[pl.ds(HALO + 1, tq), :])
    cat[:, 3 * GROUP:4 * GROUP] = _dot((cols(u_ref, U_BG) * c3).astype(BF16), sow_ref[...]).astype(BF16)

    mixed = _dot(cat[...], wo_ref[...])
    x1 = x_ref[0] + mod_ref[0, 2:3, :] * mixed
    x1_ref[0] = x1
    h2 = _rms_mod(x1, n2g_ref[...], mod_ref[0, 4:5, :], mod_ref[0, 3:4, :])
    h2_ref[0] = h2
    lg_ref[0] = _dot(h2.astype(BF16), rw_ref[...]) + rb_ref[...]


def _mix(y, u, x, mod, wl, tq):
    bsz, s, _ = x.shape
    rows = tq + 2 * HALO
    hb = tq // HALO
    n_hblk = s // HALO
    full = lambda shape: pl.BlockSpec(shape, lambda b, i: tuple(0 for _ in shape))
    kern = functools.partial(_mix_kernel, tq=tq, seq_len=s)
    return pl.pallas_call(
        kern,
        out_shape=(jax.ShapeDtypeStruct((bsz, s, D_MODEL), F32),
                   jax.ShapeDtypeStruct((bsz, s, D_MODEL), F32),
                   jax.ShapeDtypeStruct((bsz, s, LANES), F32)),
        grid=(bsz, s // tq),
        in_specs=[
            pl.BlockSpec((1, tq, GROUP), lambda b, i: (b, i, 0)),
            pl.BlockSpec((1, tq, 6 * GROUP), lambda b, i: (b, i, 0)),
            pl.BlockSpec((1, HALO, 6 * GROUP), lambda b, i: (b, jnp.maximum(i * hb - 1, 0), 0)),
            pl.BlockSpec((1, HALO, 6 * GROUP), lambda b, i: (b, jnp.minimum((i + 1) * hb, n_hblk - 1), 0)),
            pl.BlockSpec((1, tq, D_MODEL), lambda b, i: (b, i, 0)),
            pl.BlockSpec((1, 8, D_MODEL), lambda b, i: (b, 0, 0)),
            full((GROUP, GROUP)),
            full((32, GROUP)),
            full((1, GROUP)),
            full((1, GROUP)),
            full((1, GROUP)),
            full((GROUP, GROUP)),
            full((GROUP, GROUP)),
            full((GROUP, GROUP)),
            full((1, GROUP)),
            full((8, GROUP)),
            full((GROUP, GROUP)),
            full((D_MODEL, D_MODEL)),
            full((1, D_MODEL)),
            full((D_MODEL, LANES)),
            full((1, LANES)),
        ],
        out_specs=(pl.BlockSpec((1, tq, D_MODEL), lambda b, i: (b, i, 0)),
                   pl.BlockSpec((1, tq, D_MODEL), lambda b, i: (b, i, 0)),
                   pl.BlockSpec((1, tq, LANES), lambda b, i: (b, i, 0))),
        scratch_shapes=[pltpu.VMEM((rows, GROUP), F32) for _ in range(7)]
        + [pltpu.VMEM((tq, D_MODEL), BF16)],
        compiler_params=_cparams(("arbitrary", "arbitrary")),
        name="mix",
    )(y, u, u, u, x, mod, wl["fourier_w"], wl["conv31_w"], wl["conv31_b"], wl["gn_g"], wl["gn_b"],
      wl["avg"], wl["pw_w"], wl["pool_w"], wl["pool_scale"], wl["conv3_w"], wl["sconv_out_w"],
      wl["w_o"], wl["norm2_g"], wl["router_w"], wl["router_b"])


def _route_kernel(lg_ref, e_ref, rank_ref, gate_ref, base_ref, tcnt_ref, cnt_ref, carry, *, tr):
    i = pl.program_id(0)

    @pl.when(i == 0)
    def _():
        carry[...] = jnp.zeros_like(carry)

    lane = lax.broadcasted_iota(jnp.int32, (tr, LANES), 1).astype(F32)
    neg = jnp.float32(-jnp.inf)
    work = jnp.where(lane < N_EXPERTS, lg_ref[...], neg)
    vals, idxs, hots = [], [], []
    for _ in range(TOP_K):
        m = jnp.max(work, axis=1, keepdims=True)
        idx = jnp.min(jnp.where(work == m, lane, float(LANES)), axis=1, keepdims=True)
        hot = lane == idx
        vals.append(m)
        idxs.append(idx.astype(jnp.int32))
        hots.append(hot)
        work = jnp.where(hot, neg, work)
    exps = [jnp.exp(v - vals[0]) for v in vals]
    denom = exps[0] + exps[1] + exps[2] + exps[3]
    member = sum(h.astype(F32) for h in hots)
    r_io = lax.broadcasted_iota(jnp.int32, (tr, tr), 0)
    c_io = lax.broadcasted_iota(jnp.int32, (tr, tr), 1)
    lower = (c_io < r_io).astype(BF16)
    before = _dot(lower, member.astype(BF16)) + carry[...]
    col = lax.broadcasted_iota(jnp.int32, (tr, TOP_K), 1)
    e_out = jnp.zeros((tr, TOP_K), jnp.int32)
    r_out = jnp.zeros((tr, TOP_K), jnp.int32)
    g_out = jnp.zeros((tr, TOP_K), F32)
    for k in range(TOP_K):
        rk = jnp.sum(jnp.where(hots[k], before, 0.0), axis=1, keepdims=True).astype(jnp.int32)
        e_out = jnp.where(col == k, idxs[k], e_out)
        r_out = jnp.where(col == k, rk, r_out)
        g_out = jnp.where(col == k, exps[k] / denom, g_out)
    e_ref[...] = e_out
    rank_ref[...] = r_out
    gate_ref[...] = g_out
    tile_cnt = jnp.sum(member, axis=0, keepdims=True)
    base_ref[0] = carry[...].astype(jnp.int32)
    tcnt_ref[0] = tile_cnt.astype(jnp.int32)
    carry[...] = carry[...] + tile_cnt
    cnt_ref[...] = carry[...]


def _route(logits, tr):
    t_tok = logits.shape[0]
    n_tiles = t_tok // tr
    kern = functools.partial(_route_kernel, tr=tr)
    tile = lambda w: pl.BlockSpec((tr, w), lambda i: (i, 0))
    per_tile = pl.BlockSpec((1, 1, LANES), lambda i: (i, 0, 0))
    return pl.pallas_call(
        kern,
        out_shape=(jax.ShapeDtypeStruct((t_tok, TOP_K), jnp.int32),
                   jax.ShapeDtypeStruct((t_tok, TOP_K), jnp.int32),
                   jax.ShapeDtypeStruct((t_tok, TOP_K), F32),
                   jax.ShapeDtypeStruct((n_tiles, 1, LANES), jnp.int32),
                   jax.ShapeDtypeStruct((n_tiles, 1, LANES), jnp.int32),
                   jax.ShapeDtypeStruct((1, LANES), F32)),
        grid=(n_tiles,),
        in_specs=[tile(LANES)],
        out_specs=(tile(TOP_K), tile(TOP_K), tile(TOP_K), per_tile, per_tile,
                   pl.BlockSpec((1, LANES), lambda i: (0, 0))),
        scratch_shapes=[pltpu.VMEM((1, LANES), F32)],
        compiler_params=_cparams(("arbitrary",)),
        name="route",
    )(logits)


F32_TILE = 8
DISP_ROWS = TOP_K * 256 + N_EXPERTS * 2 * F32_TILE
CARRY_ROWS = N_EXPERTS * F32_TILE


def _dispatch_kernel(src_ref, dst_ref, nch_ref, head_ref, pend_ref, npad_ref, e_ref, rank_ref, off_ref, h_ref,
                     xs_hbm, outbuf, pmat, carry, zeros, sems, sem_z, *, td):
    i = pl.program_id(0)
    n_steps = pl.num_programs(0)
    slot = i % 2

    def zero_blk(row):
        return pltpu.make_async_copy(zeros, xs_hbm.at[pl.ds(pl.multiple_of(row, MOE_BLOCK), MOE_BLOCK)], sem_z)

    @pl.when(i == 0)
    def _():
        zeros[...] = jnp.zeros_like(zeros)
        carry[...] = jnp.zeros_like(carry)
        first_unused = pend_ref[N_EXPERTS - 1] // MOE_BLOCK
        n_blocks = xs_hbm.shape[0] // MOE_BLOCK

        def start(e, c):
            @pl.when(npad_ref[e] > 0)
            def _():
                zero_blk(pend_ref[e] - MOE_BLOCK).start()
            return c

        def wait(e, c):
            @pl.when(npad_ref[e] > 0)
            def _():
                zero_blk(pend_ref[e] - MOE_BLOCK).wait()
            return c

        def start_tail(b, c):
            zero_blk(b * MOE_BLOCK).start()
            return c

        def wait_tail(b, c):
            zero_blk(b * MOE_BLOCK).wait()
            return c

        lax.fori_loop(0, N_EXPERTS, start, 0)
        lax.fori_loop(first_unused, n_blocks, start_tail, 0)
        lax.fori_loop(0, N_EXPERTS, wait, 0)
        lax.fori_loop(first_unused, n_blocks, wait_tail, 0)

    def chunks(tile, sl, act):
        def per_expert(e, c):
            src0 = src_ref[tile * N_EXPERTS + e]
            dst0 = dst_ref[tile * N_EXPERTS + e]

            def per_chunk(ci, c2):
                src = pl.multiple_of(src0 + ci * F32_TILE, F32_TILE)
                dst = pl.multiple_of(dst0 + ci * F32_TILE, F32_TILE)
                act(pltpu.make_async_copy(outbuf.at[sl, pl.ds(src, F32_TILE)],
                                          xs_hbm.at[pl.ds(dst, F32_TILE)], sems.at[sl]))
                return c2

            lax.fori_loop(0, nch_ref[tile * N_EXPERTS + e], per_chunk, 0)
            return c

        lax.fori_loop(0, N_EXPERTS, per_expert, 0)

    lane_td = lax.broadcasted_iota(jnp.int32, (td, LANES), 1)
    off_f = off_ref[0].astype(F32)
    pos_cols = jnp.full((td, LANES), -1.0, F32)
    for k in range(TOP_K):
        off_k = jnp.sum(jnp.where(lane_td == e_ref[:, k:k + 1], off_f, 0.0), axis=1, keepdims=True)
        pos_cols = jnp.where(lane_td == k, rank_ref[:, k:k + 1].astype(F32) + off_k, pos_cols)
    pos_t = jnp.transpose(pos_cols).astype(jnp.int32)
    pos = [jnp.broadcast_to(pos_t[k:k + 1, :], (GROUP, td)) for k in range(TOP_K)]
    row_io = lax.broadcasted_iota(jnp.int32, (GROUP, td), 0)
    for rc in range(DISP_ROWS // GROUP):
        r = row_io + rc * GROUP
        tok = jnp.zeros((GROUP, td), F32)
        for k in range(TOP_K):
            tok = jnp.where(r == pos[k], 1.0, tok)
        pmat[rc * GROUP:(rc + 1) * GROUP, :] = tok.astype(BF16)

    outbuf[slot] = _dot(pmat[...], h_ref[...].astype(BF16))

    sub_io = lax.broadcasted_iota(jnp.int32, (F32_TILE, D_MODEL), 0)
    for e in range(N_EXPERTS):
        es = slice(e * F32_TILE, (e + 1) * F32_TILE)
        n_ch = nch_ref[i * N_EXPERTS + e]
        first = pl.multiple_of(src_ref[i * N_EXPERTS + e], F32_TILE)
        outbuf[slot, pl.ds(first, F32_TILE), :] = jnp.where(
            sub_io < head_ref[i * N_EXPERTS + e], carry[es, :], outbuf[slot, pl.ds(first, F32_TILE), :])
        last = pl.multiple_of(first + jnp.maximum(n_ch - 1, 0) * F32_TILE, F32_TILE)
        carry[es, :] = jnp.where(n_ch > 0, outbuf[slot, pl.ds(last, F32_TILE), :], carry[es, :])

    @pl.when(i > 0)
    def _():
        chunks(i - 1, 1 - slot, lambda cp: cp.wait())

    chunks(i, slot, lambda cp: cp.start())

    @pl.when(i == n_steps - 1)
    def _():
        chunks(i, slot, lambda cp: cp.wait())


def _dispatch(chunk_src, chunk_dst, n_chunks, head, pend, npad, e_idx, rank, col_off, h2, n_slots, td):
    t_tok = h2.shape[0]
    assert DISP_ROWS >= TOP_K * td + N_EXPERTS * 2 * (F32_TILE - 1) + F32_TILE and DISP_ROWS % GROUP == 0
    kern = functools.partial(_dispatch_kernel, td=td)
    tok4 = lambda: pl.BlockSpec((td, TOP_K), lambda i, *_: (i, 0))
    return pl.pallas_call(
        kern,
        out_shape=jax.ShapeDtypeStruct((n_slots, D_MODEL), F32),
        grid_spec=pltpu.PrefetchScalarGridSpec(
            num_scalar_prefetch=6,
            grid=(t_tok // td,),
            in_specs=[tok4(), tok4(),
                      pl.BlockSpec((1, 1, LANES), lambda i, *_: (i, 0, 0)),
                      pl.BlockSpec((td, D_MODEL), lambda i, *_: (i, 0))],
            out_specs=pl.BlockSpec(memory_space=pl.ANY),
            scratch_shapes=[pltpu.VMEM((2, DISP_ROWS, D_MODEL), F32),
                            pltpu.VMEM((DISP_ROWS, td), BF16),
                            pltpu.VMEM((CARRY_ROWS, D_MODEL), F32),
                            pltpu.VMEM((MOE_BLOCK, D_MODEL), F32),
                            pltpu.SemaphoreType.DMA((2,)),
                            pltpu.SemaphoreType.DMA]),
        compiler_params=_cparams(("arbitrary",)),
        name="dispatch",
    )(chunk_src, chunk_dst, n_chunks, head, pend, npad, e_idx, rank, col_off, h2)


def _experts_kernel(be_ref, nu_ref, xs_ref, wg_ref, bg_ref, wu_ref, bu_ref, wd_ref, bd_ref, o_ref, act):
    i = pl.program_id(0)

    @pl.when(i < nu_ref[0])
    def _():
        x = xs_ref[...].astype(BF16)
        for f in range(D_FF // GROUP):
            fs = slice(f * GROUP, (f + 1) * GROUP)
            gt = jnp.minimum(_dot(x, wg_ref[0, :, fs]) + bg_ref[0, :, fs], SWIGLU_LIMIT)
            up = jnp.clip(_dot(x, wu_ref[0, :, fs]) + bu_ref[0, :, fs], -SWIGLU_LIMIT, SWIGLU_LIMIT)
            act[:, fs] = ((up + 1.0) * (gt * _sigmoid(SWIGLU_ALPHA * gt))).astype(BF16)
        o_ref[...] = (_dot(act[...], wd_ref[0]) + bd_ref[0]).astype(BF16)

    @pl.when(i >= nu_ref[0])
    def _():
        o_ref[...] = jnp.zeros_like(o_ref)


def _experts(block_e, n_used, xs, wl, layer):
    n_slots = xs.shape[0]
    n_blocks = n_slots // MOE_BLOCK

    def expert(i, be, nu):
        return layer * N_EXPERTS + be[jnp.minimum(i, nu[0] - 1)]

    wspec = lambda: pl.BlockSpec((1, D_MODEL, D_FF), lambda i, be, nu: (expert(i, be, nu), 0, 0))
    bspec = lambda: pl.BlockSpec((1, 1, D_FF), lambda i, be, nu: (expert(i, be, nu), 0, 0))
    return pl.pallas_call(
        _experts_kernel,
        out_shape=jax.ShapeDtypeStruct((n_slots, D_MODEL), BF16),
        grid_spec=pltpu.PrefetchScalarGridSpec(
            num_scalar_prefetch=2,
            grid=(n_blocks,),
            in_specs=[pl.BlockSpec((MOE_BLOCK, D_MODEL), lambda i, be, nu: (jnp.minimum(i, nu[0] - 1), 0)),
                      wspec(), bspec(), wspec(), bspec(), wspec(), bspec()],
            out_specs=pl.BlockSpec((MOE_BLOCK, D_MODEL), lambda i, be, nu: (i, 0)),
            scratch_shapes=[pltpu.VMEM((MOE_BLOCK, D_FF), BF16)]),
        compiler_params=_cparams(("arbitrary",)),
        name="experts",
    )(block_e, n_used, xs, wl["exp_gate_w"], wl["exp_gate_b"], wl["exp_up_w"], wl["exp_up_b"],
      wl["exp_down_w"], wl["exp_down_b"])


ROW_TILE = 16
SEG_ROWS = TOP_K * 256 + N_EXPERTS * 2 * ROW_TILE


def _combine_kernel(src_ref, dst_ref, nch_ref, e_ref, rank_ref, gate_ref, off_ref, x1_ref, mod_ref,
                    fg_ref, ys_hbm, o_ref, segbuf, gmat, sems, *, tc, final):
    j = pl.program_id(0)
    n_tiles = pl.num_programs(0)
    slot = j % 2

    def chunks(tile, sl, act):
        def per_expert(e, c):
            src0 = src_ref[tile * N_EXPERTS + e]
            dst0 = dst_ref[tile * N_EXPERTS + e]

            def per_chunk(ci, c2):
                src = pl.multiple_of(src0 + ci * ROW_TILE, ROW_TILE)
                dst = pl.multiple_of(dst0 + ci * ROW_TILE, ROW_TILE)
                act(pltpu.make_async_copy(ys_hbm.at[pl.ds(src, ROW_TILE)],
                                          segbuf.at[sl, pl.ds(dst, ROW_TILE)], sems.at[sl]))
                return c2

            lax.fori_loop(0, nch_ref[tile * N_EXPERTS + e], per_chunk, 0)
            return c

        lax.fori_loop(0, N_EXPERTS, per_expert, 0)

    @pl.when(j == 0)
    def _():
        segbuf[...] = jnp.zeros_like(segbuf)
        chunks(0, 0, lambda cp: cp.start())

    @pl.when(j + 1 < n_tiles)
    def _():
        chunks(j + 1, 1 - slot, lambda cp: cp.start())

    lane_tc = lax.broadcasted_iota(jnp.int32, (tc, LANES), 1)
    off_f = off_ref[0].astype(F32)
    cols = []
    for k in range(TOP_K):
        off_k = jnp.sum(jnp.where(lane_tc == e_ref[:, k:k + 1], off_f, 0.0), axis=1, keepdims=True)
        cols.append(rank_ref[:, k:k + 1] + off_k.astype(jnp.int32))
    colb = [jnp.broadcast_to(c, (tc, LANES)) for c in cols]
    gateb = [jnp.broadcast_to(gate_ref[:, k:k + 1], (tc, LANES)) for k in range(TOP_K)]
    for p in range(SEG_ROWS // LANES):
        col_io = lane_tc + p * LANES
        g = jnp.zeros((tc, LANES), F32)
        for k in reversed(range(TOP_K)):
            g = jnp.where(col_io == colb[k], gateb[k], g)
        gmat[:, p * LANES:(p + 1) * LANES] = g.astype(BF16)

    chunks(j, slot, lambda cp: cp.wait())
    moe = _dot(gmat[...], segbuf[slot])

    x2 = x1_ref[...] + mod_ref[0, 5:6, :] * moe
    if final:
        ms = jnp.mean(x2 * x2, axis=-1, keepdims=True)
        x2 = x2 * lax.rsqrt(ms + RMS_EPS) * fg_ref[...]
    o_ref[...] = x2


def _combine(chunk_src, chunk_dst, n_chunks, e_idx, rank, gates, col_off, x1, mod, final_g, ys, seq_len, tc, final):
    t_tok = x1.shape[0]
    assert SEG_ROWS >= TOP_K * tc + N_EXPERTS * 2 * (ROW_TILE - 1) and SEG_ROWS % GROUP == 0
    kern = functools.partial(_combine_kernel, tc=tc, final=final)
    tok4 = lambda: pl.BlockSpec((tc, TOP_K), lambda i, *_: (i, 0))
    return pl.pallas_call(
        kern,
        out_shape=jax.ShapeDtypeStruct((t_tok, D_MODEL), F32),
        grid_spec=pltpu.PrefetchScalarGridSpec(
            num_scalar_prefetch=3,
            grid=(t_tok // tc,),
            in_specs=[tok4(), tok4(), tok4(),
                      pl.BlockSpec((1, 1, LANES), lambda i, *_: (i, 0, 0)),
                      pl.BlockSpec((tc, D_MODEL), lambda i, *_: (i, 0)),
                      pl.BlockSpec((1, 8, D_MODEL), lambda i, *_: ((i * tc) // seq_len, 0, 0)),
                      pl.BlockSpec((1, D_MODEL), lambda i, *_: (0, 0)),
                      pl.BlockSpec(memory_space=pl.ANY)],
            out_specs=pl.BlockSpec((tc, D_MODEL), lambda i, *_: (i, 0)),
            scratch_shapes=[pltpu.VMEM((2, SEG_ROWS, D_MODEL), BF16),
                            pltpu.VMEM((tc, SEG_ROWS), BF16),
                            pltpu.SemaphoreType.DMA((2,))]),
        compiler_params=_cparams(("arbitrary",)),
        name="combine",
    )(chunk_src, chunk_dst, n_chunks, e_idx, rank, gates, col_off, x1, mod, final_g, ys)


def _dft_tables(n):
    j = jnp.arange(n, dtype=jnp.int32)
    ang = ((j[:, None] * j[None, :]) % n).astype(F32) * (2.0 * math.pi / n)
    scale = 1.0 / math.sqrt(n)
    return jnp.cos(ang) * scale, jnp.sin(ang) * scale


def _block_diag(blocks):
    n, r, c = blocks.shape
    rows = lax.broadcasted_iota(jnp.int32, (n * r, n * c), 0) // r
    cols = lax.broadcasted_iota(jnp.int32, (n * r, n * c), 1) // c
    return jnp.where(rows == cols, jnp.tile(blocks.reshape(n * r, c), (1, n)), 0)


def _prep_weights(w):
    depth = w["w_in"].shape[0]
    c64, s64 = _dft_tables(HEAD_DIM)
    n_heads = GROUP // HEAD_DIM
    chan_dft = jnp.concatenate([_block_diag(jnp.stack([c64] * n_heads)),
                                _block_diag(jnp.stack([s64] * n_heads))], axis=1).astype(BF16)
    avg = _block_diag(jnp.full((n_heads, HEAD_DIM, HEAD_DIM), 1.0 / HEAD_DIM, F32)).astype(BF16)
    stacked = dict(
        exp_gate_w=w["exp_gate_w"].astype(BF16).reshape(depth * N_EXPERTS, D_MODEL, D_FF),
        exp_up_w=w["exp_up_w"].astype(BF16).reshape(depth * N_EXPERTS, D_MODEL, D_FF),
        exp_down_w=w["exp_down_w"].astype(BF16).reshape(depth * N_EXPERTS, D_FF, D_MODEL),
        exp_gate_b=w["exp_gate_b"].reshape(depth * N_EXPERTS, 1, D_FF),
        exp_up_b=w["exp_up_b"].reshape(depth * N_EXPERTS, 1, D_FF),
        exp_down_b=w["exp_down_b"].reshape(depth * N_EXPERTS, 1, D_MODEL))
    layers = []
    for l in range(depth):
        layers.append(dict(
            stacked,
            norm1_g=w["norm1_g"][l].reshape(1, D_MODEL),
            norm2_g=w["norm2_g"][l].reshape(1, D_MODEL),
            w_in=w["w_in"][l].astype(BF16),
            fourier_w=w["fourier_w"][l].astype(BF16),
            conv31_w=jnp.pad(w["conv31_w"][l], ((0, 1), (0, 0))),
            conv31_b=w["conv31_b"][l].reshape(1, GROUP),
            gn_g=w["gn_g"][l].reshape(1, GROUP),
            gn_b=w["gn_b"][l].reshape(1, GROUP),
            avg=avg,
            pw_w=w["pw_w"][l].astype(BF16),
            pool_w=_block_diag(w["pool_w"][l]).astype(BF16),
            pool_scale=w["pool_scale"][l].reshape(1, GROUP),
            conv3_w=jnp.pad(w["conv3_w"][l], ((0, 5), (0, 0))),
            sconv_out_w=w["sconv_out_w"][l].astype(BF16),
            w_o=w["w_o"][l].astype(BF16),
            router_w=jnp.pad(w["router_w"][l], ((0, 0), (0, LANES - N_EXPERTS))).astype(BF16),
            router_b=jnp.pad(w["router_b"][l], (0, LANES - N_EXPERTS)).reshape(1, LANES),
        ))
    return chan_dft, layers


def _tiles(bsz, s):
    t_seq = min(512, s)
    t_dft_m = min(s, max(128, (8 * 1024 * 1024) // (bsz * GROUP * 4)))
    t_dft_k = min(1024, s)
    return dict(seq=t_seq, dft_m=t_dft_m, dft_k=t_dft_k, tok=min(256, bsz * s))


def _moe(h2, logits, x1, mod, final_g, wl, layer, seq_len, tiles, final):
    t_tok = h2.shape[0]
    tok = tiles["tok"]
    n_assign = t_tok * TOP_K
    n_blocks = -(-n_assign // MOE_BLOCK) + N_EXPERTS
    n_slots = n_blocks * MOE_BLOCK
    e_idx, rank, gates, base, tile_cnt, counts = _route(logits, tok)
    counts = counts[0, :N_EXPERTS].astype(jnp.int32)
    padded = (counts + MOE_BLOCK - 1) // MOE_BLOCK * MOE_BLOCK
    pend = jnp.cumsum(padded)
    pstart = pend - padded
    npad = padded - counts
    block_start = jnp.arange(n_blocks, dtype=jnp.int32) * MOE_BLOCK
    block_e = jnp.minimum(jnp.sum(block_start[:, None] >= pend[None, :], axis=1), N_EXPERTS - 1).astype(jnp.int32)
    n_used = (pend[-1:] // MOE_BLOCK).astype(jnp.int32)
    seg_lo = pstart[None, :] + base[:, 0, :N_EXPERTS]
    seg_n = tile_cnt[:, 0, :N_EXPERTS]
    win_lo = seg_lo - (seg_lo & (ROW_TILE - 1))
    win_rows = jnp.where(seg_n > 0, -(-(seg_lo + seg_n - win_lo) // ROW_TILE) * ROW_TILE, 0)
    win_dst = jnp.cumsum(win_rows, axis=1) - win_rows
    col_off = jnp.pad(win_dst - win_lo + pstart[None, :], ((0, 0), (0, LANES - N_EXPERTS)))
    col_off = col_off.reshape(-1, 1, LANES).astype(jnp.int32)
    flat = lambda a: a.reshape(-1).astype(jnp.int32)
    head = seg_lo & (F32_TILE - 1)
    dwin_lo = seg_lo - head
    dwin_rows = jnp.where(seg_n > 0, -(-(seg_lo + seg_n - dwin_lo) // F32_TILE) * F32_TILE, 0)
    dwin_src = jnp.cumsum(dwin_rows, axis=1) - dwin_rows
    dcol_off = jnp.pad(dwin_src - dwin_lo + pstart[None, :], ((0, 0), (0, LANES - N_EXPERTS)))
    dcol_off = dcol_off.reshape(-1, 1, LANES).astype(jnp.int32)
    xs = _dispatch(flat(dwin_src), flat(dwin_lo), flat(dwin_rows // F32_TILE), flat(jnp.where(seg_n > 0, head, 0)),
                   pend.astype(jnp.int32), npad.astype(jnp.int32), e_idx, rank, dcol_off, h2, n_slots, tok)
    ys = _experts(block_e, n_used, xs, wl, layer)
    return _combine(flat(win_lo), flat(win_dst), flat(win_rows // ROW_TILE), e_idx, rank, gates, col_off,
                    x1, mod, final_g, ys, seq_len, tok, final)


def _encoder(x, c, w, chan_dft, layers, tables, tiles=None):
    bsz, s, _ = x.shape
    tiles = tiles or _tiles(bsz, s)
    cos_t, nsin_t = tables
    depth = len(layers)
    final_g = w["final_g"].reshape(1, D_MODEL)
    ada_b = w["ada_b"].reshape(depth, 1, 6 * D_MODEL)
    for l, wl in enumerate(layers):
        mod = _ada(c, w["ada_w"], ada_b, l).reshape(bsz, 6, D_MODEL)
        mod = jnp.pad(mod, ((0, 0), (0, 2), (0, 0)))
        p, u = _inproj(x, mod, wl["norm1_g"], wl["w_in"], chan_dft, tiles["seq"])
        y = _seqdft(cos_t, nsin_t, p, tiles["dft_m"], tiles["dft_k"])
        x1, h2, logits = _mix(y, u, x, mod, wl, tiles["seq"])
        x = _moe(h2.reshape(bsz * s, D_MODEL), logits.reshape(bsz * s, LANES),
                 x1.reshape(bsz * s, D_MODEL), mod, final_g, wl, l, s, tiles,
                 final=(l == depth - 1)).reshape(bsz, s, D_MODEL)
    return x


def _seq_tables(s):
    j = jnp.arange(s, dtype=jnp.int32)[:, None]

    def cos_sin(k):
        ang = ((j * k[None, :]) % s).astype(F32) * (2.0 * math.pi / s)
        return jnp.cos(ang)[:, :, None], jnp.sin(ang)[:, :, None]

    ca, sa = cos_sin(jnp.arange(s // LANES, dtype=jnp.int32) * LANES)
    cb, sb = cos_sin(jnp.arange(LANES, dtype=jnp.int32))
    cb, sb = cb.reshape(s, 1, LANES), sb.reshape(s, 1, LANES)
    scale = 1.0 / math.sqrt(s)
    cos_t = (ca * cb - sa * sb) * scale
    nsin_t = (sa * cb + ca * sb) * (-scale)
    return cos_t.reshape(s, s).astype(BF16), nsin_t.reshape(s, s).astype(BF16)


def kernel(x_prompt, x_sample, c_prompt, c_sample, norm1_g, norm2_g, ada_w, ada_b, w_in, fourier_w, conv31_w, conv31_b, gn_g, gn_b, pw_w, pool_w, pool_scale, conv3_w, sconv_out_w, w_o, router_w, router_b, exp_gate_w, exp_gate_b, exp_up_w, exp_up_b, exp_down_w, exp_down_b, final_g):
    w = dict(norm1_g=norm1_g, norm2_g=norm2_g, ada_w=ada_w, ada_b=ada_b, w_in=w_in, fourier_w=fourier_w,
             conv31_w=conv31_w, conv31_b=conv31_b, gn_g=gn_g, gn_b=gn_b, pw_w=pw_w, pool_w=pool_w,
             pool_scale=pool_scale, conv3_w=conv3_w, sconv_out_w=sconv_out_w, w_o=w_o, router_w=router_w,
             router_b=router_b, exp_gate_w=exp_gate_w, exp_gate_b=exp_gate_b, exp_up_w=exp_up_w,
             exp_up_b=exp_up_b, exp_down_w=exp_down_w, exp_down_b=exp_down_b, final_g=final_g)
    chan_dft, layers = _prep_weights(w)
    y_prompt = _encoder(x_prompt, c_prompt, w, chan_dft, layers, _seq_tables(x_prompt.shape[1]))
    y_sample = _encoder(x_sample, c_sample, w, chan_dft, layers, _seq_tables(x_sample.shape[1]))
    return (y_prompt, y_sample)
```

```python
import functools
import math

import jax
import jax.numpy as jnp
from jax import lax
from jax.experimental import pallas as pl
from jax.experimental.pallas import tpu as pltpu

D_MODEL = 1024
GROUP = 256
HEAD_DIM = 64
CONV31 = 31
POOL_WINDOWS = (2, 4, 8, 16)
N_EXPERTS = 32
TOP_K = 4
D_FF = 1024
SWIGLU_ALPHA = 1.702
SWIGLU_LIMIT = 7.0
MOE_BLOCK = 512
RMS_EPS = 1e-6
GN_EPS = 1e-5
HALO = 16
LANES = 128
VMEM_LIMIT = 48 * 1024 * 1024

F32 = jnp.float32
BF16 = jnp.bfloat16


def _cparams(sem):
    return pltpu.CompilerParams(dimension_semantics=sem, vmem_limit_bytes=VMEM_LIMIT)


def _dot(a, b):
    return jnp.dot(a, b, preferred_element_type=F32)


def _split_bf16(x):
    hi = x.astype(BF16)
    lo = (x - hi.astype(F32)).astype(BF16)
    return hi, lo


def _dot_hilo(x, w_bf16):
    hi, lo = _split_bf16(x)
    return _dot(hi, w_bf16) + _dot(lo, w_bf16)


def _sigmoid(x):
    return 1.0 / (1.0 + jnp.exp(-x))


def _ada_kernel(c_ref, w_ref, b_ref, o_ref):
    c = c_ref[...]
    cs = c * _sigmoid(c)
    c_hi, c_lo = _split_bf16(cs)
    w_hi, w_lo = _split_bf16(w_ref[0])
    o_ref[...] = _dot(c_hi, w_hi) + _dot(c_lo, w_hi) + _dot(c_hi, w_lo) + b_ref[0]


def _ada(c, ada_w, ada_b, layer):
    bsz = c.shape[0]
    n_out = ada_w.shape[2]
    return pl.pallas_call(
        _ada_kernel,
        out_shape=jax.ShapeDtypeStruct((bsz, n_out), F32),
        grid=(n_out // D_MODEL,),
        in_specs=[
            pl.BlockSpec((bsz, D_MODEL), lambda j: (0, 0)),
            pl.BlockSpec((1, D_MODEL, D_MODEL), lambda j: (layer, 0, j)),
            pl.BlockSpec((1, 1, D_MODEL), lambda j: (layer, 0, j)),
        ],
        out_specs=pl.BlockSpec((bsz, D_MODEL), lambda j: (0, j)),
        compiler_params=_cparams(("arbitrary",)),
        name="ada",
    )(c, ada_w, ada_b)


def _rms_mod(x, g, scale, shift):
    ms = jnp.mean(x * x, axis=-1, keepdims=True)
    y = x * lax.rsqrt(ms + RMS_EPS) * g
    return y * (1.0 + scale) + shift


def _inproj_kernel(x_ref, mod_ref, g_ref, w_ref, cs_ref, p_ref, u_ref):
    x = x_ref[0]
    h = _rms_mod(x, g_ref[...], mod_ref[0, 1:2, :], mod_ref[0, 0:1, :]).astype(BF16)
    a = _dot(h, w_ref[:, 0:GROUP]).astype(BF16)
    p_ref[0] = _dot(a, cs_ref[...]).astype(BF16)
    for j in range(6):
        u_ref[0, :, j * GROUP:(j + 1) * GROUP] = _dot(
            h, w_ref[:, (j + 1) * GROUP:(j + 2) * GROUP]).astype(BF16)


def _inproj(x, mod, g, w_in, chan_dft, tm):
    bsz, s, _ = x.shape
    return pl.pallas_call(
        _inproj_kernel,
        out_shape=(jax.ShapeDtypeStruct((bsz, s, 2 * GROUP), BF16),
                   jax.ShapeDtypeStruct((bsz, s, 6 * GROUP), BF16)),
        grid=(bsz, s // tm),
        in_specs=[
            pl.BlockSpec((1, tm, D_MODEL), lambda b, i: (b, i, 0)),
            pl.BlockSpec((1, 8, D_MODEL), lambda b, i: (b, 0, 0)),
            pl.BlockSpec((1, D_MODEL), lambda b, i: (0, 0)),
            pl.BlockSpec((D_MODEL, 7 * GROUP), lambda b, i: (0, 0)),
            pl.BlockSpec((GROUP, 2 * GROUP), lambda b, i: (0, 0)),
        ],
        out_specs=(pl.BlockSpec((1, tm, 2 * GROUP), lambda b, i: (b, i, 0)),
                   pl.BlockSpec((1, tm, 6 * GROUP), lambda b, i: (b, i, 0))),
        compiler_params=_cparams(("arbitrary", "arbitrary")),
        name="inproj",
    )(x, mod, g, w_in, chan_dft)


def _seqdft_kernel(c_ref, s_ref, p_ref, o_ref, acc_ref):
    k = pl.program_id(1)
    b = pl.program_id(2)
    contrib = _dot(c_ref[...], p_ref[0, :, 0:GROUP]) + _dot(s_ref[...], p_ref[0, :, GROUP:2 * GROUP])

    @pl.when(k == 0)
    def _():
        acc_ref[b] = contrib

    @pl.when(k > 0)
    def _():
        acc_ref[b] = acc_ref[b] + contrib

    @pl.when(k == pl.num_programs(1) - 1)
    def _():
        o_ref[b] = acc_ref[b].astype(BF16)


def _seqdft(cos_t, nsin_t, p, tm, tk):
    bsz, s, _ = p.shape
    return pl.pallas_call(
        _seqdft_kernel,
        out_shape=jax.ShapeDtypeStruct((bsz, s, GROUP), BF16),
        grid=(s // tm, s // tk, bsz),
        in_specs=[
            pl.BlockSpec((tm, tk), lambda i, k, b: (i, k)),
            pl.BlockSpec((tm, tk), lambda i, k, b: (i, k)),
            pl.BlockSpec((1, tk, 2 * GROUP), lambda i, k, b: (b, k, 0)),
        ],
        out_specs=pl.BlockSpec((bsz, tm, GROUP), lambda i, k, b: (0, i, 0)),
        scratch_shapes=[pltpu.VMEM((bsz, tm, GROUP), F32)],
        compiler_params=_cparams(("arbitrary", "arbitrary", "arbitrary")),
        name="seqdft",
    )(cos_t, nsin_t, p)


U_V, U_G, U_P, U_BG, U_CG, U_XV = (j * GROUP for j in range(6))


def _mix_kernel(y_ref, u_ref, up_ref, un_ref, x_ref, mod_ref,
                fw_ref, c31w_ref, c31b_ref, gng_ref, gnb_ref, avg_ref, pww_ref,
                poolw_ref, pscale_ref, c3w_ref, sow_ref, wo_ref, n2g_ref, rw_ref, rb_ref,
                x1_ref, h2_ref, lg_ref,
                zext, pext, s2, s4, s8, s16, qext, cat, *, tq, seq_len):
    i = pl.program_id(1)
    has_prev = (i > 0).astype(F32)
    has_next = (i < pl.num_programs(1) - 1).astype(F32)
    rows = tq + 2 * HALO

    def cols(ref, c0):
        return ref[0, :, c0:c0 + GROUP].astype(F32)

    def fill_ext(dst, fn):
        dst[0:HALO, :] = fn(up_ref) * has_prev
        dst[HALO:HALO + tq, :] = fn(u_ref)
        dst[HALO + tq:rows, :] = fn(un_ref) * has_next

    cat[:, 0:GROUP] = _dot(y_ref[0], fw_ref[...]).astype(BF16)

    fill_ext(zext, lambda r: cols(r, U_V) * _sigmoid(cols(r, U_G)))
    conv = jnp.broadcast_to(c31b_ref[...], (tq, GROUP))
    for j in range(CONV31):
        conv = conv + c31w_ref[j:j + 1, :] * zext[pl.ds(HALO - CONV31 // 2 + j, tq), :]
    mu = _dot_hilo(conv, avg_ref[...])
    dev = conv - mu
    var = _dot_hilo(dev * dev, avg_ref[...])
    zn = dev * lax.rsqrt(var + GN_EPS) * gng_ref[...] + gnb_ref[...]
    cat[:, GROUP:2 * GROUP] = _dot((zn * _sigmoid(zn)).astype(BF16), pww_ref[...]).astype(BF16)

    fill_ext(pext, lambda r: cols(r, U_P))
    n = rows - 8
    zeros8 = jnp.zeros((8, GROUP), F32)
    s2[0:n, :] = pext[0:n, :] + pext[pl.ds(1, n), :]
    s2[n:rows, :] = zeros8
    s4[0:n, :] = s2[0:n, :] + s2[pl.ds(2, n), :]
    s4[n:rows, :] = zeros8
    s8[0:n, :] = s4[0:n, :] + s4[pl.ds(4, n), :]
    s8[n:rows, :] = zeros8
    s16[0:n, :] = s8[0:n, :] + s8[pl.ds(8, n), :]
    lane = lax.broadcasted_iota(jnp.int32, (tq, GROUP), 1)
    pos = lax.broadcasted_iota(jnp.int32, (tq, GROUP), 0) + i * tq
    win = jnp.where(lane < HEAD_DIM, s2[pl.ds(HALO - 1, tq), :],
                    jnp.where(lane < 2 * HEAD_DIM, s4[pl.ds(HALO - 2, tq), :],
                              jnp.where(lane < 3 * HEAD_DIM, s8[pl.ds(HALO - 4, tq), :],
                                        s16[pl.ds(HALO - 8, tq), :])))
    half = jnp.where(lane < HEAD_DIM, 1,
                     jnp.where(lane < 2 * HEAD_DIM, 2, jnp.where(lane < 3 * HEAD_DIM, 4, 8)))
    cnt = jnp.minimum(pos + half, seq_len) - jnp.maximum(pos - half, 0)
    dpool = win / cnt.astype(F32) - pext[HALO:HALO + tq, :]
    cat[:, 2 * GROUP:3 * GROUP] = (_dot(dpool.astype(BF16), poolw_ref[...]) * pscale_ref[...]).astype(BF16)

    fill_ext(qext, lambda r: cols(r, U_CG) * cols(r, U_XV))
    c3 = (c3w_ref[0:1, :] * qext[pl.ds(HALO - 1, tq), :]
          + c3w_ref[1:2, :] * qext[HALO:HALO + tq, :]
          + c3w_ref[2:3, :] * qext[pl.ds(HALO + 1, tq), :])
    cat[:, 3 * GROUP:4 * GROUP] = _dot((cols(u_ref, U_BG) * c3).astype(BF16), sow_ref[...]).astype(BF16)

    mixed = _dot(cat[...], wo_ref[...])
    x1 = x_ref[0] + mod_ref[0, 2:3, :] * mixed
    x1_ref[0] = x1
    h2 = _rms_mod(x1, n2g_ref[...], mod_ref[0, 4:5, :], mod_ref[0, 3:4, :])
    h2_ref[0] = h2
    lg_ref[0] = _dot(h2.astype(BF16), rw_ref[...]) + rb_ref[...]


def _mix(y, u, x, mod, wl, tq):
    bsz, s, _ = x.shape
    rows = tq + 2 * HALO
    hb = tq // HALO
    n_hblk = s // HALO
    full = lambda shape: pl.BlockSpec(shape, lambda b, i: tuple(0 for _ in shape))
    kern = functools.partial(_mix_kernel, tq=tq, seq_len=s)
    return pl.pallas_call(
        kern,
        out_shape=(jax.ShapeDtypeStruct((bsz, s, D_MODEL), F32),
                   jax.ShapeDtypeStruct((bsz, s, D_MODEL), F32),
                   jax.ShapeDtypeStruct((bsz, s, LANES), F32)),
        grid=(bsz, s // tq),
        in_specs=[
            pl.BlockSpec((1, tq, GROUP), lambda b, i: (b, i, 0)),
            pl.BlockSpec((1, tq, 6 * GROUP), lambda b, i: (b, i, 0)),
            pl.BlockSpec((1, HALO, 6 * GROUP), lambda b, i: (b, jnp.maximum(i * hb - 1, 0), 0)),
            pl.BlockSpec((1, HALO, 6 * GROUP), lambda b, i: (b, jnp.minimum((i + 1) * hb, n_hblk - 1), 0)),
            pl.BlockSpec((1, tq, D_MODEL), lambda b, i: (b, i, 0)),
            pl.BlockSpec((1, 8, D_MODEL), lambda b, i: (b, 0, 0)),
            full((GROUP, GROUP)),
            full((32, GROUP)),
            full((1, GROUP)),
            full((1, GROUP)),
            full((1, GROUP)),
            full((GROUP, GROUP)),
            full((GROUP, GROUP)),
            full((GROUP, GROUP)),
            full((1, GROUP)),
            full((8, GROUP)),
            full((GROUP, GROUP)),
            full((D_MODEL, D_MODEL)),
            full((1, D_MODEL)),
            full((D_MODEL, LANES)),
            full((1, LANES)),
        ],
        out_specs=(pl.BlockSpec((1, tq, D_MODEL), lambda b, i: (b, i, 0)),
                   pl.BlockSpec((1, tq, D_MODEL), lambda b, i: (b, i, 0)),
                   pl.BlockSpec((1, tq, LANES), lambda b, i: (b, i, 0))),
        scratch_shapes=[pltpu.VMEM((rows, GROUP), F32) for _ in range(7)]
        + [pltpu.VMEM((tq, D_MODEL), BF16)],
        compiler_params=_cparams(("arbitrary", "arbitrary")),
        name="mix",
    )(y, u, u, u, x, mod, wl["fourier_w"], wl["conv31_w"], wl["conv31_b"], wl["gn_g"], wl["gn_b"],
      wl["avg"], wl["pw_w"], wl["pool_w"], wl["pool_scale"], wl["conv3_w"], wl["sconv_out_w"],
      wl["w_o"], wl["norm2_g"], wl["router_w"], wl["router_b"])


def _route_kernel(lg_ref, e_ref, rank_ref, gate_ref, base_ref, tcnt_ref, cnt_ref, carry, *, tr):
    i = pl.program_id(0)

    @pl.when(i == 0)
    def _():
        carry[...] = jnp.zeros_like(carry)

    lane = lax.broadcasted_iota(jnp.int32, (tr, LANES), 1).astype(F32)
    neg = jnp.float32(-jnp.inf)
    work = jnp.where(lane < N_EXPERTS, lg_ref[...], neg)
    vals, idxs, hots = [], [], []
    for _ in range(TOP_K):
        m = jnp.max(work, axis=1, keepdims=True)
        idx = jnp.min(jnp.where(work == m, lane, float(LANES)), axis=1, keepdims=True)
        hot = lane == idx
        vals.append(m)
        idxs.append(idx.astype(jnp.int32))
        hots.append(hot)
        work = jnp.where(hot, neg, work)
    exps = [jnp.exp(v - vals[0]) for v in vals]
    denom = exps[0] + exps[1] + exps[2] + exps[3]
    member = sum(h.astype(F32) for h in hots)
    r_io = lax.broadcasted_iota(jnp.int32, (tr, tr), 0)
    c_io = lax.broadcasted_iota(jnp.int32, (tr, tr), 1)
    lower = (c_io < r_io).astype(BF16)
    before = _dot(lower, member.astype(BF16)) + carry[...]
    col = lax.broadcasted_iota(jnp.int32, (tr, TOP_K), 1)
    e_out = jnp.zeros((tr, TOP_K), jnp.int32)
    r_out = jnp.zeros((tr, TOP_K), jnp.int32)
    g_out = jnp.zeros((tr, TOP_K), F32)
    for k in range(TOP_K):
        rk = jnp.sum(jnp.where(hots[k], before, 0.0), axis=1, keepdims=True).astype(jnp.int32)
        e_out = jnp.where(col == k, idxs[k], e_out)
        r_out = jnp.where(col == k, rk, r_out)
        g_out = jnp.where(col == k, exps[k] / denom, g_out)
    e_ref[...] = e_out
    rank_ref[...] = r_out
    gate_ref[...] = g_out
    tile_cnt = jnp.sum(member, axis=0, keepdims=True)
    base_ref[0] = carry[...].astype(jnp.int32)
    tcnt_ref[0] = tile_cnt.astype(jnp.int32)
    carry[...] = carry[...] + tile_cnt
    cnt_ref[...] = carry[...]


def _route(logits, tr):
    t_tok = logits.shape[0]
    n_tiles = t_tok // tr
    kern = functools.partial(_route_kernel, tr=tr)
    tile = lambda w: pl.BlockSpec((tr, w), lambda i: (i, 0))
    per_tile = pl.BlockSpec((1, 1, LANES), lambda i: (i, 0, 0))
    return pl.pallas_call(
        kern,
        out_shape=(jax.ShapeDtypeStruct((t_tok, TOP_K), jnp.int32),
                   jax.ShapeDtypeStruct((t_tok, TOP_K), jnp.int32),
                   jax.ShapeDtypeStruct((t_tok, TOP_K), F32),
                   jax.ShapeDtypeStruct((n_tiles, 1, LANES), jnp.int32),
                   jax.ShapeDtypeStruct((n_tiles, 1, LANES), jnp.int32),
                   jax.ShapeDtypeStruct((1, LANES), F32)),
        grid=(n_tiles,),
        in_specs=[tile(LANES)],
        out_specs=(tile(TOP_K), tile(TOP_K), tile(TOP_K), per_tile, per_tile,
                   pl.BlockSpec((1, LANES), lambda i: (0, 0))),
        scratch_shapes=[pltpu.VMEM((1, LANES), F32)],
        compiler_params=_cparams(("arbitrary",)),
        name="route",
    )(logits)


F32_TILE = 8
DISP_ROWS = TOP_K * 256 + N_EXPERTS * 2 * F32_TILE
CARRY_ROWS = N_EXPERTS * F32_TILE
DISP_WIN = 64


def _dispatch_kernel(src_ref, dst_ref, nch_ref, head_ref, pend_ref, npad_ref, e_ref, rank_ref, off_ref, h_ref,
                     xs_hbm, outbuf, carry, zeros, sems, sem_z, *, td):
    i = pl.program_id(0)
    n_steps = pl.num_programs(0)
    slot = i % 2

    def zero_blk(row):
        return pltpu.make_async_copy(zeros, xs_hbm.at[pl.ds(pl.multiple_of(row, MOE_BLOCK), MOE_BLOCK)], sem_z)

    @pl.when(i == 0)
    def _():
        zeros[...] = jnp.zeros_like(zeros)
        carry[...] = jnp.zeros_like(carry)
        outbuf[...] = jnp.zeros_like(outbuf)
        first_unused = pend_ref[N_EXPERTS - 1] // MOE_BLOCK
        n_blocks = xs_hbm.shape[0] // MOE_BLOCK

        def start(e, c):
            @pl.when(npad_ref[e] > 0)
            def _():
                zero_blk(pend_ref[e] - MOE_BLOCK).start()

            @pl.when(npad_ref[e] > MOE_BLOCK)
            def _():
                zero_blk(pend_ref[e] - 2 * MOE_BLOCK).start()
            return c

        def wait(e, c):
            @pl.when(npad_ref[e] > 0)
            def _():
                zero_blk(pend_ref[e] - MOE_BLOCK).wait()

            @pl.when(npad_ref[e] > MOE_BLOCK)
            def _():
                zero_blk(pend_ref[e] - 2 * MOE_BLOCK).wait()
            return c

        def start_tail(b, c):
            zero_blk(b * MOE_BLOCK).start()
            return c

        def wait_tail(b, c):
            zero_blk(b * MOE_BLOCK).wait()
            return c

        lax.fori_loop(0, N_EXPERTS, start, 0)
        lax.fori_loop(first_unused, n_blocks, start_tail, 0)
        lax.fori_loop(0, N_EXPERTS, wait, 0)
        lax.fori_loop(first_unused, n_blocks, wait_tail, 0)

    def window_copies(tile, sl, act):
        for e in range(N_EXPERTS):
            n_ch = nch_ref[tile * N_EXPERTS + e]
            src0 = pl.multiple_of(src_ref[tile * N_EXPERTS + e], F32_TILE)
            dst0 = pl.multiple_of(dst_ref[tile * N_EXPERTS + e], F32_TILE)

            @pl.when(n_ch > 0)
            def _():
                act(pltpu.make_async_copy(outbuf.at[sl, pl.ds(src0, DISP_WIN)],
                                          xs_hbm.at[pl.ds(dst0, DISP_WIN)], sems.at[sl]))

            def per_chunk(ci, c2):
                src = pl.multiple_of(src0 + ci * F32_TILE, F32_TILE)
                dst = pl.multiple_of(dst0 + ci * F32_TILE, F32_TILE)
                act(pltpu.make_async_copy(outbuf.at[sl, pl.ds(src, F32_TILE)],
                                          xs_hbm.at[pl.ds(dst, F32_TILE)], sems.at[sl]))
                return c2

            lax.fori_loop(DISP_WIN // F32_TILE, n_ch, per_chunk, 0)

    lane_td = lax.broadcasted_iota(jnp.int32, (td, LANES), 1)
    off_f = off_ref[0].astype(F32)
    pos_cols = jnp.full((td, LANES), -1.0, F32)
    for k in range(TOP_K):
        off_k = jnp.sum(jnp.where(lane_td == e_ref[:, k:k + 1], off_f, 0.0), axis=1, keepdims=True)
        pos_cols = jnp.where(lane_td == k, rank_ref[:, k:k + 1].astype(F32) + off_k, pos_cols)
    pos_t = jnp.transpose(pos_cols).astype(jnp.int32)
    pos = [jnp.broadcast_to(pos_t[k:k + 1, :], (GROUP, td)) for k in range(TOP_K)]
    row_io = lax.broadcasted_iota(jnp.int32, (GROUP, td), 0)
    h_bf = h_ref[...].astype(BF16)
    for rc in range(DISP_ROWS // GROUP):
        r = row_io + rc * GROUP
        tok = jnp.zeros((GROUP, td), F32)
        for k in range(TOP_K):
            tok = jnp.where(r == pos[k], 1.0, tok)
        outbuf[slot, rc * GROUP:(rc + 1) * GROUP, :] = _dot(tok.astype(BF16), h_bf)

    sub_io = lax.broadcasted_iota(jnp.int32, (F32_TILE, D_MODEL), 0)
    for e in range(N_EXPERTS):
        es = slice(e * F32_TILE, (e + 1) * F32_TILE)
        n_ch = nch_ref[i * N_EXPERTS + e]
        first = pl.multiple_of(src_ref[i * N_EXPERTS + e], F32_TILE)
        outbuf[slot, pl.ds(first, F32_TILE), :] = jnp.where(
            sub_io < head_ref[i * N_EXPERTS + e], carry[es, :], outbuf[slot, pl.ds(first, F32_TILE), :])
        last = pl.multiple_of(first + jnp.maximum(n_ch - 1, 0) * F32_TILE, F32_TILE)
        carry[es, :] = jnp.where(n_ch > 0, outbuf[slot, pl.ds(last, F32_TILE), :], carry[es, :])

    @pl.when(i > 0)
    def _():
        window_copies(i - 1, 1 - slot, lambda cp: cp.wait())

    window_copies(i, slot, lambda cp: cp.start())

    @pl.when(i == n_steps - 1)
    def _():
        window_copies(i, slot, lambda cp: cp.wait())


def _dispatch(chunk_src, chunk_dst, n_chunks, head, pend, npad, e_idx, rank, col_off, h2, n_slots, td):
    t_tok = h2.shape[0]
    assert DISP_ROWS >= TOP_K * td + N_EXPERTS * 2 * (F32_TILE - 1) + F32_TILE and DISP_ROWS % GROUP == 0
    kern = functools.partial(_dispatch_kernel, td=td)
    tok4 = lambda: pl.BlockSpec((td, TOP_K), lambda i, *_: (i, 0))
    return pl.pallas_call(
        kern,
        out_shape=jax.ShapeDtypeStruct((n_slots, D_MODEL), F32),
        grid_spec=pltpu.PrefetchScalarGridSpec(
            num_scalar_prefetch=6,
            grid=(t_tok // td,),
            in_specs=[tok4(), tok4(),
                      pl.BlockSpec((1, 1, LANES), lambda i, *_: (i, 0, 0)),
                      pl.BlockSpec((td, D_MODEL), lambda i, *_: (i, 0))],
            out_specs=pl.BlockSpec(memory_space=pl.ANY),
            scratch_shapes=[pltpu.VMEM((2, DISP_ROWS + DISP_WIN, D_MODEL), F32),
                            pltpu.VMEM((CARRY_ROWS, D_MODEL), F32),
                            pltpu.VMEM((MOE_BLOCK, D_MODEL), F32),
                            pltpu.SemaphoreType.DMA((2,)),
                            pltpu.SemaphoreType.DMA]),
        compiler_params=_cparams(("arbitrary",)),
        name="dispatch",
    )(chunk_src, chunk_dst, n_chunks, head, pend, npad, e_idx, rank, col_off, h2)


def _experts_kernel(be_ref, nu_ref, xs_ref, wg_ref, bg_ref, wu_ref, bu_ref, wd_ref, bd_ref, o_ref, act):
    i = pl.program_id(0)

    @pl.when(i < nu_ref[0])
    def _():
        x = xs_ref[...].astype(BF16)
        for f in range(D_FF // GROUP):
            fs = slice(f * GROUP, (f + 1) * GROUP)
            gt = jnp.minimum(_dot(x, wg_ref[0, :, fs]) + bg_ref[0, :, fs], SWIGLU_LIMIT)
            up = jnp.clip(_dot(x, wu_ref[0, :, fs]) + bu_ref[0, :, fs], -SWIGLU_LIMIT, SWIGLU_LIMIT)
            act[:, fs] = ((up + 1.0) * (gt * _sigmoid(SWIGLU_ALPHA * gt))).astype(BF16)
        o_ref[...] = (_dot(act[...], wd_ref[0]) + bd_ref[0]).astype(BF16)

    @pl.when(i >= nu_ref[0])
    def _():
        o_ref[...] = jnp.zeros_like(o_ref)


def _experts(block_e, n_used, xs, wl, layer):
    n_slots = xs.shape[0]
    n_blocks = n_slots // MOE_BLOCK

    def expert(i, be, nu):
        return layer * N_EXPERTS + be[jnp.minimum(i, nu[0] - 1)]

    wspec = lambda: pl.BlockSpec((1, D_MODEL, D_FF), lambda i, be, nu: (expert(i, be, nu), 0, 0))
    bspec = lambda: pl.BlockSpec((1, 1, D_FF), lambda i, be, nu: (expert(i, be, nu), 0, 0))
    return pl.pallas_call(
        _experts_kernel,
        out_shape=jax.ShapeDtypeStruct((n_slots, D_MODEL), BF16),
        grid_spec=pltpu.PrefetchScalarGridSpec(
            num_scalar_prefetch=2,
            grid=(n_blocks,),
            in_specs=[pl.BlockSpec((MOE_BLOCK, D_MODEL), lambda i, be, nu: (jnp.minimum(i, nu[0] - 1), 0)),
                      wspec(), bspec(), wspec(), bspec(), wspec(), bspec()],
            out_specs=pl.BlockSpec((MOE_BLOCK, D_MODEL), lambda i, be, nu: (i, 0)),
            scratch_shapes=[pltpu.VMEM((MOE_BLOCK, D_FF), BF16)]),
        compiler_params=_cparams(("arbitrary",)),
        name="experts",
    )(block_e, n_used, xs, wl["exp_gate_w"], wl["exp_gate_b"], wl["exp_up_w"], wl["exp_up_b"],
      wl["exp_down_w"], wl["exp_down_b"])


ROW_TILE = 16
SEG_CAP = 64
SEG_WIN = SEG_CAP + ROW_TILE
K_CHUNK = 512


def _combine_kernel(seg_ref, rounds_ref, e_ref, rank_ref, gate_ref, base_ref, pstart_ref, x1_ref, mod_ref,
                    fg_ref, ys_hbm, o_ref, segbuf, acc, sems, *, tc, n_slots, final):
    j = pl.program_id(0)
    n_tiles = pl.num_programs(0)
    slot = j % 2

    def window_start(s):
        return jnp.minimum(s - (s & (ROW_TILE - 1)), n_slots - SEG_WIN)

    def windows(tile, r, sl, act):
        for e in range(N_EXPERTS):
            a = pl.multiple_of(window_start(seg_ref[tile * N_EXPERTS + e] + r * SEG_CAP), ROW_TILE)
            act(pltpu.make_async_copy(ys_hbm.at[pl.ds(a, SEG_WIN)], segbuf.at[sl, e], sems.at[sl]))

    @pl.when(j == 0)
    def _():
        windows(0, 0, 0, lambda cp: cp.start())

    @pl.when(j + 1 < n_tiles)
    def _():
        windows(j + 1, 0, 1 - slot, lambda cp: cp.start())

    lane_row = lax.broadcasted_iota(jnp.int32, (1, LANES), 1)
    lane_tc = lax.broadcasted_iota(jnp.int32, (tc, LANES), 1)
    pstart_v = pstart_ref[...]
    base_v = base_ref[0]

    def contribution(r, sl):
        lo_v = base_v + r * SEG_CAP
        off_f = (lane_row * SEG_WIN - window_start(pstart_v + lo_v) + pstart_v).astype(F32)
        lo_f = lo_v.astype(F32)
        colb, gateb = [], []
        for k in range(TOP_K):
            sel = lane_tc == e_ref[:, k:k + 1]
            off_k = jnp.sum(jnp.where(sel, off_f, 0.0), axis=1, keepdims=True).astype(jnp.int32)
            lo_k = jnp.sum(jnp.where(sel, lo_f, 0.0), axis=1, keepdims=True).astype(jnp.int32)
            rk = rank_ref[:, k:k + 1]
            col_k = jnp.where((rk >= lo_k) & (rk < lo_k + SEG_CAP), rk + off_k, -1)
            colb.append(jnp.broadcast_to(col_k, (tc, LANES)))
            gateb.append(jnp.broadcast_to(gate_ref[:, k:k + 1], (tc, LANES)))
        seg = segbuf[sl].reshape(N_EXPERTS * SEG_WIN, D_MODEL)
        out = jnp.zeros((tc, D_MODEL), F32)
        for c in range(N_EXPERTS * SEG_WIN // K_CHUNK):
            pieces = []
            for p in range(c * K_CHUNK // LANES, (c + 1) * K_CHUNK // LANES):
                col_io = lane_tc + p * LANES
                g = jnp.zeros((tc, LANES), F32)
                for k in reversed(range(TOP_K)):
                    g = jnp.where(col_io == colb[k], gateb[k], g)
                pieces.append(g.astype(BF16))
            out = out + _dot(jnp.concatenate(pieces, axis=1), seg[c * K_CHUNK:(c + 1) * K_CHUNK, :])
        return out

    windows(j, 0, slot, lambda cp: cp.wait())
    acc[...] = contribution(0, slot)

    def extra_round(r, c):
        windows(j, r, slot, lambda cp: cp.start())
        windows(j, r, slot, lambda cp: cp.wait())
        acc[...] = acc[...] + contribution(r, slot)
        return c

    lax.fori_loop(1, rounds_ref[j], extra_round, 0)

    x2 = x1_ref[...] + mod_ref[0, 5:6, :] * acc[...]
    if final:
        ms = jnp.mean(x2 * x2, axis=-1, keepdims=True)
        x2 = x2 * lax.rsqrt(ms + RMS_EPS) * fg_ref[...]
    o_ref[...] = x2


def _combine(seg_start, rounds, e_idx, rank, gates, base, pstart_v, x1, mod, final_g, ys, seq_len, tc, final):
    t_tok = x1.shape[0]
    n_slots = ys.shape[0]
    assert (N_EXPERTS * SEG_WIN) % K_CHUNK == 0 and K_CHUNK % LANES == 0
    kern = functools.partial(_combine_kernel, tc=tc, n_slots=n_slots, final=final)
    tok4 = lambda: pl.BlockSpec((tc, TOP_K), lambda i, *_: (i, 0))
    return pl.pallas_call(
        kern,
        out_shape=jax.ShapeDtypeStruct((t_tok, D_MODEL), F32),
        grid_spec=pltpu.PrefetchScalarGridSpec(
            num_scalar_prefetch=2,
            grid=(t_tok // tc,),
            in_specs=[tok4(), tok4(), tok4(),
                      pl.BlockSpec((1, 1, LANES), lambda i, *_: (i, 0, 0)),
                      pl.BlockSpec((1, LANES), lambda i, *_: (0, 0)),
                      pl.BlockSpec((tc, D_MODEL), lambda i, *_: (i, 0)),
                      pl.BlockSpec((1, 8, D_MODEL), lambda i, *_: ((i * tc) // seq_len, 0, 0)),
                      pl.BlockSpec((1, D_MODEL), lambda i, *_: (0, 0)),
                      pl.BlockSpec(memory_space=pl.ANY)],
            out_specs=pl.BlockSpec((tc, D_MODEL), lambda i, *_: (i, 0)),
            scratch_shapes=[pltpu.VMEM((2, N_EXPERTS, SEG_WIN, D_MODEL), BF16),
                            pltpu.VMEM((tc, D_MODEL), F32),
                            pltpu.SemaphoreType.DMA((2,))]),
        compiler_params=_cparams(("arbitrary",)),
        name="combine",
    )(seg_start, rounds, e_idx, rank, gates, base, pstart_v, x1, mod, final_g, ys)


def _dft_tables(n):
    j = jnp.arange(n, dtype=jnp.int32)
    ang = ((j[:, None] * j[None, :]) % n).astype(F32) * (2.0 * math.pi / n)
    scale = 1.0 / math.sqrt(n)
    return jnp.cos(ang) * scale, jnp.sin(ang) * scale


def _block_diag(blocks):
    n, r, c = blocks.shape
    rows = lax.broadcasted_iota(jnp.int32, (n * r, n * c), 0) // r
    cols = lax.broadcasted_iota(jnp.int32, (n * r, n * c), 1) // c
    return jnp.where(rows == cols, jnp.tile(blocks.reshape(n * r, c), (1, n)), 0)


def _prep_weights(w):
    depth = w["w_in"].shape[0]
    c64, s64 = _dft_tables(HEAD_DIM)
    n_heads = GROUP // HEAD_DIM
    chan_dft = jnp.concatenate([_block_diag(jnp.stack([c64] * n_heads)),
                                _block_diag(jnp.stack([s64] * n_heads))], axis=1).astype(BF16)
    avg = _block_diag(jnp.full((n_heads, HEAD_DIM, HEAD_DIM), 1.0 / HEAD_DIM, F32)).astype(BF16)
    stacked = dict(
        exp_gate_w=w["exp_gate_w"].astype(BF16).reshape(depth * N_EXPERTS, D_MODEL, D_FF),
        exp_up_w=w["exp_up_w"].astype(BF16).reshape(depth * N_EXPERTS, D_MODEL, D_FF),
        exp_down_w=w["exp_down_w"].astype(BF16).reshape(depth * N_EXPERTS, D_FF, D_MODEL),
        exp_gate_b=w["exp_gate_b"].reshape(depth * N_EXPERTS, 1, D_FF),
        exp_up_b=w["exp_up_b"].reshape(depth * N_EXPERTS, 1, D_FF),
        exp_down_b=w["exp_down_b"].reshape(depth * N_EXPERTS, 1, D_MODEL))
    layers = []
    for l in range(depth):
        layers.append(dict(
            stacked,
            norm1_g=w["norm1_g"][l].reshape(1, D_MODEL),
            norm2_g=w["norm2_g"][l].reshape(1, D_MODEL),
            w_in=w["w_in"][l].astype(BF16),
            fourier_w=w["fourier_w"][l].astype(BF16),
            conv31_w=jnp.pad(w["conv31_w"][l], ((0, 1), (0, 0))),
            conv31_b=w["conv31_b"][l].reshape(1, GROUP),
            gn_g=w["gn_g"][l].reshape(1, GROUP),
            gn_b=w["gn_b"][l].reshape(1, GROUP),
            avg=avg,
            pw_w=w["pw_w"][l].astype(BF16),
            pool_w=_block_diag(w["pool_w"][l]).astype(BF16),
            pool_scale=w["pool_scale"][l].reshape(1, GROUP),
            conv3_w=jnp.pad(w["conv3_w"][l], ((0, 5), (0, 0))),
            sconv_out_w=w["sconv_out_w"][l].astype(BF16),
            w_o=w["w_o"][l].astype(BF16),
            router_w=jnp.pad(w["router_w"][l], ((0, 0), (0, LANES - N_EXPERTS))).astype(BF16),
            router_b=jnp.pad(w["router_b"][l], (0, LANES - N_EXPERTS)).reshape(1, LANES),
        ))
    return chan_dft, layers


def _tiles(bsz, s):
    t_seq = min(512, s)
    t_dft_m = min(s, max(128, (8 * 1024 * 1024) // (bsz * GROUP * 4)))
    t_dft_k = min(1024, s)
    return dict(seq=t_seq, dft_m=t_dft_m, dft_k=t_dft_k, tok=min(256, bsz * s))


def _moe(h2, logits, x1, mod, final_g, wl, layer, seq_len, tiles, final):
    t_tok = h2.shape[0]
    tok = tiles["tok"]
    n_assign = t_tok * TOP_K
    n_blocks = -(-(n_assign + N_EXPERTS * DISP_WIN) // MOE_BLOCK) + N_EXPERTS
    n_slots = n_blocks * MOE_BLOCK
    e_idx, rank, gates, base, tile_cnt, counts = _route(logits, tok)
    counts = counts[0, :N_EXPERTS].astype(jnp.int32)
    padded = jnp.where(counts > 0, (counts + DISP_WIN + MOE_BLOCK - 1) // MOE_BLOCK * MOE_BLOCK, 0)
    pend = jnp.cumsum(padded)
    pstart = pend - padded
    npad = padded - counts
    block_start = jnp.arange(n_blocks, dtype=jnp.int32) * MOE_BLOCK
    block_e = jnp.minimum(jnp.sum(block_start[:, None] >= pend[None, :], axis=1), N_EXPERTS - 1).astype(jnp.int32)
    n_used = (pend[-1:] // MOE_BLOCK).astype(jnp.int32)
    seg_lo = pstart[None, :] + base[:, 0, :N_EXPERTS]
    seg_n = tile_cnt[:, 0, :N_EXPERTS]
    rounds = jnp.maximum(-(-jnp.max(seg_n, axis=1) // SEG_CAP), 1).astype(jnp.int32)
    pstart_v = jnp.pad(pstart, (0, LANES - N_EXPERTS)).reshape(1, LANES).astype(jnp.int32)
    flat = lambda a: a.reshape(-1).astype(jnp.int32)
    head = seg_lo & (F32_TILE - 1)
    dwin_lo = seg_lo - head
    dwin_rows = jnp.where(seg_n > 0, -(-(seg_lo + seg_n - dwin_lo) // F32_TILE) * F32_TILE, 0)
    dwin_src = jnp.cumsum(dwin_rows, axis=1) - dwin_rows
    dcol_off = jnp.pad(dwin_src - dwin_lo + pstart[None, :], ((0, 0), (0, LANES - N_EXPERTS)))
    dcol_off = dcol_off.reshape(-1, 1, LANES).astype(jnp.int32)
    xs = _dispatch(flat(dwin_src), flat(dwin_lo), flat(dwin_rows // F32_TILE), flat(jnp.where(seg_n > 0, head, 0)),
                   pend.astype(jnp.int32), npad.astype(jnp.int32), e_idx, rank, dcol_off, h2, n_slots, tok)
    ys = _experts(block_e, n_used, xs, wl, layer)
    return _combine(flat(seg_lo), rounds, e_idx, rank, gates, base, pstart_v, x1, mod, final_g, ys,
                    seq_len, tok, final)


def _encoder(x, c, w, chan_dft, layers, tables, tiles=None):
    bsz, s, _ = x.shape
    tiles = tiles or _tiles(bsz, s)
    cos_t, nsin_t = tables
    depth = len(layers)
    final_g = w["final_g"].reshape(1, D_MODEL)
    ada_b = w["ada_b"].reshape(depth, 1, 6 * D_MODEL)
    for l, wl in enumerate(layers):
        mod = _ada(c, w["ada_w"], ada_b, l).reshape(bsz, 6, D_MODEL)
        mod = jnp.pad(mod, ((0, 0), (0, 2), (0, 0)))
        p, u = _inproj(x, mod, wl["norm1_g"], wl["w_in"], chan_dft, tiles["seq"])
        y = _seqdft(cos_t, nsin_t, p, tiles["dft_m"], tiles["dft_k"])
        x1, h2, logits = _mix(y, u, x, mod, wl, tiles["seq"])
        x = _moe(h2.reshape(bsz * s, D_MODEL), logits.reshape(bsz * s, LANES),
                 x1.reshape(bsz * s, D_MODEL), mod, final_g, wl, l, s, tiles,
                 final=(l == depth - 1)).reshape(bsz, s, D_MODEL)
    return x


def _seq_tables(s):
    k = jnp.arange(s, dtype=jnp.int32)[None, :]

    def cos_sin(j):
        ang = ((j[:, None] * k) % s).astype(F32) * (2.0 * math.pi / s)
        return jnp.cos(ang), jnp.sin(ang)

    ca, sa = cos_sin(jnp.arange(s // LANES, dtype=jnp.int32) * LANES)
    cb, sb = cos_sin(jnp.arange(LANES, dtype=jnp.int32))
    scale = 1.0 / math.sqrt(s)
    cos_t = (ca[:, None, :] * cb[None, :, :] - sa[:, None, :] * sb[None, :, :]) * scale
    nsin_t = (sa[:, None, :] * cb[None, :, :] + ca[:, None, :] * sb[None, :, :]) * (-scale)
    return cos_t.reshape(s, s).astype(BF16), nsin_t.reshape(s, s).astype(BF16)


def kernel(x_prompt, x_sample, c_prompt, c_sample, norm1_g, norm2_g, ada_w, ada_b, w_in, fourier_w, conv31_w, conv31_b, gn_g, gn_b, pw_w, pool_w, pool_scale, conv3_w, sconv_out_w, w_o, router_w, router_b, exp_gate_w, exp_gate_b, exp_up_w, exp_up_b, exp_down_w, exp_down_b, final_g):
    w = dict(norm1_g=norm1_g, norm2_g=norm2_g, ada_w=ada_w, ada_b=ada_b, w_in=w_in, fourier_w=fourier_w,
             conv31_w=conv31_w, conv31_b=conv31_b, gn_g=gn_g, gn_b=gn_b, pw_w=pw_w, pool_w=pool_w,
             pool_scale=pool_scale, conv3_w=conv3_w, sconv_out_w=sconv_out_w, w_o=w_o, router_w=router_w,
             router_b=router_b, exp_gate_w=exp_gate_w, exp_gate_b=exp_gate_b, exp_up_w=exp_up_w,
             exp_up_b=exp_up_b, exp_down_w=exp_down_w, exp_down_b=exp_down_b, final_g=final_g)
    chan_dft, layers = _prep_weights(w)
    y_prompt = _encoder(x_prompt, c_prompt, w, chan_dft, layers, _seq_tables(x_prompt.shape[1]))
    y_sample = _encoder(x_sample, c_sample, w, chan_dft, layers, _seq_tables(x_sample.shape[1]))
    return (y_prompt, y_sample)
```

```python
import functools
import math

import jax
import jax.numpy as jnp
from jax import lax
from jax.experimental import pallas as pl
from jax.experimental.pallas import tpu as pltpu

D_MODEL = 1024
GROUP = 256
HEAD_DIM = 64
CONV31 = 31
POOL_WINDOWS = (2, 4, 8, 16)
N_EXPERTS = 32
TOP_K = 4
D_FF = 1024
SWIGLU_ALPHA = 1.702
SWIGLU_LIMIT = 7.0
MOE_BLOCK = 512
RMS_EPS = 1e-6
GN_EPS = 1e-5
HALO = 16
LANES = 128
VMEM_LIMIT = 48 * 1024 * 1024

F32 = jnp.float32
BF16 = jnp.bfloat16


def _cparams(sem):
    return pltpu.CompilerParams(dimension_semantics=sem, vmem_limit_bytes=VMEM_LIMIT)


def _dot(a, b):
    return jnp.dot(a, b, preferred_element_type=F32)


def _split_bf16(x):
    hi = x.astype(BF16)
    lo = (x - hi.astype(F32)).astype(BF16)
    return hi, lo


def _dot_hilo(x, w_bf16):
    hi, lo = _split_bf16(x)
    return _dot(hi, w_bf16) + _dot(lo, w_bf16)


def _sigmoid(x):
    return 1.0 / (1.0 + jnp.exp(-x))


def _ada_kernel(c_ref, w_ref, b_ref, o_ref):
    c = c_ref[...]
    cs = c * _sigmoid(c)
    c_hi, c_lo = _split_bf16(cs)
    w_hi, w_lo = _split_bf16(w_ref[0])
    o_ref[...] = _dot(c_hi, w_hi) + _dot(c_lo, w_hi) + _dot(c_hi, w_lo) + b_ref[0]


def _ada(c, ada_w, ada_b, layer):
    bsz = c.shape[0]
    n_out = ada_w.shape[2]
    return pl.pallas_call(
        _ada_kernel,
        out_shape=jax.ShapeDtypeStruct((bsz, n_out), F32),
        grid=(n_out // D_MODEL,),
        in_specs=[
            pl.BlockSpec((bsz, D_MODEL), lambda j: (0, 0)),
            pl.BlockSpec((1, D_MODEL, D_MODEL), lambda j: (layer, 0, j)),
            pl.BlockSpec((1, 1, D_MODEL), lambda j: (layer, 0, j)),
        ],
        out_specs=pl.BlockSpec((bsz, D_MODEL), lambda j: (0, j)),
        compiler_params=_cparams(("arbitrary",)),
        name="ada",
    )(c, ada_w, ada_b)


def _rms_mod(x, g, scale, shift):
    ms = jnp.mean(x * x, axis=-1, keepdims=True)
    y = x * lax.rsqrt(ms + RMS_EPS) * g
    return y * (1.0 + scale) + shift


def _inproj_kernel(x_ref, mod_ref, g_ref, w_ref, cs_ref, p_ref, u_ref):
    x = x_ref[0]
    h = _rms_mod(x, g_ref[...], mod_ref[0, 1:2, :], mod_ref[0, 0:1, :]).astype(BF16)
    a = _dot(h, w_ref[:, 0:GROUP]).astype(BF16)
    p_ref[0] = _dot(a, cs_ref[...]).astype(BF16)
    for j in range(6):
        u_ref[0, :, j * GROUP:(j + 1) * GROUP] = _dot(
            h, w_ref[:, (j + 1) * GROUP:(j + 2) * GROUP]).astype(BF16)


def _inproj(x, mod, g, w_in, chan_dft, tm):
    bsz, s, _ = x.shape
    return pl.pallas_call(
        _inproj_kernel,
        out_shape=(jax.ShapeDtypeStruct((bsz, s, 2 * GROUP), BF16),
                   jax.ShapeDtypeStruct((bsz, s, 6 * GROUP), BF16)),
        grid=(bsz, s // tm),
        in_specs=[
            pl.BlockSpec((1, tm, D_MODEL), lambda b, i: (b, i, 0)),
            pl.BlockSpec((1, 8, D_MODEL), lambda b, i: (b, 0, 0)),
            pl.BlockSpec((1, D_MODEL), lambda b, i: (0, 0)),
            pl.BlockSpec((D_MODEL, 7 * GROUP), lambda b, i: (0, 0)),
            pl.BlockSpec((GROUP, 2 * GROUP), lambda b, i: (0, 0)),
        ],
        out_specs=(pl.BlockSpec((1, tm, 2 * GROUP), lambda b, i: (b, i, 0)),
                   pl.BlockSpec((1, tm, 6 * GROUP), lambda b, i: (b, i, 0))),
        compiler_params=_cparams(("arbitrary", "arbitrary")),
        name="inproj",
    )(x, mod, g, w_in, chan_dft)


def _seqdft_kernel(c_ref, s_ref, p_ref, o_ref, acc_ref):
    k = pl.program_id(1)
    b = pl.program_id(2)
    contrib = _dot(c_ref[...], p_ref[0, :, 0:GROUP]) + _dot(s_ref[...], p_ref[0, :, GROUP:2 * GROUP])

    @pl.when(k == 0)
    def _():
        acc_ref[b] = contrib

    @pl.when(k > 0)
    def _():
        acc_ref[b] = acc_ref[b] + contrib

    @pl.when(k == pl.num_programs(1) - 1)
    def _():
        o_ref[b] = acc_ref[b].astype(BF16)


def _seqdft(cos_t, nsin_t, p, tm, tk):
    bsz, s, _ = p.shape
    return pl.pallas_call(
        _seqdft_kernel,
        out_shape=jax.ShapeDtypeStruct((bsz, s, GROUP), BF16),
        grid=(s // tm, s // tk, bsz),
        in_specs=[
            pl.BlockSpec((tm, tk), lambda i, k, b: (i, k)),
            pl.BlockSpec((tm, tk), lambda i, k, b: (i, k)),
            pl.BlockSpec((1, tk, 2 * GROUP), lambda i, k, b: (b, k, 0)),
        ],
        out_specs=pl.BlockSpec((bsz, tm, GROUP), lambda i, k, b: (0, i, 0)),
        scratch_shapes=[pltpu.VMEM((bsz, tm, GROUP), F32)],
        compiler_params=_cparams(("arbitrary", "arbitrary", "arbitrary")),
        name="seqdft",
    )(cos_t, nsin_t, p)


U_V, U_G, U_P, U_BG, U_CG, U_XV = (j * GROUP for j in range(6))


def _mix_kernel(y_ref, u_ref, up_ref, un_ref, x_ref, mod_ref,
                fw_ref, c31w_ref, c31b_ref, gng_ref, gnb_ref, avg_ref, pww_ref,
                poolw_ref, pscale_ref, c3w_ref, sow_ref, wo_ref, n2g_ref, rw_ref, rb_ref,
                x1_ref, h2_ref, lg_ref,
                zext, zsh, pext, s2, s4, s8, s16, qext, cat, *, tq, seq_len):
    i = pl.program_id(1)
    has_prev = (i > 0).astype(F32)
    has_next = (i < pl.num_programs(1) - 1).astype(F32)
    rows = tq + 2 * HALO

    def cols(ref, c0):
        return ref[0, :, c0:c0 + GROUP].astype(F32)

    def fill_ext(dst, fn):
        dst[0:HALO, :] = fn(up_ref) * has_prev
        dst[HALO:HALO + tq, :] = fn(u_ref)
        dst[HALO + tq:rows, :] = fn(un_ref) * has_next

    cat[:, 0:GROUP] = _dot(y_ref[0], fw_ref[...]).astype(BF16)

    fill_ext(zext, lambda r: cols(r, U_V) * _sigmoid(cols(r, U_G)))
    conv = jnp.broadcast_to(c31b_ref[...], (tq, GROUP))
    first = HALO - CONV31 // 2
    for res in range(8):
        taps = [j for j in range(CONV31) if (first + j) % 8 == res]
        span = (first + taps[-1]) // 8 * 8 + tq
        zsh[0:span, :] = zext[pl.ds(res, span), :]
        for j in taps:
            q = (first + j) // 8 * 8
            conv = conv + c31w_ref[j:j + 1, :] * zsh[q:q + tq, :]
    mu = _dot_hilo(conv, avg_ref[...])
    dev = conv - mu
    var = _dot_hilo(dev * dev, avg_ref[...])
    zn = dev * lax.rsqrt(var + GN_EPS) * gng_ref[...] + gnb_ref[...]
    cat[:, GROUP:2 * GROUP] = _dot((zn * _sigmoid(zn)).astype(BF16), pww_ref[...]).astype(BF16)

    fill_ext(pext, lambda r: cols(r, U_P))
    n = rows - 8
    zeros8 = jnp.zeros((8, GROUP), F32)
    s2[0:n, :] = pext[0:n, :] + pext[pl.ds(1, n), :]
    s2[n:rows, :] = zeros8
    s4[0:n, :] = s2[0:n, :] + s2[pl.ds(2, n), :]
    s4[n:rows, :] = zeros8
    s8[0:n, :] = s4[0:n, :] + s4[pl.ds(4, n), :]
    s8[n:rows, :] = zeros8
    s16[0:n, :] = s8[0:n, :] + s8[pl.ds(8, n), :]
    lane = lax.broadcasted_iota(jnp.int32, (tq, GROUP), 1)
    pos = lax.broadcasted_iota(jnp.int32, (tq, GROUP), 0) + i * tq
    win = jnp.where(lane < HEAD_DIM, s2[pl.ds(HALO - 1, tq), :],
                    jnp.where(lane < 2 * HEAD_DIM, s4[pl.ds(HALO - 2, tq), :],
                              jnp.where(lane < 3 * HEAD_DIM, s8[pl.ds(HALO - 4, tq), :],
                                        s16[pl.ds(HALO - 8, tq), :])))
    half = jnp.where(lane < HEAD_DIM, 1,
                     jnp.where(lane < 2 * HEAD_DIM, 2, jnp.where(lane < 3 * HEAD_DIM, 4, 8)))
    cnt = jnp.minimum(pos + half, seq_len) - jnp.maximum(pos - half, 0)
    dpool = win / cnt.astype(F32) - pext[HALO:HALO + tq, :]
    cat[:, 2 * GROUP:3 * GROUP] = (_dot(dpool.astype(BF16), poolw_ref[...]) * pscale_ref[...]).astype(BF16)

    fill_ext(qext, lambda r: cols(r, U_CG) * cols(r, U_XV))
    c3 = (c3w_ref[0:1, :] * qext[pl.ds(HALO - 1, tq), :]
          + c3w_ref[1:2, :] * qext[HALO:HALO + tq, :]
          + c3w_ref[2:3, :] * qext[pl.ds(HALO + 1, tq), :])
    cat[:, 3 * GROUP:4 * GROUP] = _dot((cols(u_ref, U_BG) * c3).astype(BF16), sow_ref[...]).astype(BF16)

    mixed = _dot(cat[...], wo_ref[...])
    x1 = x_ref[0] + mod_ref[0, 2:3, :] * mixed
    x1_ref[0] = x1
    h2 = _rms_mod(x1, n2g_ref[...], mod_ref[0, 4:5, :], mod_ref[0, 3:4, :])
    h2_ref[0] = h2
    lg_ref[0] = _dot(h2.astype(BF16), rw_ref[...]) + rb_ref[...]


def _mix(y, u, x, mod, wl, tq):
    bsz, s, _ = x.shape
    rows = tq + 2 * HALO
    hb = tq // HALO
    n_hblk = s // HALO
    full = lambda shape: pl.BlockSpec(shape, lambda b, i: tuple(0 for _ in shape))
    kern = functools.partial(_mix_kernel, tq=tq, seq_len=s)
    return pl.pallas_call(
        kern,
        out_shape=(jax.ShapeDtypeStruct((bsz, s, D_MODEL), F32),
                   jax.ShapeDtypeStruct((bsz, s, D_MODEL), F32),
                   jax.ShapeDtypeStruct((bsz, s, LANES), F32)),
        grid=(bsz, s // tq),
        in_specs=[
            pl.BlockSpec((1, tq, GROUP), lambda b, i: (b, i, 0)),
            pl.BlockSpec((1, tq, 6 * GROUP), lambda b, i: (b, i, 0)),
            pl.BlockSpec((1, HALO, 6 * GROUP), lambda b, i: (b, jnp.maximum(i * hb - 1, 0), 0)),
            pl.BlockSpec((1, HALO, 6 * GROUP), lambda b, i: (b, jnp.minimum((i + 1) * hb, n_hblk - 1), 0)),
            pl.BlockSpec((1, tq, D_MODEL), lambda b, i: (b, i, 0)),
            pl.BlockSpec((1, 8, D_MODEL), lambda b, i: (b, 0, 0)),
            full((GROUP, GROUP)),
            full((32, GROUP)),
            full((1, GROUP)),
            full((1, GROUP)),
            full((1, GROUP)),
            full((GROUP, GROUP)),
            full((GROUP, GROUP)),
            full((GROUP, GROUP)),
            full((1, GROUP)),
            full((8, GROUP)),
            full((GROUP, GROUP)),
            full((D_MODEL, D_MODEL)),
            full((1, D_MODEL)),
            full((D_MODEL, LANES)),
            full((1, LANES)),
        ],
        out_specs=(pl.BlockSpec((1, tq, D_MODEL), lambda b, i: (b, i, 0)),
                   pl.BlockSpec((1, tq, D_MODEL), lambda b, i: (b, i, 0)),
                   pl.BlockSpec((1, tq, LANES), lambda b, i: (b, i, 0))),
        scratch_shapes=[pltpu.VMEM((rows, GROUP), F32) for _ in range(8)]
        + [pltpu.VMEM((tq, D_MODEL), BF16)],
        compiler_params=_cparams(("arbitrary", "arbitrary")),
        name="mix",
    )(y, u, u, u, x, mod, wl["fourier_w"], wl["conv31_w"], wl["conv31_b"], wl["gn_g"], wl["gn_b"],
      wl["avg"], wl["pw_w"], wl["pool_w"], wl["pool_scale"], wl["conv3_w"], wl["sconv_out_w"],
      wl["w_o"], wl["norm2_g"], wl["router_w"], wl["router_b"])


def _route_kernel(lg_ref, e_ref, rank_ref, gate_ref, base_ref, tcnt_ref, cnt_ref, carry, *, tr):
    i = pl.program_id(0)

    @pl.when(i == 0)
    def _():
        carry[...] = jnp.zeros_like(carry)

    lane = lax.broadcasted_iota(jnp.int32, (tr, LANES), 1).astype(F32)
    neg = jnp.float32(-jnp.inf)
    work = jnp.where(lane < N_EXPERTS, lg_ref[...], neg)
    vals, idxs, hots = [], [], []
    for _ in range(TOP_K):
        m = jnp.max(work, axis=1, keepdims=True)
        idx = jnp.min(jnp.where(work == m, lane, float(LANES)), axis=1, keepdims=True)
        hot = lane == idx
        vals.append(m)
        idxs.append(idx.astype(jnp.int32))
        hots.append(hot)
        work = jnp.where(hot, neg, work)
    exps = [jnp.exp(v - vals[0]) for v in vals]
    denom = exps[0] + exps[1] + exps[2] + exps[3]
    member = sum(h.astype(F32) for h in hots)
    r_io = lax.broadcasted_iota(jnp.int32, (tr, tr), 0)
    c_io = lax.broadcasted_iota(jnp.int32, (tr, tr), 1)
    lower = (c_io < r_io).astype(BF16)
    before = _dot(lower, member.astype(BF16)) + carry[...]
    col = lax.broadcasted_iota(jnp.int32, (tr, TOP_K), 1)
    e_out = jnp.zeros((tr, TOP_K), jnp.int32)
    r_out = jnp.zeros((tr, TOP_K), jnp.int32)
    g_out = jnp.zeros((tr, TOP_K), F32)
    for k in range(TOP_K):
        rk = jnp.sum(jnp.where(hots[k], before, 0.0), axis=1, keepdims=True).astype(jnp.int32)
        e_out = jnp.where(col == k, idxs[k], e_out)
        r_out = jnp.where(col == k, rk, r_out)
        g_out = jnp.where(col == k, exps[k] / denom, g_out)
    e_ref[...] = e_out
    rank_ref[...] = r_out
    gate_ref[...] = g_out
    tile_cnt = jnp.sum(member, axis=0, keepdims=True)
    base_ref[0] = carry[...].astype(jnp.int32)
    tcnt_ref[0] = tile_cnt.astype(jnp.int32)
    carry[...] = carry[...] + tile_cnt
    cnt_ref[...] = carry[...]


def _route(logits, tr):
    t_tok = logits.shape[0]
    n_tiles = t_tok // tr
    kern = functools.partial(_route_kernel, tr=tr)
    tile = lambda w: pl.BlockSpec((tr, w), lambda i: (i, 0))
    per_tile = pl.BlockSpec((1, 1, LANES), lambda i: (i, 0, 0))
    return pl.pallas_call(
        kern,
        out_shape=(jax.ShapeDtypeStruct((t_tok, TOP_K), jnp.int32),
                   jax.ShapeDtypeStruct((t_tok, TOP_K), jnp.int32),
                   jax.ShapeDtypeStruct((t_tok, TOP_K), F32),
                   jax.ShapeDtypeStruct((n_tiles, 1, LANES), jnp.int32),
                   jax.ShapeDtypeStruct((n_tiles, 1, LANES), jnp.int32),
                   jax.ShapeDtypeStruct((1, LANES), F32)),
        grid=(n_tiles,),
        in_specs=[tile(LANES)],
        out_specs=(tile(TOP_K), tile(TOP_K), tile(TOP_K), per_tile, per_tile,
                   pl.BlockSpec((1, LANES), lambda i: (0, 0))),
        scratch_shapes=[pltpu.VMEM((1, LANES), F32)],
        compiler_params=_cparams(("arbitrary",)),
        name="route",
    )(logits)


F32_TILE = 8
DISP_ROWS = TOP_K * 256 + N_EXPERTS * 2 * F32_TILE
CARRY_ROWS = N_EXPERTS * F32_TILE
DISP_WIN = 64


def _dispatch_kernel(src_ref, dst_ref, nch_ref, head_ref, pend_ref, npad_ref, e_ref, rank_ref, off_ref, h_ref,
                     xs_hbm, outbuf, carry, zeros, sems, sem_z, *, td):
    i = pl.program_id(0)
    n_steps = pl.num_programs(0)
    slot = i % 2

    def zero_blk(row):
        return pltpu.make_async_copy(zeros, xs_hbm.at[pl.ds(pl.multiple_of(row, MOE_BLOCK), MOE_BLOCK)], sem_z)

    @pl.when(i == 0)
    def _():
        zeros[...] = jnp.zeros_like(zeros)
        carry[...] = jnp.zeros_like(carry)
        outbuf[...] = jnp.zeros_like(outbuf)
        first_unused = pend_ref[N_EXPERTS - 1] // MOE_BLOCK
        n_blocks = xs_hbm.shape[0] // MOE_BLOCK

        def start(e, c):
            @pl.when(npad_ref[e] > 0)
            def _():
                zero_blk(pend_ref[e] - MOE_BLOCK).start()

            @pl.when(npad_ref[e] > MOE_BLOCK)
            def _():
                zero_blk(pend_ref[e] - 2 * MOE_BLOCK).start()
            return c

        def wait(e, c):
            @pl.when(npad_ref[e] > 0)
            def _():
                zero_blk(pend_ref[e] - MOE_BLOCK).wait()

            @pl.when(npad_ref[e] > MOE_BLOCK)
            def _():
                zero_blk(pend_ref[e] - 2 * MOE_BLOCK).wait()
            return c

        def start_tail(b, c):
            zero_blk(b * MOE_BLOCK).start()
            return c

        def wait_tail(b, c):
            zero_blk(b * MOE_BLOCK).wait()
            return c

        lax.fori_loop(0, N_EXPERTS, start, 0)
        lax.fori_loop(first_unused, n_blocks, start_tail, 0)
        lax.fori_loop(0, N_EXPERTS, wait, 0)
        lax.fori_loop(first_unused, n_blocks, wait_tail, 0)

    def window_copies(tile, sl, act):
        for e in range(N_EXPERTS):
            n_ch = nch_ref[tile * N_EXPERTS + e]
            src0 = pl.multiple_of(src_ref[tile * N_EXPERTS + e], F32_TILE)
            dst0 = pl.multiple_of(dst_ref[tile * N_EXPERTS + e], F32_TILE)

            @pl.when(n_ch > 0)
            def _():
                act(pltpu.make_async_copy(outbuf.at[sl, pl.ds(src0, DISP_WIN)],
                                          xs_hbm.at[pl.ds(dst0, DISP_WIN)], sems.at[sl]))

            def per_chunk(ci, c2):
                src = pl.multiple_of(src0 + ci * F32_TILE, F32_TILE)
                dst = pl.multiple_of(dst0 + ci * F32_TILE, F32_TILE)
                act(pltpu.make_async_copy(outbuf.at[sl, pl.ds(src, F32_TILE)],
                                          xs_hbm.at[pl.ds(dst, F32_TILE)], sems.at[sl]))
                return c2

            lax.fori_loop(DISP_WIN // F32_TILE, n_ch, per_chunk, 0)

    lane_td = lax.broadcasted_iota(jnp.int32, (td, LANES), 1)
    off_f = off_ref[0].astype(F32)
    pos_cols = jnp.full((td, LANES), -1.0, F32)
    for k in range(TOP_K):
        off_k = jnp.sum(jnp.where(lane_td == e_ref[:, k:k + 1], off_f, 0.0), axis=1, keepdims=True)
        pos_cols = jnp.where(lane_td == k, rank_ref[:, k:k + 1].astype(F32) + off_k, pos_cols)
    pos_t = jnp.transpose(pos_cols).astype(jnp.int32)
    pos = [jnp.broadcast_to(pos_t[k:k + 1, :], (GROUP, td)) for k in range(TOP_K)]
    row_io = lax.broadcasted_iota(jnp.int32, (GROUP, td), 0)
    h_bf = h_ref[...].astype(BF16)
    for rc in range(DISP_ROWS // GROUP):
        r = row_io + rc * GROUP
        tok = jnp.zeros((GROUP, td), F32)
        for k in range(TOP_K):
            tok = jnp.where(r == pos[k], 1.0, tok)
        outbuf[slot, rc * GROUP:(rc + 1) * GROUP, :] = _dot(tok.astype(BF16), h_bf)

    sub_io = lax.broadcasted_iota(jnp.int32, (F32_TILE, D_MODEL), 0)
    for e in range(N_EXPERTS):
        es = slice(e * F32_TILE, (e + 1) * F32_TILE)
        n_ch = nch_ref[i * N_EXPERTS + e]
        first = pl.multiple_of(src_ref[i * N_EXPERTS + e], F32_TILE)
        outbuf[slot, pl.ds(first, F32_TILE), :] = jnp.where(
            sub_io < head_ref[i * N_EXPERTS + e], carry[es, :], outbuf[slot, pl.ds(first, F32_TILE), :])
        last = pl.multiple_of(first + jnp.maximum(n_ch - 1, 0) * F32_TILE, F32_TILE)
        carry[es, :] = jnp.where(n_ch > 0, outbuf[slot, pl.ds(last, F32_TILE), :], carry[es, :])

    @pl.when(i > 0)
    def _():
        window_copies(i - 1, 1 - slot, lambda cp: cp.wait())

    window_copies(i, slot, lambda cp: cp.start())

    @pl.when(i == n_steps - 1)
    def _():
        window_copies(i, slot, lambda cp: cp.wait())


def _dispatch(chunk_src, chunk_dst, n_chunks, head, pend, npad, e_idx, rank, col_off, h2, n_slots, td):
    t_tok = h2.shape[0]
    assert DISP_ROWS >= TOP_K * td + N_EXPERTS * 2 * (F32_TILE - 1) + F32_TILE and DISP_ROWS % GROUP == 0
    kern = functools.partial(_dispatch_kernel, td=td)
    tok4 = lambda: pl.BlockSpec((td, TOP_K), lambda i, *_: (i, 0))
    return pl.pallas_call(
        kern,
        out_shape=jax.ShapeDtypeStruct((n_slots, D_MODEL), F32),
        grid_spec=pltpu.PrefetchScalarGridSpec(
            num_scalar_prefetch=6,
            grid=(t_tok // td,),
            in_specs=[tok4(), tok4(),
                      pl.BlockSpec((1, 1, LANES), lambda i, *_: (i, 0, 0)),
                      pl.BlockSpec((td, D_MODEL), lambda i, *_: (i, 0))],
            out_specs=pl.BlockSpec(memory_space=pl.ANY),
            scratch_shapes=[pltpu.VMEM((2, DISP_ROWS + DISP_WIN, D_MODEL), F32),
                            pltpu.VMEM((CARRY_ROWS, D_MODEL), F32),
                            pltpu.VMEM((MOE_BLOCK, D_MODEL), F32),
                            pltpu.SemaphoreType.DMA((2,)),
                            pltpu.SemaphoreType.DMA]),
        compiler_params=_cparams(("arbitrary",)),
        name="dispatch",
    )(chunk_src, chunk_dst, n_chunks, head, pend, npad, e_idx, rank, col_off, h2)


def _experts_kernel(be_ref, nu_ref, xs_ref, wg_ref, bg_ref, wu_ref, bu_ref, wd_ref, bd_ref, o_ref, act):
    i = pl.program_id(0)

    @pl.when(i < nu_ref[0])
    def _():
        x = xs_ref[...].astype(BF16)
        for f in range(D_FF // GROUP):
            fs = slice(f * GROUP, (f + 1) * GROUP)
            gt = jnp.minimum(_dot(x, wg_ref[0, :, fs]) + bg_ref[0, :, fs], SWIGLU_LIMIT)
            up = jnp.clip(_dot(x, wu_ref[0, :, fs]) + bu_ref[0, :, fs], -SWIGLU_LIMIT, SWIGLU_LIMIT)
            act[:, fs] = ((up + 1.0) * (gt * _sigmoid(SWIGLU_ALPHA * gt))).astype(BF16)
        o_ref[...] = (_dot(act[...], wd_ref[0]) + bd_ref[0]).astype(BF16)

    @pl.when(i >= nu_ref[0])
    def _():
        o_ref[...] = jnp.zeros_like(o_ref)


def _experts(block_e, n_used, xs, wl, layer):
    n_slots = xs.shape[0]
    n_blocks = n_slots // MOE_BLOCK

    def expert(i, be, nu):
        return layer * N_EXPERTS + be[jnp.minimum(i, nu[0] - 1)]

    wspec = lambda: pl.BlockSpec((1, D_MODEL, D_FF), lambda i, be, nu: (expert(i, be, nu), 0, 0))
    bspec = lambda: pl.BlockSpec((1, 1, D_FF), lambda i, be, nu: (expert(i, be, nu), 0, 0))
    return pl.pallas_call(
        _experts_kernel,
        out_shape=jax.ShapeDtypeStruct((n_slots, D_MODEL), BF16),
        grid_spec=pltpu.PrefetchScalarGridSpec(
            num_scalar_prefetch=2,
            grid=(n_blocks,),
            in_specs=[pl.BlockSpec((MOE_BLOCK, D_MODEL), lambda i, be, nu: (jnp.minimum(i, nu[0] - 1), 0)),
                      wspec(), bspec(), wspec(), bspec(), wspec(), bspec()],
            out_specs=pl.BlockSpec((MOE_BLOCK, D_MODEL), lambda i, be, nu: (i, 0)),
            scratch_shapes=[pltpu.VMEM((MOE_BLOCK, D_FF), BF16)]),
        compiler_params=_cparams(("arbitrary",)),
        name="experts",
    )(block_e, n_used, xs, wl["exp_gate_w"], wl["exp_gate_b"], wl["exp_up_w"], wl["exp_up_b"],
      wl["exp_down_w"], wl["exp_down_b"])


ROW_TILE = 16
SEG_ROWS = TOP_K * 256 + N_EXPERTS * 2 * ROW_TILE


def _combine_kernel(src_ref, dst_ref, nch_ref, e_ref, rank_ref, gate_ref, off_ref, x1_ref, mod_ref,
                    fg_ref, ys_hbm, o_ref, segbuf, gmat, sems, *, tc, final):
    j = pl.program_id(0)
    n_tiles = pl.num_programs(0)
    slot = j % 2

    def chunks(tile, sl, act):
        def per_expert(e, c):
            src0 = src_ref[tile * N_EXPERTS + e]
            dst0 = dst_ref[tile * N_EXPERTS + e]

            def per_chunk(ci, c2):
                src = pl.multiple_of(src0 + ci * ROW_TILE, ROW_TILE)
                dst = pl.multiple_of(dst0 + ci * ROW_TILE, ROW_TILE)
                act(pltpu.make_async_copy(ys_hbm.at[pl.ds(src, ROW_TILE)],
                                          segbuf.at[sl, pl.ds(dst, ROW_TILE)], sems.at[sl]))
                return c2

            lax.fori_loop(0, nch_ref[tile * N_EXPERTS + e], per_chunk, 0)
            return c

        lax.fori_loop(0, N_EXPERTS, per_expert, 0)

    @pl.when(j == 0)
    def _():
        segbuf[...] = jnp.zeros_like(segbuf)
        chunks(0, 0, lambda cp: cp.start())

    @pl.when(j + 1 < n_tiles)
    def _():
        chunks(j + 1, 1 - slot, lambda cp: cp.start())

    lane_tc = lax.broadcasted_iota(jnp.int32, (tc, LANES), 1)
    off_f = off_ref[0].astype(F32)
    cols = []
    for k in range(TOP_K):
        off_k = jnp.sum(jnp.where(lane_tc == e_ref[:, k:k + 1], off_f, 0.0), axis=1, keepdims=True)
        cols.append(rank_ref[:, k:k + 1] + off_k.astype(jnp.int32))
    colb = [jnp.broadcast_to(c, (tc, LANES)) for c in cols]
    gateb = [jnp.broadcast_to(gate_ref[:, k:k + 1], (tc, LANES)) for k in range(TOP_K)]
    for p in range(SEG_ROWS // LANES):
        col_io = lane_tc + p * LANES
        g = jnp.zeros((tc, LANES), F32)
        for k in reversed(range(TOP_K)):
            g = jnp.where(col_io == colb[k], gateb[k], g)
        gmat[:, p * LANES:(p + 1) * LANES] = g.astype(BF16)

    chunks(j, slot, lambda cp: cp.wait())
    moe = _dot(gmat[...], segbuf[slot])

    x2 = x1_ref[...] + mod_ref[0, 5:6, :] * moe
    if final:
        ms = jnp.mean(x2 * x2, axis=-1, keepdims=True)
        x2 = x2 * lax.rsqrt(ms + RMS_EPS) * fg_ref[...]
    o_ref[...] = x2


def _combine(chunk_src, chunk_dst, n_chunks, e_idx, rank, gates, col_off, x1, mod, final_g, ys, seq_len, tc, final):
    t_tok = x1.shape[0]
    assert SEG_ROWS >= TOP_K * tc + N_EXPERTS * 2 * (ROW_TILE - 1) and SEG_ROWS % GROUP == 0
    kern = functools.partial(_combine_kernel, tc=tc, final=final)
    tok4 = lambda: pl.BlockSpec((tc, TOP_K), lambda i, *_: (i, 0))
    return pl.pallas_call(
        kern,
        out_shape=jax.ShapeDtypeStruct((t_tok, D_MODEL), F32),
        grid_spec=pltpu.PrefetchScalarGridSpec(
            num_scalar_prefetch=3,
            grid=(t_tok // tc,),
            in_specs=[tok4(), tok4(), tok4(),
                      pl.BlockSpec((1, 1, LANES), lambda i, *_: (i, 0, 0)),
                      pl.BlockSpec((tc, D_MODEL), lambda i, *_: (i, 0)),
                      pl.BlockSpec((1, 8, D_MODEL), lambda i, *_: ((i * tc) // seq_len, 0, 0)),
                      pl.BlockSpec((1, D_MODEL), lambda i, *_: (0, 0)),
                      pl.BlockSpec(memory_space=pl.ANY)],
            out_specs=pl.BlockSpec((tc, D_MODEL), lambda i, *_: (i, 0)),
            scratch_shapes=[pltpu.VMEM((2, SEG_ROWS, D_MODEL), BF16),
                            pltpu.VMEM((tc, SEG_ROWS), BF16),
                            pltpu.SemaphoreType.DMA((2,))]),
        compiler_params=_cparams(("arbitrary",)),
        name="combine",
    )(chunk_src, chunk_dst, n_chunks, e_idx, rank, gates, col_off, x1, mod, final_g, ys)


def _dft_tables(n):
    j = jnp.arange(n, dtype=jnp.int32)
    ang = ((j[:, None] * j[None, :]) % n).astype(F32) * (2.0 * math.pi / n)
    scale = 1.0 / math.sqrt(n)
    return jnp.cos(ang) * scale, jnp.sin(ang) * scale


def _block_diag(blocks):
    n, r, c = blocks.shape
    rows = lax.broadcasted_iota(jnp.int32, (n * r, n * c), 0) // r
    cols = lax.broadcasted_iota(jnp.int32, (n * r, n * c), 1) // c
    return jnp.where(rows == cols, jnp.tile(blocks.reshape(n * r, c), (1, n)), 0)


def _prep_weights(w):
    depth = w["w_in"].shape[0]
    c64, s64 = _dft_tables(HEAD_DIM)
    n_heads = GROUP // HEAD_DIM
    chan_dft = jnp.concatenate([_block_diag(jnp.stack([c64] * n_heads)),
                                _block_diag(jnp.stack([s64] * n_heads))], axis=1).astype(BF16)
    avg = _block_diag(jnp.full((n_heads, HEAD_DIM, HEAD_DIM), 1.0 / HEAD_DIM, F32)).astype(BF16)
    stacked = dict(
        exp_gate_w=w["exp_gate_w"].astype(BF16).reshape(depth * N_EXPERTS, D_MODEL, D_FF),
        exp_up_w=w["exp_up_w"].astype(BF16).reshape(depth * N_EXPERTS, D_MODEL, D_FF),
        exp_down_w=w["exp_down_w"].astype(BF16).reshape(depth * N_EXPERTS, D_FF, D_MODEL),
        exp_gate_b=w["exp_gate_b"].reshape(depth * N_EXPERTS, 1, D_FF),
        exp_up_b=w["exp_up_b"].reshape(depth * N_EXPERTS, 1, D_FF),
        exp_down_b=w["exp_down_b"].reshape(depth * N_EXPERTS, 1, D_MODEL))
    layers = []
    for l in range(depth):
        layers.append(dict(
            stacked,
            norm1_g=w["norm1_g"][l].reshape(1, D_MODEL),
            norm2_g=w["norm2_g"][l].reshape(1, D_MODEL),
            w_in=w["w_in"][l].astype(BF16),
            fourier_w=w["fourier_w"][l].astype(BF16),
            conv31_w=jnp.pad(w["conv31_w"][l], ((0, 1), (0, 0))),
            conv31_b=w["conv31_b"][l].reshape(1, GROUP),
            gn_g=w["gn_g"][l].reshape(1, GROUP),
            gn_b=w["gn_b"][l].reshape(1, GROUP),
            avg=avg,
            pw_w=w["pw_w"][l].astype(BF16),
            pool_w=_block_diag(w["pool_w"][l]).astype(BF16),
            pool_scale=w["pool_scale"][l].reshape(1, GROUP),
            conv3_w=jnp.pad(w["conv3_w"][l], ((0, 5), (0, 0))),
            sconv_out_w=w["sconv_out_w"][l].astype(BF16),
            w_o=w["w_o"][l].astype(BF16),
            router_w=jnp.pad(w["router_w"][l], ((0, 0), (0, LANES - N_EXPERTS))).astype(BF16),
            router_b=jnp.pad(w["router_b"][l], (0, LANES - N_EXPERTS)).reshape(1, LANES),
        ))
    return chan_dft, layers


def _tiles(bsz, s):
    t_seq = min(512, s)
    t_dft_m = min(s, max(128, (8 * 1024 * 1024) // (bsz * GROUP * 4)))
    t_dft_k = min(2048, s)
    return dict(seq=t_seq, dft_m=t_dft_m, dft_k=t_dft_k, tok=min(256, bsz * s))


def _moe(h2, logits, x1, mod, final_g, wl, layer, seq_len, tiles, final):
    t_tok = h2.shape[0]
    tok = tiles["tok"]
    n_assign = t_tok * TOP_K
    n_blocks = -(-(n_assign + N_EXPERTS * DISP_WIN) // MOE_BLOCK) + N_EXPERTS
    n_slots = n_blocks * MOE_BLOCK
    e_idx, rank, gates, base, tile_cnt, counts = _route(logits, tok)
    counts = counts[0, :N_EXPERTS].astype(jnp.int32)
    padded = jnp.where(counts > 0, (counts + DISP_WIN + MOE_BLOCK - 1) // MOE_BLOCK * MOE_BLOCK, 0)
    pend = jnp.cumsum(padded)
    pstart = pend - padded
    npad = padded - counts
    block_start = jnp.arange(n_blocks, dtype=jnp.int32) * MOE_BLOCK
    block_e = jnp.minimum(jnp.sum(block_start[:, None] >= pend[None, :], axis=1), N_EXPERTS - 1).astype(jnp.int32)
    n_used = (pend[-1:] // MOE_BLOCK).astype(jnp.int32)
    seg_lo = pstart[None, :] + base[:, 0, :N_EXPERTS]
    seg_n = tile_cnt[:, 0, :N_EXPERTS]
    win_lo = seg_lo - (seg_lo & (ROW_TILE - 1))
    win_rows = jnp.where(seg_n > 0, -(-(seg_lo + seg_n - win_lo) // ROW_TILE) * ROW_TILE, 0)
    win_dst = jnp.cumsum(win_rows, axis=1) - win_rows
    col_off = jnp.pad(win_dst - win_lo + pstart[None, :], ((0, 0), (0, LANES - N_EXPERTS)))
    col_off = col_off.reshape(-1, 1, LANES).astype(jnp.int32)
    flat = lambda a: a.reshape(-1).astype(jnp.int32)
    head = seg_lo & (F32_TILE - 1)
    dwin_lo = seg_lo - head
    dwin_rows = jnp.where(seg_n > 0, -(-(seg_lo + seg_n - dwin_lo) // F32_TILE) * F32_TILE, 0)
    dwin_src = jnp.cumsum(dwin_rows, axis=1) - dwin_rows
    dcol_off = jnp.pad(dwin_src - dwin_lo + pstart[None, :], ((0, 0), (0, LANES - N_EXPERTS)))
    dcol_off = dcol_off.reshape(-1, 1, LANES).astype(jnp.int32)
    xs = _dispatch(flat(dwin_src), flat(dwin_lo), flat(dwin_rows // F32_TILE), flat(jnp.where(seg_n > 0, head, 0)),
                   pend.astype(jnp.int32), npad.astype(jnp.int32), e_idx, rank, dcol_off, h2, n_slots, tok)
    ys = _experts(block_e, n_used, xs, wl, layer)
    return _combine(flat(win_lo), flat(win_dst), flat(win_rows // ROW_TILE), e_idx, rank, gates, col_off,
                    x1, mod, final_g, ys, seq_len, tok, final)


def _encoder(x, c, w, chan_dft, layers, tables, tiles=None):
    bsz, s, _ = x.shape
    tiles = tiles or _tiles(bsz, s)
    cos_t, nsin_t = tables
    depth = len(layers)
    final_g = w["final_g"].reshape(1, D_MODEL)
    ada_b = w["ada_b"].reshape(depth, 1, 6 * D_MODEL)
    for l, wl in enumerate(layers):
        mod = _ada(c, w["ada_w"], ada_b, l).reshape(bsz, 6, D_MODEL)
        mod = jnp.pad(mod, ((0, 0), (0, 2), (0, 0)))
        p, u = _inproj(x, mod, wl["norm1_g"], wl["w_in"], chan_dft, tiles["seq"])
        y = _seqdft(cos_t, nsin_t, p, tiles["dft_m"], tiles["dft_k"])
        x1, h2, logits = _mix(y, u, x, mod, wl, tiles["seq"])
        x = _moe(h2.reshape(bsz * s, D_MODEL), logits.reshape(bsz * s, LANES),
                 x1.reshape(bsz * s, D_MODEL), mod, final_g, wl, l, s, tiles,
                 final=(l == depth - 1)).reshape(bsz, s, D_MODEL)
    return x


def _seq_tables(s):
    k = jnp.arange(s, dtype=jnp.int32)[None, :]

    def cos_sin(j):
        ang = ((j[:, None] * k) % s).astype(F32) * (2.0 * math.pi / s)
        return jnp.cos(ang), jnp.sin(ang)

    ca, sa = cos_sin(jnp.arange(s // LANES, dtype=jnp.int32) * LANES)
    cb, sb = cos_sin(jnp.arange(LANES, dtype=jnp.int32))
    scale = 1.0 / math.sqrt(s)
    cos_t = (ca[:, None, :] * cb[None, :, :] - sa[:, None, :] * sb[None, :, :]) * scale
    nsin_t = (sa[:, None, :] * cb[None, :, :] + ca[:, None, :] * sb[None, :, :]) * (-scale)
    return cos_t.reshape(s, s).astype(BF16), nsin_t.reshape(s, s).astype(BF16)


def kernel(x_prompt, x_sample, c_prompt, c_sample, norm1_g, norm2_g, ada_w, ada_b, w_in, fourier_w, conv31_w, conv31_b, gn_g, gn_b, pw_w, pool_w, pool_scale, conv3_w, sconv_out_w, w_o, router_w, router_b, exp_gate_w, exp_gate_b, exp_up_w, exp_up_b, exp_down_w, exp_down_b, final_g):
    w = dict(norm1_g=norm1_g, norm2_g=norm2_g, ada_w=ada_w, ada_b=ada_b, w_in=w_in, fourier_w=fourier_w,
             conv31_w=conv31_w, conv31_b=conv31_b, gn_g=gn_g, gn_b=gn_b, pw_w=pw_w, pool_w=pool_w,
             pool_scale=pool_scale, conv3_w=conv3_w, sconv_out_w=sconv_out_w, w_o=w_o, router_w=router_w,
             router_b=router_b, exp_gate_w=exp_gate_w, exp_gate_b=exp_gate_b, exp_up_w=exp_up_w,
             exp_up_b=exp_up_b, exp_down_w=exp_down_w, exp_down_b=exp_down_b, final_g=final_g)
    chan_dft, layers = _prep_weights(w)
    y_prompt = _encoder(x_prompt, c_prompt, w, chan_dft, layers, _seq_tables(x_prompt.shape[1]))
    y_sample = _encoder(x_sample, c_sample, w, chan_dft, layers, _seq_tables(x_sample.shape[1]))
    return (y_prompt, y_sample)
```

```python
import functools
import math

import jax
import jax.numpy as jnp
from jax import lax
from jax.experimental import pallas as pl
from jax.experimental.pallas import tpu as pltpu

D_MODEL = 1024
GROUP = 256
HEAD_DIM = 64
CONV31 = 31
POOL_WINDOWS = (2, 4, 8, 16)
N_EXPERTS = 32
TOP_K = 4
D_FF = 1024
SWIGLU_ALPHA = 1.702
SWIGLU_LIMIT = 7.0
MOE_BLOCK = 512
RMS_EPS = 1e-6
GN_EPS = 1e-5
HALO = 16
LANES = 128
VMEM_LIMIT = 48 * 1024 * 1024

F32 = jnp.float32
BF16 = jnp.bfloat16


def _cparams(sem):
    return pltpu.CompilerParams(dimension_semantics=sem, vmem_limit_bytes=VMEM_LIMIT)


def _dot(a, b):
    return jnp.dot(a, b, preferred_element_type=F32)


def _split_bf16(x):
    hi = x.astype(BF16)
    lo = (x - hi.astype(F32)).astype(BF16)
    return hi, lo


def _dot_hilo(x, w_bf16):
    hi, lo = _split_bf16(x)
    return _dot(hi, w_bf16) + _dot(lo, w_bf16)


def _sigmoid(x):
    return 1.0 / (1.0 + jnp.exp(-x))


def _ada_kernel(c_ref, w_ref, b_ref, o_ref):
    c = c_ref[...]
    cs = c * _sigmoid(c)
    c_hi, c_lo = _split_bf16(cs)
    w_hi, w_lo = _split_bf16(w_ref[0])
    o_ref[...] = _dot(c_hi, w_hi) + _dot(c_lo, w_hi) + _dot(c_hi, w_lo) + b_ref[0]


def _ada(c, ada_w, ada_b, layer):
    bsz = c.shape[0]
    n_out = ada_w.shape[2]
    return pl.pallas_call(
        _ada_kernel,
        out_shape=jax.ShapeDtypeStruct((bsz, n_out), F32),
        grid=(n_out // D_MODEL,),
        in_specs=[
            pl.BlockSpec((bsz, D_MODEL), lambda j: (0, 0)),
            pl.BlockSpec((1, D_MODEL, D_MODEL), lambda j: (layer, 0, j)),
            pl.BlockSpec((1, 1, D_MODEL), lambda j: (layer, 0, j)),
        ],
        out_specs=pl.BlockSpec((bsz, D_MODEL), lambda j: (0, j)),
        compiler_params=_cparams(("arbitrary",)),
        name="ada",
    )(c, ada_w, ada_b)


def _rms_mod(x, g, scale, shift):
    ms = jnp.mean(x * x, axis=-1, keepdims=True)
    y = x * lax.rsqrt(ms + RMS_EPS) * g
    return y * (1.0 + scale) + shift


def _inproj_kernel(x_ref, mod_ref, g_ref, w_ref, cs_ref, p_ref, u_ref):
    x = x_ref[0]
    h = _rms_mod(x, g_ref[...], mod_ref[0, 1:2, :], mod_ref[0, 0:1, :]).astype(BF16)
    a = _dot(h, w_ref[:, 0:GROUP]).astype(BF16)
    p_ref[0] = _dot(a, cs_ref[...]).astype(BF16)
    for j in range(6):
        u_ref[0, :, j * GROUP:(j + 1) * GROUP] = _dot(
            h, w_ref[:, (j + 1) * GROUP:(j + 2) * GROUP]).astype(BF16)


def _inproj(x, mod, g, w_in, chan_dft, tm):
    bsz, s, _ = x.shape
    return pl.pallas_call(
        _inproj_kernel,
        out_shape=(jax.ShapeDtypeStruct((bsz, s, 2 * GROUP), BF16),
                   jax.ShapeDtypeStruct((bsz, s, 6 * GROUP), BF16)),
        grid=(bsz, s // tm),
        in_specs=[
            pl.BlockSpec((1, tm, D_MODEL), lambda b, i: (b, i, 0)),
            pl.BlockSpec((1, 8, D_MODEL), lambda b, i: (b, 0, 0)),
            pl.BlockSpec((1, D_MODEL), lambda b, i: (0, 0)),
            pl.BlockSpec((D_MODEL, 7 * GROUP), lambda b, i: (0, 0)),
            pl.BlockSpec((GROUP, 2 * GROUP), lambda b, i: (0, 0)),
        ],
        out_specs=(pl.BlockSpec((1, tm, 2 * GROUP), lambda b, i: (b, i, 0)),
                   pl.BlockSpec((1, tm, 6 * GROUP), lambda b, i: (b, i, 0))),
        compiler_params=_cparams(("arbitrary", "arbitrary")),
        name="inproj",
    )(x, mod, g, w_in, chan_dft)


def _seqdft_kernel(c_ref, s_ref, p_ref, o_ref, acc_ref):
    k = pl.program_id(1)
    b = pl.program_id(2)
    contrib = _dot(c_ref[...], p_ref[0, :, 0:GROUP]) + _dot(s_ref[...], p_ref[0, :, GROUP:2 * GROUP])

    @pl.when(k == 0)
    def _():
        acc_ref[b] = contrib

    @pl.when(k > 0)
    def _():
        acc_ref[b] = acc_ref[b] + contrib

    @pl.when(k == pl.num_programs(1) - 1)
    def _():
        o_ref[b] = acc_ref[b].astype(BF16)


def _seqdft(cos_t, nsin_t, p, tm, tk):
    bsz, s, _ = p.shape
    return pl.pallas_call(
        _seqdft_kernel,
        out_shape=jax.ShapeDtypeStruct((bsz, s, GROUP), BF16),
        grid=(s // tm, s // tk, bsz),
        in_specs=[
            pl.BlockSpec((tm, tk), lambda i, k, b: (i, k)),
            pl.BlockSpec((tm, tk), lambda i, k, b: (i, k)),
            pl.BlockSpec((1, tk, 2 * GROUP), lambda i, k, b: (b, k, 0)),
        ],
        out_specs=pl.BlockSpec((bsz, tm, GROUP), lambda i, k, b: (0, i, 0)),
        scratch_shapes=[pltpu.VMEM((bsz, tm, GROUP), F32)],
        compiler_params=_cparams(("arbitrary", "arbitrary", "arbitrary")),
        name="seqdft",
    )(cos_t, nsin_t, p)


U_V, U_G, U_P, U_BG, U_CG, U_XV = (j * GROUP for j in range(6))


def _mix_kernel(y_ref, u_ref, up_ref, un_ref, x_ref, mod_ref,
                fw_ref, c31w_ref, c31b_ref, gng_ref, gnb_ref, avg_ref, pww_ref,
                poolw_ref, pscale_ref, c3w_ref, sow_ref, wo_ref, n2g_ref, rw_ref, rb_ref,
                x1_ref, h2_ref, lg_ref,
                zext, zsh, pext, s2, s4, s8, s16, qext, cat, *, tq, seq_len):
    i = pl.program_id(1)
    has_prev = (i > 0).astype(F32)
    has_next = (i < pl.num_programs(1) - 1).astype(F32)
    rows = tq + 2 * HALO

    def cols(ref, c0):
        return ref[0, :, c0:c0 + GROUP].astype(F32)

    def fill_ext(dst, fn):
        dst[0:HALO, :] = fn(up_ref) * has_prev
        dst[HALO:HALO + tq, :] = fn(u_ref)
        dst[HALO + tq:rows, :] = fn(un_ref) * has_next

    cat[:, 0:GROUP] = _dot(y_ref[0], fw_ref[...]).astype(BF16)

    fill_ext(zext, lambda r: cols(r, U_V) * _sigmoid(cols(r, U_G)))
    conv = jnp.broadcast_to(c31b_ref[...], (tq, GROUP))
    first = HALO - CONV31 // 2
    for res in range(8):
        taps = [j for j in range(CONV31) if (first + j) % 8 == res]
        span = (first + taps[-1]) // 8 * 8 + tq
        zsh[0:span, :] = zext[pl.ds(res, span), :]
        for j in taps:
            q = (first + j) // 8 * 8
            conv = conv + c31w_ref[j:j + 1, :] * zsh[q:q + tq, :]
    mu = _dot_hilo(conv, avg_ref[...])
    dev = conv - mu
    var = _dot_hilo(dev * dev, avg_ref[...])
    zn = dev * lax.rsqrt(var + GN_EPS) * gng_ref[...] + gnb_ref[...]
    cat[:, GROUP:2 * GROUP] = _dot((zn * _sigmoid(zn)).astype(BF16), pww_ref[...]).astype(BF16)

    fill_ext(pext, lambda r: cols(r, U_P))
    n = rows - 8
    zeros8 = jnp.zeros((8, GROUP), F32)
    s2[0:n, :] = pext[0:n, :] + pext[pl.ds(1, n), :]
    s2[n:rows, :] = zeros8
    s4[0:n, :] = s2[0:n, :] + s2[pl.ds(2, n), :]
    s4[n:rows, :] = zeros8
    s8[0:n, :] = s4[0:n, :] + s4[pl.ds(4, n), :]
    s8[n:rows, :] = zeros8
    s16[0:n, :] = s8[0:n, :] + s8[pl.ds(8, n), :]
    lane = lax.broadcasted_iota(jnp.int32, (tq, GROUP), 1)
    pos = lax.broadcasted_iota(jnp.int32, (tq, GROUP), 0) + i * tq
    win = jnp.where(lane < HEAD_DIM, s2[pl.ds(HALO - 1, tq), :],
                    jnp.where(lane < 2 * HEAD_DIM, s4[pl.ds(HALO - 2, tq), :],
                              jnp.where(lane < 3 * HEAD_DIM, s8[pl.ds(HALO - 4, tq), :],
                                        s16[pl.ds(HALO - 8, tq), :])))
    half = jnp.where(lane < HEAD_DIM, 1,
                     jnp.where(lane < 2 * HEAD_DIM, 2, jnp.where(lane < 3 * HEAD_DIM, 4, 8)))
    cnt = jnp.minimum(pos + half, seq_len) - jnp.maximum(pos - half, 0)
    dpool = win / cnt.astype(F32) - pext[HALO:HALO + tq, :]
    cat[:, 2 * GROUP:3 * GROUP] = (_dot(dpool.astype(BF16), poolw_ref[...]) * pscale_ref[...]).astype(BF16)

    fill_ext(qext, lambda r: cols(r, U_CG) * cols(r, U_XV))
    c3 = (c3w_ref[0:1, :] * qext[pl.ds(HALO - 1, tq), :]
          + c3w_ref[1:2, :] * qext[HALO:HALO + tq, :]
          + c3w_ref[2:3, :] * qext[pl.ds(HALO + 1, tq), :])
    cat[:, 3 * GROUP:4 * GROUP] = _dot((cols(u_ref, U_BG) * c3).astype(BF16), sow_ref[...]).astype(BF16)

    mixed = _dot(cat[...], wo_ref[...])
    x1 = x_ref[0] + mod_ref[0, 2:3, :] * mixed
    x1_ref[0] = x1
    h2 = _rms_mod(x1, n2g_ref[...], mod_ref[0, 4:5, :], mod_ref[0, 3:4, :])
    h2_ref[0] = h2
    lg_ref[0] = _dot(h2.astype(BF16), rw_ref[...]) + rb_ref[...]


def _mix(y, u, x, mod, wl, tq):
    bsz, s, _ = x.shape
    rows = tq + 2 * HALO
    hb = tq // HALO
    n_hblk = s // HALO
    full = lambda shape: pl.BlockSpec(shape, lambda b, i: tuple(0 for _ in shape))
    kern = functools.partial(_mix_kernel, tq=tq, seq_len=s)
    return pl.pallas_call(
        kern,
        out_shape=(jax.ShapeDtypeStruct((bsz, s, D_MODEL), F32),
                   jax.ShapeDtypeStruct((bsz, s, D_MODEL), F32),
                   jax.ShapeDtypeStruct((bsz, s, LANES), F32)),
        grid=(bsz, s // tq),
        in_specs=[
            pl.BlockSpec((1, tq, GROUP), lambda b, i: (b, i, 0)),
            pl.BlockSpec((1, tq, 6 * GROUP), lambda b, i: (b, i, 0)),
            pl.BlockSpec((1, HALO, 6 * GROUP), lambda b, i: (b, jnp.maximum(i * hb - 1, 0), 0)),
            pl.BlockSpec((1, HALO, 6 * GROUP), lambda b, i: (b, jnp.minimum((i + 1) * hb, n_hblk - 1), 0)),
            pl.BlockSpec((1, tq, D_MODEL), lambda b, i: (b, i, 0)),
            pl.BlockSpec((1, 8, D_MODEL), lambda b, i: (b, 0, 0)),
            full((GROUP, GROUP)),
            full((32, GROUP)),
            full((1, GROUP)),
            full((1, GROUP)),
            full((1, GROUP)),
            full((GROUP, GROUP)),
            full((GROUP, GROUP)),
            full((GROUP, GROUP)),
            full((1, GROUP)),
            full((8, GROUP)),
            full((GROUP, GROUP)),
            full((D_MODEL, D_MODEL)),
            full((1, D_MODEL)),
            full((D_MODEL, LANES)),
            full((1, LANES)),
        ],
        out_specs=(pl.BlockSpec((1, tq, D_MODEL), lambda b, i: (b, i, 0)),
                   pl.BlockSpec((1, tq, D_MODEL), lambda b, i: (b, i, 0)),
                   pl.BlockSpec((1, tq, LANES), lambda b, i: (b, i, 0))),
        scratch_shapes=[pltpu.VMEM((rows, GROUP), F32) for _ in range(8)]
        + [pltpu.VMEM((tq, D_MODEL), BF16)],
        compiler_params=_cparams(("arbitrary", "arbitrary")),
        name="mix",
    )(y, u, u, u, x, mod, wl["fourier_w"], wl["conv31_w"], wl["conv31_b"], wl["gn_g"], wl["gn_b"],
      wl["avg"], wl["pw_w"], wl["pool_w"], wl["pool_scale"], wl["conv3_w"], wl["sconv_out_w"],
      wl["w_o"], wl["norm2_g"], wl["router_w"], wl["router_b"])


def _route_kernel(lg_ref, e_ref, rank_ref, gate_ref, base_ref, tcnt_ref, cnt_ref, carry, *, tr, sub):
    i = pl.program_id(0)

    @pl.when(i == 0)
    def _():
        carry[...] = jnp.zeros_like(carry)

    lane = lax.broadcasted_iota(jnp.int32, (tr, LANES), 1).astype(F32)
    neg = jnp.float32(-jnp.inf)
    work = jnp.where(lane < N_EXPERTS, lg_ref[...], neg)
    vals, idxs, hots = [], [], []
    for _ in range(TOP_K):
        m = jnp.max(work, axis=1, keepdims=True)
        idx = jnp.min(jnp.where(work == m, lane, float(LANES)), axis=1, keepdims=True)
        hot = lane == idx
        vals.append(m)
        idxs.append(idx.astype(jnp.int32))
        hots.append(hot)
        work = jnp.where(hot, neg, work)
    exps = [jnp.exp(v - vals[0]) for v in vals]
    denom = exps[0] + exps[1] + exps[2] + exps[3]
    member = sum(h.astype(F32) for h in hots)
    r_io = lax.broadcasted_iota(jnp.int32, (tr, tr), 0)
    c_io = lax.broadcasted_iota(jnp.int32, (tr, tr), 1)
    lower = (c_io < r_io).astype(BF16)
    before = _dot(lower, member.astype(BF16)) + carry[...]
    col = lax.broadcasted_iota(jnp.int32, (tr, TOP_K), 1)
    e_out = jnp.zeros((tr, TOP_K), jnp.int32)
    r_out = jnp.zeros((tr, TOP_K), jnp.int32)
    g_out = jnp.zeros((tr, TOP_K), F32)
    for k in range(TOP_K):
        rk = jnp.sum(jnp.where(hots[k], before, 0.0), axis=1, keepdims=True).astype(jnp.int32)
        e_out = jnp.where(col == k, idxs[k], e_out)
        r_out = jnp.where(col == k, rk, r_out)
        g_out = jnp.where(col == k, exps[k] / denom, g_out)
    e_ref[...] = e_out
    rank_ref[...] = r_out
    gate_ref[...] = g_out
    running = carry[...]
    for q in range(tr // sub):
        cnt_q = jnp.sum(member[q * sub:(q + 1) * sub, :], axis=0, keepdims=True)
        base_ref[q] = running.astype(jnp.int32)
        tcnt_ref[q] = cnt_q.astype(jnp.int32)
        running = running + cnt_q
    carry[...] = running
    cnt_ref[...] = running


def _route(logits, sub):
    t_tok = logits.shape[0]
    tr = min(2 * sub, t_tok)
    n_steps = t_tok // tr
    n_tiles = t_tok // sub
    kern = functools.partial(_route_kernel, tr=tr, sub=sub)
    tile = lambda w: pl.BlockSpec((tr, w), lambda i: (i, 0))
    per_tile = pl.BlockSpec((tr // sub, 1, LANES), lambda i: (i, 0, 0))
    return pl.pallas_call(
        kern,
        out_shape=(jax.ShapeDtypeStruct((t_tok, TOP_K), jnp.int32),
                   jax.ShapeDtypeStruct((t_tok, TOP_K), jnp.int32),
                   jax.ShapeDtypeStruct((t_tok, TOP_K), F32),
                   jax.ShapeDtypeStruct((n_tiles, 1, LANES), jnp.int32),
                   jax.ShapeDtypeStruct((n_tiles, 1, LANES), jnp.int32),
                   jax.ShapeDtypeStruct((1, LANES), F32)),
        grid=(n_steps,),
        in_specs=[tile(LANES)],
        out_specs=(tile(TOP_K), tile(TOP_K), tile(TOP_K), per_tile, per_tile,
                   pl.BlockSpec((1, LANES), lambda i: (0, 0))),
        scratch_shapes=[pltpu.VMEM((1, LANES), F32)],
        compiler_params=_cparams(("arbitrary",)),
        name="route",
    )(logits)


F32_TILE = 8
DISP_ROWS = TOP_K * 256 + N_EXPERTS * 2 * F32_TILE
CARRY_ROWS = N_EXPERTS * F32_TILE
DISP_WIN = 64


def _dispatch_kernel(src_ref, dst_ref, nch_ref, head_ref, nwin_ref, nextra_ref, pend_ref, npad_ref,
                     e_ref, rank_ref, off_ref, h_ref, xs_hbm, outbuf, carry, zeros, sems, sem_z, *, td):
    i = pl.program_id(0)
    n_steps = pl.num_programs(0)
    slot = i % 2

    def zero_blk(row):
        return pltpu.make_async_copy(zeros, xs_hbm.at[pl.ds(pl.multiple_of(row, MOE_BLOCK), MOE_BLOCK)], sem_z)

    @pl.when(i == 0)
    def _():
        zeros[...] = jnp.zeros_like(zeros)
        carry[...] = jnp.zeros_like(carry)
        outbuf[...] = jnp.zeros_like(outbuf)
        first_unused = pend_ref[N_EXPERTS - 1] // MOE_BLOCK
        n_blocks = xs_hbm.shape[0] // MOE_BLOCK

        def start(e, c):
            @pl.when(npad_ref[e] > 0)
            def _():
                zero_blk(pend_ref[e] - MOE_BLOCK).start()

            @pl.when(npad_ref[e] > MOE_BLOCK)
            def _():
                zero_blk(pend_ref[e] - 2 * MOE_BLOCK).start()
            return c

        def wait(e, c):
            @pl.when(npad_ref[e] > 0)
            def _():
                zero_blk(pend_ref[e] - MOE_BLOCK).wait()

            @pl.when(npad_ref[e] > MOE_BLOCK)
            def _():
                zero_blk(pend_ref[e] - 2 * MOE_BLOCK).wait()
            return c

        def start_tail(b, c):
            zero_blk(b * MOE_BLOCK).start()
            return c

        def wait_tail(b, c):
            zero_blk(b * MOE_BLOCK).wait()
            return c

        lax.fori_loop(0, N_EXPERTS, start, 0)
        lax.fori_loop(first_unused, n_blocks, start_tail, 0)
        lax.fori_loop(0, N_EXPERTS, wait, 0)
        lax.fori_loop(first_unused, n_blocks, wait_tail, 0)

    def window_copies(tile, sl, act):
        for e in range(N_EXPERTS):
            n_ch = nch_ref[tile * N_EXPERTS + e]
            src0 = pl.multiple_of(src_ref[tile * N_EXPERTS + e], F32_TILE)
            dst0 = pl.multiple_of(dst_ref[tile * N_EXPERTS + e], F32_TILE)

            @pl.when(n_ch > 0)
            def _():
                act(pltpu.make_async_copy(outbuf.at[sl, pl.ds(src0, DISP_WIN)],
                                          xs_hbm.at[pl.ds(dst0, DISP_WIN)], sems.at[sl]))

            def per_chunk(ci, c2):
                src = pl.multiple_of(src0 + ci * F32_TILE, F32_TILE)
                dst = pl.multiple_of(dst0 + ci * F32_TILE, F32_TILE)
                act(pltpu.make_async_copy(outbuf.at[sl, pl.ds(src, F32_TILE)],
                                          xs_hbm.at[pl.ds(dst, F32_TILE)], sems.at[sl]))
                return c2

            lax.fori_loop(DISP_WIN // F32_TILE, n_ch, per_chunk, 0)

    lane_td = lax.broadcasted_iota(jnp.int32, (td, LANES), 1)
    off_f = off_ref[0].astype(F32)
    pos_cols = jnp.full((td, LANES), -1.0, F32)
    for k in range(TOP_K):
        off_k = jnp.sum(jnp.where(lane_td == e_ref[:, k:k + 1], off_f, 0.0), axis=1, keepdims=True)
        pos_cols = jnp.where(lane_td == k, rank_ref[:, k:k + 1].astype(F32) + off_k, pos_cols)
    pos_t = jnp.transpose(pos_cols).astype(jnp.int32)
    pos = [jnp.broadcast_to(pos_t[k:k + 1, :], (GROUP, td)) for k in range(TOP_K)]
    row_io = lax.broadcasted_iota(jnp.int32, (GROUP, td), 0)
    h_bf = h_ref[...].astype(BF16)
    for rc in range(DISP_ROWS // GROUP):
        r = row_io + rc * GROUP
        tok = jnp.zeros((GROUP, td), F32)
        for k in range(TOP_K):
            tok = jnp.where(r == pos[k], 1.0, tok)
        outbuf[slot, rc * GROUP:(rc + 1) * GROUP, :] = _dot(tok.astype(BF16), h_bf)

    sub_io = lax.broadcasted_iota(jnp.int32, (F32_TILE, D_MODEL), 0)
    for e in range(N_EXPERTS):
        es = slice(e * F32_TILE, (e + 1) * F32_TILE)
        n_ch = nch_ref[i * N_EXPERTS + e]
        first = pl.multiple_of(src_ref[i * N_EXPERTS + e], F32_TILE)
        outbuf[slot, pl.ds(first, F32_TILE), :] = jnp.where(
            sub_io < head_ref[i * N_EXPERTS + e], carry[es, :], outbuf[slot, pl.ds(first, F32_TILE), :])
        last = pl.multiple_of(first + jnp.maximum(n_ch - 1, 0) * F32_TILE, F32_TILE)
        carry[es, :] = jnp.where(n_ch > 0, outbuf[slot, pl.ds(last, F32_TILE), :], carry[es, :])

    def wait_copies(tile, sl):
        def wait_window(c, carry_):
            pltpu.make_async_copy(outbuf.at[sl, pl.ds(0, DISP_WIN)], xs_hbm.at[pl.ds(0, DISP_WIN)],
                                  sems.at[sl]).wait()
            return carry_

        def wait_chunk(c, carry_):
            pltpu.make_async_copy(outbuf.at[sl, pl.ds(0, F32_TILE)], xs_hbm.at[pl.ds(0, F32_TILE)],
                                  sems.at[sl]).wait()
            return carry_

        lax.fori_loop(0, nwin_ref[tile], wait_window, 0)
        lax.fori_loop(0, nextra_ref[tile], wait_chunk, 0)

    @pl.when(i > 0)
    def _():
        wait_copies(i - 1, 1 - slot)

    window_copies(i, slot, lambda cp: cp.start())

    @pl.when(i == n_steps - 1)
    def _():
        wait_copies(i, slot)


def _dispatch(chunk_src, chunk_dst, n_chunks, head, n_windows, n_extra, pend, npad, e_idx, rank, col_off, h2,
              n_slots, td):
    t_tok = h2.shape[0]
    assert DISP_ROWS >= TOP_K * td + N_EXPERTS * 2 * (F32_TILE - 1) + F32_TILE and DISP_ROWS % GROUP == 0
    kern = functools.partial(_dispatch_kernel, td=td)
    tok4 = lambda: pl.BlockSpec((td, TOP_K), lambda i, *_: (i, 0))
    return pl.pallas_call(
        kern,
        out_shape=jax.ShapeDtypeStruct((n_slots, D_MODEL), F32),
        grid_spec=pltpu.PrefetchScalarGridSpec(
            num_scalar_prefetch=8,
            grid=(t_tok // td,),
            in_specs=[tok4(), tok4(),
                      pl.BlockSpec((1, 1, LANES), lambda i, *_: (i, 0, 0)),
                      pl.BlockSpec((td, D_MODEL), lambda i, *_: (i, 0))],
            out_specs=pl.BlockSpec(memory_space=pl.ANY),
            scratch_shapes=[pltpu.VMEM((2, DISP_ROWS + DISP_WIN, D_MODEL), F32),
                            pltpu.VMEM((CARRY_ROWS, D_MODEL), F32),
                            pltpu.VMEM((MOE_BLOCK, D_MODEL), F32),
                            pltpu.SemaphoreType.DMA((2,)),
                            pltpu.SemaphoreType.DMA]),
        compiler_params=_cparams(("arbitrary",)),
        name="dispatch",
    )(chunk_src, chunk_dst, n_chunks, head, n_windows, n_extra, pend, npad, e_idx, rank, col_off, h2)


def _experts_kernel(be_ref, nu_ref, xs_ref, wg_ref, bg_ref, wu_ref, bu_ref, wd_ref, bd_ref, o_ref, act):
    i = pl.program_id(0)

    @pl.when(i < nu_ref[0])
    def _():
        x = xs_ref[...].astype(BF16)
        for f in range(D_FF // GROUP):
            fs = slice(f * GROUP, (f + 1) * GROUP)
            gt = jnp.minimum(_dot(x, wg_ref[0, :, fs]) + bg_ref[0, :, fs], SWIGLU_LIMIT)
            up = jnp.clip(_dot(x, wu_ref[0, :, fs]) + bu_ref[0, :, fs], -SWIGLU_LIMIT, SWIGLU_LIMIT)
            act[:, fs] = ((up + 1.0) * (gt * _sigmoid(SWIGLU_ALPHA * gt))).astype(BF16)
        o_ref[...] = (_dot(act[...], wd_ref[0]) + bd_ref[0]).astype(BF16)

    @pl.when(i >= nu_ref[0])
    def _():
        o_ref[...] = jnp.zeros_like(o_ref)


def _experts(block_e, n_used, xs, wl, layer):
    n_slots = xs.shape[0]
    n_blocks = n_slots // MOE_BLOCK

    def expert(i, be, nu):
        return layer * N_EXPERTS + be[jnp.minimum(i, nu[0] - 1)]

    wspec = lambda: pl.BlockSpec((1, D_MODEL, D_FF), lambda i, be, nu: (expert(i, be, nu), 0, 0))
    bspec = lambda: pl.BlockSpec((1, 1, D_FF), lambda i, be, nu: (expert(i, be, nu), 0, 0))
    return pl.pallas_call(
        _experts_kernel,
        out_shape=jax.ShapeDtypeStruct((n_slots, D_MODEL), BF16),
        grid_spec=pltpu.PrefetchScalarGridSpec(
            num_scalar_prefetch=2,
            grid=(n_blocks,),
            in_specs=[pl.BlockSpec((MOE_BLOCK, D_MODEL), lambda i, be, nu: (jnp.minimum(i, nu[0] - 1), 0)),
                      wspec(), bspec(), wspec(), bspec(), wspec(), bspec()],
            out_specs=pl.BlockSpec((MOE_BLOCK, D_MODEL), lambda i, be, nu: (i, 0)),
            scratch_shapes=[pltpu.VMEM((MOE_BLOCK, D_FF), BF16)]),
        compiler_params=_cparams(("arbitrary",)),
        name="experts",
    )(block_e, n_used, xs, wl["exp_gate_w"], wl["exp_gate_b"], wl["exp_up_w"], wl["exp_up_b"],
      wl["exp_down_w"], wl["exp_down_b"])


ROW_TILE = 16
SEG_ROWS = TOP_K * 256 + N_EXPERTS * 2 * ROW_TILE


def _combine_kernel(src_ref, dst_ref, nch_ref, tot_ref, e_ref, rank_ref, gate_ref, off_ref, x1_ref, mod_ref,
                    fg_ref, ys_hbm, o_ref, segbuf, gmat, sems, *, tc, final):
    j = pl.program_id(0)
    n_tiles = pl.num_programs(0)
    slot = j % 2

    def chunks(tile, sl, act):
        def per_expert(e, c):
            src0 = src_ref[tile * N_EXPERTS + e]
            dst0 = dst_ref[tile * N_EXPERTS + e]

            def per_chunk(ci, c2):
                src = pl.multiple_of(src0 + ci * ROW_TILE, ROW_TILE)
                dst = pl.multiple_of(dst0 + ci * ROW_TILE, ROW_TILE)
                act(pltpu.make_async_copy(ys_hbm.at[pl.ds(src, ROW_TILE)],
                                          segbuf.at[sl, pl.ds(dst, ROW_TILE)], sems.at[sl]))
                return c2

            lax.fori_loop(0, nch_ref[tile * N_EXPERTS + e], per_chunk, 0)
            return c

        lax.fori_loop(0, N_EXPERTS, per_expert, 0)

    @pl.when(j == 0)
    def _():
        segbuf[...] = jnp.zeros_like(segbuf)
        chunks(0, 0, lambda cp: cp.start())

    @pl.when(j + 1 < n_tiles)
    def _():
        chunks(j + 1, 1 - slot, lambda cp: cp.start())

    lane_tc = lax.broadcasted_iota(jnp.int32, (tc, LANES), 1)
    off_f = off_ref[0].astype(F32)
    cols = []
    for k in range(TOP_K):
        off_k = jnp.sum(jnp.where(lane_tc == e_ref[:, k:k + 1], off_f, 0.0), axis=1, keepdims=True)
        cols.append(rank_ref[:, k:k + 1] + off_k.astype(jnp.int32))
    colb = [jnp.broadcast_to(c, (tc, LANES)) for c in cols]
    gateb = [jnp.broadcast_to(gate_ref[:, k:k + 1], (tc, LANES)) for k in range(TOP_K)]
    for p in range(SEG_ROWS // LANES):
        col_io = lane_tc + p * LANES
        g = jnp.zeros((tc, LANES), F32)
        for k in reversed(range(TOP_K)):
            g = jnp.where(col_io == colb[k], gateb[k], g)
        gmat[:, p * LANES:(p + 1) * LANES] = g.astype(BF16)

    def wait_chunk(c, carry):
        pltpu.make_async_copy(ys_hbm.at[pl.ds(0, ROW_TILE)], segbuf.at[slot, pl.ds(0, ROW_TILE)],
                              sems.at[slot]).wait()
        return carry

    lax.fori_loop(0, tot_ref[j], wait_chunk, 0)
    moe = _dot(gmat[...], segbuf[slot])

    x2 = x1_ref[...] + mod_ref[0, 5:6, :] * moe
    if final:
        ms = jnp.mean(x2 * x2, axis=-1, keepdims=True)
        x2 = x2 * lax.rsqrt(ms + RMS_EPS) * fg_ref[...]
    o_ref[...] = x2


def _combine(chunk_src, chunk_dst, n_chunks, tile_chunks, e_idx, rank, gates, col_off, x1, mod, final_g, ys,
             seq_len, tc, final):
    t_tok = x1.shape[0]
    assert SEG_ROWS >= TOP_K * tc + N_EXPERTS * 2 * (ROW_TILE - 1) and SEG_ROWS % GROUP == 0
    kern = functools.partial(_combine_kernel, tc=tc, final=final)
    tok4 = lambda: pl.BlockSpec((tc, TOP_K), lambda i, *_: (i, 0))
    return pl.pallas_call(
        kern,
        out_shape=jax.ShapeDtypeStruct((t_tok, D_MODEL), F32),
        grid_spec=pltpu.PrefetchScalarGridSpec(
            num_scalar_prefetch=4,
            grid=(t_tok // tc,),
            in_specs=[tok4(), tok4(), tok4(),
                      pl.BlockSpec((1, 1, LANES), lambda i, *_: (i, 0, 0)),
                      pl.BlockSpec((tc, D_MODEL), lambda i, *_: (i, 0)),
                      pl.BlockSpec((1, 8, D_MODEL), lambda i, *_: ((i * tc) // seq_len, 0, 0)),
                      pl.BlockSpec((1, D_MODEL), lambda i, *_: (0, 0)),
                      pl.BlockSpec(memory_space=pl.ANY)],
            out_specs=pl.BlockSpec((tc, D_MODEL), lambda i, *_: (i, 0)),
            scratch_shapes=[pltpu.VMEM((2, SEG_ROWS, D_MODEL), BF16),
                            pltpu.VMEM((tc, SEG_ROWS), BF16),
                            pltpu.SemaphoreType.DMA((2,))]),
        compiler_params=_cparams(("arbitrary",)),
        name="combine",
    )(chunk_src, chunk_dst, n_chunks, tile_chunks, e_idx, rank, gates, col_off, x1, mod, final_g, ys)


def _dft_tables(n):
    j = jnp.arange(n, dtype=jnp.int32)
    ang = ((j[:, None] * j[None, :]) % n).astype(F32) * (2.0 * math.pi / n)
    scale = 1.0 / math.sqrt(n)
    return jnp.cos(ang) * scale, jnp.sin(ang) * scale


def _block_diag(blocks):
    n, r, c = blocks.shape
    rows = lax.broadcasted_iota(jnp.int32, (n * r, n * c), 0) // r
    cols = lax.broadcasted_iota(jnp.int32, (n * r, n * c), 1) // c
    return jnp.where(rows == cols, jnp.tile(blocks.reshape(n * r, c), (1, n)), 0)


def _prep_weights(w):
    depth = w["w_in"].shape[0]
    c64, s64 = _dft_tables(HEAD_DIM)
    n_heads = GROUP // HEAD_DIM
    chan_dft = jnp.concatenate([_block_diag(jnp.stack([c64] * n_heads)),
                                _block_diag(jnp.stack([s64] * n_heads))], axis=1).astype(BF16)
    avg = _block_diag(jnp.full((n_heads, HEAD_DIM, HEAD_DIM), 1.0 / HEAD_DIM, F32)).astype(BF16)
    stacked = dict(
        exp_gate_w=w["exp_gate_w"].astype(BF16).reshape(depth * N_EXPERTS, D_MODEL, D_FF),
        exp_up_w=w["exp_up_w"].astype(BF16).reshape(depth * N_EXPERTS, D_MODEL, D_FF),
        exp_down_w=w["exp_down_w"].astype(BF16).reshape(depth * N_EXPERTS, D_FF, D_MODEL),
        exp_gate_b=w["exp_gate_b"].reshape(depth * N_EXPERTS, 1, D_FF),
        exp_up_b=w["exp_up_b"].reshape(depth * N_EXPERTS, 1, D_FF),
        exp_down_b=w["exp_down_b"].reshape(depth * N_EXPERTS, 1, D_MODEL))
    layers = []
    for l in range(depth):
        layers.append(dict(
            stacked,
            norm1_g=w["norm1_g"][l].reshape(1, D_MODEL),
            norm2_g=w["norm2_g"][l].reshape(1, D_MODEL),
            w_in=w["w_in"][l].astype(BF16),
            fourier_w=w["fourier_w"][l].astype(BF16),
            conv31_w=jnp.pad(w["conv31_w"][l], ((0, 1), (0, 0))),
            conv31_b=w["conv31_b"][l].reshape(1, GROUP),
            gn_g=w["gn_g"][l].reshape(1, GROUP),
            gn_b=w["gn_b"][l].reshape(1, GROUP),
            avg=avg,
            pw_w=w["pw_w"][l].astype(BF16),
            pool_w=_block_diag(w["pool_w"][l]).astype(BF16),
            pool_scale=w["pool_scale"][l].reshape(1, GROUP),
            conv3_w=jnp.pad(w["conv3_w"][l], ((0, 5), (0, 0))),
            sconv_out_w=w["sconv_out_w"][l].astype(BF16),
            w_o=w["w_o"][l].astype(BF16),
            router_w=jnp.pad(w["router_w"][l], ((0, 0), (0, LANES - N_EXPERTS))).astype(BF16),
            router_b=jnp.pad(w["router_b"][l], (0, LANES - N_EXPERTS)).reshape(1, LANES),
        ))
    return chan_dft, layers


def _tiles(bsz, s):
    t_seq = min(512, s)
    t_dft_m = min(s, max(128, (8 * 1024 * 1024) // (bsz * GROUP * 4)))
    t_dft_k = min(2048, s)
    return dict(seq=t_seq, dft_m=t_dft_m, dft_k=t_dft_k, tok=min(256, bsz * s))


def _moe(h2, logits, x1, mod, final_g, wl, layer, seq_len, tiles, final):
    t_tok = h2.shape[0]
    tok = tiles["tok"]
    n_assign = t_tok * TOP_K
    n_blocks = -(-(n_assign + N_EXPERTS * DISP_WIN) // MOE_BLOCK) + N_EXPERTS
    n_slots = n_blocks * MOE_BLOCK
    e_idx, rank, gates, base, tile_cnt, counts = _route(logits, tok)
    counts = counts[0, :N_EXPERTS].astype(jnp.int32)
    padded = jnp.where(counts > 0, (counts + DISP_WIN + MOE_BLOCK - 1) // MOE_BLOCK * MOE_BLOCK, 0)
    pend = jnp.cumsum(padded)
    pstart = pend - padded
    npad = padded - counts
    block_start = jnp.arange(n_blocks, dtype=jnp.int32) * MOE_BLOCK
    block_e = jnp.minimum(jnp.sum(block_start[:, None] >= pend[None, :], axis=1), N_EXPERTS - 1).astype(jnp.int32)
    n_used = (pend[-1:] // MOE_BLOCK).astype(jnp.int32)
    seg_lo = pstart[None, :] + base[:, 0, :N_EXPERTS]
    seg_n = tile_cnt[:, 0, :N_EXPERTS]
    win_lo = seg_lo - (seg_lo & (ROW_TILE - 1))
    win_rows = jnp.where(seg_n > 0, -(-(seg_lo + seg_n - win_lo) // ROW_TILE) * ROW_TILE, 0)
    win_dst = jnp.cumsum(win_rows, axis=1) - win_rows
    col_off = jnp.pad(win_dst - win_lo + pstart[None, :], ((0, 0), (0, LANES - N_EXPERTS)))
    col_off = col_off.reshape(-1, 1, LANES).astype(jnp.int32)
    flat = lambda a: a.reshape(-1).astype(jnp.int32)
    head = seg_lo & (F32_TILE - 1)
    dwin_lo = seg_lo - head
    dwin_rows = jnp.where(seg_n > 0, -(-(seg_lo + seg_n - dwin_lo) // F32_TILE) * F32_TILE, 0)
    dwin_src = jnp.cumsum(dwin_rows, axis=1) - dwin_rows
    dcol_off = jnp.pad(dwin_src - dwin_lo + pstart[None, :], ((0, 0), (0, LANES - N_EXPERTS)))
    dcol_off = dcol_off.reshape(-1, 1, LANES).astype(jnp.int32)
    d_chunks = dwin_rows // F32_TILE
    xs = _dispatch(flat(dwin_src), flat(dwin_lo), flat(d_chunks), flat(jnp.where(seg_n > 0, head, 0)),
                   flat(jnp.sum(d_chunks > 0, axis=1)),
                   flat(jnp.sum(jnp.maximum(d_chunks - DISP_WIN // F32_TILE, 0), axis=1)),
                   pend.astype(jnp.int32), npad.astype(jnp.int32), e_idx, rank, dcol_off, h2, n_slots, tok)
    ys = _experts(block_e, n_used, xs, wl, layer)
    return _combine(flat(win_lo), flat(win_dst), flat(win_rows // ROW_TILE),
                    flat(jnp.sum(win_rows // ROW_TILE, axis=1)), e_idx, rank, gates, col_off,
                    x1, mod, final_g, ys, seq_len, tok, final)


def _encoder(x, c, w, chan_dft, layers, tables, tiles=None):
    bsz, s, _ = x.shape
    tiles = tiles or _tiles(bsz, s)
    cos_t, nsin_t = tables
    depth = len(layers)
    final_g = w["final_g"].reshape(1, D_MODEL)
    ada_b = w["ada_b"].reshape(depth, 1, 6 * D_MODEL)
    for l, wl in enumerate(layers):
        mod = _ada(c, w["ada_w"], ada_b, l).reshape(bsz, 6, D_MODEL)
        mod = jnp.pad(mod, ((0, 0), (0, 2), (0, 0)))
        p, u = _inproj(x, mod, wl["norm1_g"], wl["w_in"], chan_dft, tiles["seq"])
        y = _seqdft(cos_t, nsin_t, p, tiles["dft_m"], tiles["dft_k"])
        x1, h2, logits = _mix(y, u, x, mod, wl, tiles["seq"])
        x = _moe(h2.reshape(bsz * s, D_MODEL), logits.reshape(bsz * s, LANES),
                 x1.reshape(bsz * s, D_MODEL), mod, final_g, wl, l, s, tiles,
                 final=(l == depth - 1)).reshape(bsz, s, D_MODEL)
    return x


def _seq_tables(s):
    k = jnp.arange(s, dtype=jnp.int32)[None, :]

    def cos_sin(j):
        ang = ((j[:, None] * k) % s).astype(F32) * (2.0 * math.pi / s)
        return jnp.cos(ang), jnp.sin(ang)

    ca, sa = cos_sin(jnp.arange(s // LANES, dtype=jnp.int32) * LANES)
    cb, sb = cos_sin(jnp.arange(LANES, dtype=jnp.int32))
    scale = 1.0 / math.sqrt(s)
    cos_t = (ca[:, None, :] * cb[None, :, :] - sa[:, None, :] * sb[None, :, :]) * scale
    nsin_t = (sa[:, None, :] * cb[None, :, :] + ca[:, None, :] * sb[None, :, :]) * (-scale)
    return cos_t.reshape(s, s).astype(BF16), nsin_t.reshape(s, s).astype(BF16)


def kernel(x_prompt, x_sample, c_prompt, c_sample, norm1_g, norm2_g, ada_w, ada_b, w_in, fourier_w, conv31_w, conv31_b, gn_g, gn_b, pw_w, pool_w, pool_scale, conv3_w, sconv_out_w, w_o, router_w, router_b, exp_gate_w, exp_gate_b, exp_up_w, exp_up_b, exp_down_w, exp_down_b, final_g):
    w = dict(norm1_g=norm1_g, norm2_g=norm2_g, ada_w=ada_w, ada_b=ada_b, w_in=w_in, fourier_w=fourier_w,
             conv31_w=conv31_w, conv31_b=conv31_b, gn_g=gn_g, gn_b=gn_b, pw_w=pw_w, pool_w=pool_w,
             pool_scale=pool_scale, conv3_w=conv3_w, sconv_out_w=sconv_out_w, w_o=w_o, router_w=router_w,
             router_b=router_b, exp_gate_w=exp_gate_w, exp_gate_b=exp_gate_b, exp_up_w=exp_up_w,
             exp_up_b=exp_up_b, exp_down_w=exp_down_w, exp_down_b=exp_down_b, final_g=final_g)
    chan_dft, layers = _prep_weights(w)
    y_prompt = _encoder(x_prompt, c_prompt, w, chan_dft, layers, _seq_tables(x_prompt.shape[1]))
    y_sample = _encoder(x_sample, c_sample, w, chan_dft, layers, _seq_tables(x_sample.shape[1]))
    return (y_prompt, y_sample)
```

```python
import functools
import math

import jax
import jax.numpy as jnp
from jax import lax
from jax.experimental import pallas as pl
from jax.experimental.pallas import tpu as pltpu

D_MODEL = 1024
GROUP = 256
HEAD_DIM = 64
CONV31 = 31
POOL_WINDOWS = (2, 4, 8, 16)
N_EXPERTS = 32
TOP_K = 4
D_FF = 1024
SWIGLU_ALPHA = 1.702
SWIGLU_LIMIT = 7.0
MOE_BLOCK = 512
RMS_EPS = 1e-6
GN_EPS = 1e-5
HALO = 16
LANES = 128
VMEM_LIMIT = 48 * 1024 * 1024

F32 = jnp.float32
BF16 = jnp.bfloat16


def _cparams(sem):
    return pltpu.CompilerParams(dimension_semantics=sem, vmem_limit_bytes=VMEM_LIMIT)


def _dot(a, b):
    return jnp.dot(a, b, preferred_element_type=F32)


def _split_bf16(x):
    hi = x.astype(BF16)
    lo = (x - hi.astype(F32)).astype(BF16)
    return hi, lo


def _dot_hilo(x, w_bf16):
    hi, lo = _split_bf16(x)
    return _dot(hi, w_bf16) + _dot(lo, w_bf16)


def _sigmoid(x):
    return 1.0 / (1.0 + jnp.exp(-x))


def _ada_kernel(c_ref, w_ref, b_ref, o_ref):
    c = c_ref[...]
    cs = c * _sigmoid(c)
    c_hi, c_lo = _split_bf16(cs)
    w_hi, w_lo = _split_bf16(w_ref[0])
    o_ref[...] = _dot(c_hi, w_hi) + _dot(c_lo, w_hi) + _dot(c_hi, w_lo) + b_ref[0]


def _ada(c, ada_w, ada_b, layer):
    bsz = c.shape[0]
    n_out = ada_w.shape[2]
    return pl.pallas_call(
        _ada_kernel,
        out_shape=jax.ShapeDtypeStruct((bsz, n_out), F32),
        grid=(n_out // D_MODEL,),
        in_specs=[
            pl.BlockSpec((bsz, D_MODEL), lambda j: (0, 0)),
            pl.BlockSpec((1, D_MODEL, D_MODEL), lambda j: (layer, 0, j)),
            pl.BlockSpec((1, 1, D_MODEL), lambda j: (layer, 0, j)),
        ],
        out_specs=pl.BlockSpec((bsz, D_MODEL), lambda j: (0, j)),
        compiler_params=_cparams(("arbitrary",)),
        name="ada",
    )(c, ada_w, ada_b)


def _rms_mod(x, g, scale, shift):
    ms = jnp.mean(x * x, axis=-1, keepdims=True)
    y = x * lax.rsqrt(ms + RMS_EPS) * g
    return y * (1.0 + scale) + shift


def _inproj_kernel(x_ref, mod_ref, g_ref, w_ref, cs_ref, p_ref, u_ref):
    x = x_ref[0]
    h = _rms_mod(x, g_ref[...], mod_ref[0, 1:2, :], mod_ref[0, 0:1, :]).astype(BF16)
    a = _dot(h, w_ref[:, 0:GROUP]).astype(BF16)
    p_ref[0] = _dot(a, cs_ref[...]).astype(BF16)
    for j in range(6):
        u_ref[0, :, j * GROUP:(j + 1) * GROUP] = _dot(
            h, w_ref[:, (j + 1) * GROUP:(j + 2) * GROUP]).astype(BF16)


def _inproj(x, mod, g, w_in, chan_dft, tm):
    bsz, s, _ = x.shape
    return pl.pallas_call(
        _inproj_kernel,
        out_shape=(jax.ShapeDtypeStruct((bsz, s, 2 * GROUP), BF16),
                   jax.ShapeDtypeStruct((bsz, s, 6 * GROUP), BF16)),
        grid=(bsz, s // tm),
        in_specs=[
            pl.BlockSpec((1, tm, D_MODEL), lambda b, i: (b, i, 0)),
            pl.BlockSpec((1, 8, D_MODEL), lambda b, i: (b, 0, 0)),
            pl.BlockSpec((1, D_MODEL), lambda b, i: (0, 0)),
            pl.BlockSpec((D_MODEL, 7 * GROUP), lambda b, i: (0, 0)),
            pl.BlockSpec((GROUP, 2 * GROUP), lambda b, i: (0, 0)),
        ],
        out_specs=(pl.BlockSpec((1, tm, 2 * GROUP), lambda b, i: (b, i, 0)),
                   pl.BlockSpec((1, tm, 6 * GROUP), lambda b, i: (b, i, 0))),
        compiler_params=_cparams(("arbitrary", "arbitrary")),
        name="inproj",
    )(x, mod, g, w_in, chan_dft)


def _seqdft_kernel(c_ref, s_ref, p_ref, o_ref, acc_ref):
    k = pl.program_id(1)
    b = pl.program_id(2)
    contrib = _dot(c_ref[...], p_ref[0, :, 0:GROUP]) + _dot(s_ref[...], p_ref[0, :, GROUP:2 * GROUP])

    @pl.when(k == 0)
    def _():
        acc_ref[b] = contrib

    @pl.when(k > 0)
    def _():
        acc_ref[b] = acc_ref[b] + contrib

    @pl.when(k == pl.num_programs(1) - 1)
    def _():
        o_ref[b] = acc_ref[b].astype(BF16)


def _seqdft(cos_t, nsin_t, p, tm, tk):
    bsz, s, _ = p.shape
    return pl.pallas_call(
        _seqdft_kernel,
        out_shape=jax.ShapeDtypeStruct((bsz, s, GROUP), BF16),
        grid=(s // tm, s // tk, bsz),
        in_specs=[
            pl.BlockSpec((tm, tk), lambda i, k, b: (i, k)),
            pl.BlockSpec((tm, tk), lambda i, k, b: (i, k)),
            pl.BlockSpec((1, tk, 2 * GROUP), lambda i, k, b: (b, k, 0)),
        ],
        out_specs=pl.BlockSpec((bsz, tm, GROUP), lambda i, k, b: (0, i, 0)),
        scratch_shapes=[pltpu.VMEM((bsz, tm, GROUP), F32)],
        compiler_params=_cparams(("arbitrary", "arbitrary", "arbitrary")),
        name="seqdft",
    )(cos_t, nsin_t, p)


U_V, U_G, U_P, U_BG, U_CG, U_XV = (j * GROUP for j in range(6))


def _mix_kernel(y_ref, u_ref, up_ref, un_ref, x_ref, mod_ref,
                fw_ref, c31w_ref, c31b_ref, gng_ref, gnb_ref, avg_ref, pww_ref,
                poolw_ref, pscale_ref, c3w_ref, sow_ref, wo_ref, n2g_ref, rw_ref, rb_ref,
                x1_ref, h2_ref, lg_ref,
                zext, zsh, pext, s2, s4, s8, s16, qext, cat, *, tq, seq_len):
    i = pl.program_id(1)
    has_prev = (i > 0).astype(F32)
    has_next = (i < pl.num_programs(1) - 1).astype(F32)
    rows = tq + 2 * HALO

    def cols(ref, c0):
        return ref[0, :, c0:c0 + GROUP].astype(F32)

    def fill_ext(dst, fn):
        dst[0:HALO, :] = fn(up_ref) * has_prev
        dst[HALO:HALO + tq, :] = fn(u_ref)
        dst[HALO + tq:rows, :] = fn(un_ref) * has_next

    cat[:, 0:GROUP] = _dot(y_ref[0], fw_ref[...]).astype(BF16)

    fill_ext(zext, lambda r: cols(r, U_V) * _sigmoid(cols(r, U_G)))
    conv = jnp.broadcast_to(c31b_ref[...], (tq, GROUP))
    first = HALO - CONV31 // 2
    for res in range(8):
        taps = [j for j in range(CONV31) if (first + j) % 8 == res]
        span = (first + taps[-1]) // 8 * 8 + tq
        zsh[0:span, :] = zext[pl.ds(res, span), :]
        for j in taps:
            q = (first + j) // 8 * 8
            conv = conv + c31w_ref[j:j + 1, :] * zsh[q:q + tq, :]
    mu = _dot_hilo(conv, avg_ref[...])
    dev = conv - mu
    var = _dot_hilo(dev * dev, avg_ref[...])
    zn = dev * lax.rsqrt(var + GN_EPS) * gng_ref[...] + gnb_ref[...]
    cat[:, GROUP:2 * GROUP] = _dot((zn * _sigmoid(zn)).astype(BF16), pww_ref[...]).astype(BF16)

    fill_ext(pext, lambda r: cols(r, U_P))
    n = rows - 8
    zeros8 = jnp.zeros((8, GROUP), F32)
    s2[0:n, :] = pext[0:n, :] + pext[pl.ds(1, n), :]
    s2[n:rows, :] = zeros8
    s4[0:n, :] = s2[0:n, :] + s2[pl.ds(2, n), :]
    s4[n:rows, :] = zeros8
    s8[0:n, :] = s4[0:n, :] + s4[pl.ds(4, n), :]
    s8[n:rows, :] = zeros8
    s16[0:n, :] = s8[0:n, :] + s8[pl.ds(8, n), :]
    lane = lax.broadcasted_iota(jnp.int32, (tq, GROUP), 1)
    pos = lax.broadcasted_iota(jnp.int32, (tq, GROUP), 0) + i * tq
    win = jnp.where(lane < HEAD_DIM, s2[pl.ds(HALO - 1, tq), :],
                    jnp.where(lane < 2 * HEAD_DIM, s4[pl.ds(HALO - 2, tq), :],
                              jnp.where(lane < 3 * HEAD_DIM, s8[pl.ds(HALO - 4, tq), :],
                                        s16[pl.ds(HALO - 8, tq), :])))
    half = jnp.where(lane < HEAD_DIM, 1,
                     jnp.where(lane < 2 * HEAD_DIM, 2, jnp.where(lane < 3 * HEAD_DIM, 4, 8)))
    cnt = jnp.minimum(pos + half, seq_len) - jnp.maximum(pos - half, 0)
    dpool = win / cnt.astype(F32) - pext[HALO:HALO + tq, :]
    cat[:, 2 * GROUP:3 * GROUP] = (_dot(dpool.astype(BF16), poolw_ref[...]) * pscale_ref[...]).astype(BF16)

    fill_ext(qext, lambda r: cols(r, U_CG) * cols(r, U_XV))
    c3 = (c3w_ref[0:1, :] * qext[pl.ds(HALO - 1, tq), :]
          + c3w_ref[1:2, :] * qext[HALO:HALO + tq, :]
          + c3w_ref[2:3, :] * qext[pl.ds(HALO + 1, tq), :])
    cat[:, 3 * GROUP:4 * GROUP] = _dot((cols(u_ref, U_BG) * c3).astype(BF16), sow_ref[...]).astype(BF16)

    mixed = _dot(cat[...], wo_ref[...])
    x1 = x_ref[0] + mod_ref[0, 2:3, :] * mixed
    x1_ref[0] = x1
    h2 = _rms_mod(x1, n2g_ref[...], mod_ref[0, 4:5, :], mod_ref[0, 3:4, :])
    h2_ref[0] = h2
    lg_ref[0] = _dot(h2.astype(BF16), rw_ref[...]) + rb_ref[...]


def _mix(y, u, x, mod, wl, tq):
    bsz, s, _ = x.shape
    rows = tq + 2 * HALO
    hb = tq // HALO
    n_hblk = s // HALO
    full = lambda shape: pl.BlockSpec(shape, lambda b, i: tuple(0 for _ in shape))
    kern = functools.partial(_mix_kernel, tq=tq, seq_len=s)
    return pl.pallas_call(
        kern,
        out_shape=(jax.ShapeDtypeStruct((bsz, s, D_MODEL), F32),
                   jax.ShapeDtypeStruct((bsz, s, D_MODEL), F32),
                   jax.ShapeDtypeStruct((bsz, s, LANES), F32)),
        grid=(bsz, s // tq),
        in_specs=[
            pl.BlockSpec((1, tq, GROUP), lambda b, i: (b, i, 0)),
            pl.BlockSpec((1, tq, 6 * GROUP), lambda b, i: (b, i, 0)),
            pl.BlockSpec((1, HALO, 6 * GROUP), lambda b, i: (b, jnp.maximum(i * hb - 1, 0), 0)),
            pl.BlockSpec((1, HALO, 6 * GROUP), lambda b, i: (b, jnp.minimum((i + 1) * hb, n_hblk - 1), 0)),
            pl.BlockSpec((1, tq, D_MODEL), lambda b, i: (b, i, 0)),
            pl.BlockSpec((1, 8, D_MODEL), lambda b, i: (b, 0, 0)),
            full((GROUP, GROUP)),
            full((32, GROUP)),
            full((1, GROUP)),
            full((1, GROUP)),
            full((1, GROUP)),
            full((GROUP, GROUP)),
            full((GROUP, GROUP)),
            full((GROUP, GROUP)),
            full((1, GROUP)),
            full((8, GROUP)),
            full((GROUP, GROUP)),
            full((D_MODEL, D_MODEL)),
            full((1, D_MODEL)),
            full((D_MODEL, LANES)),
            full((1, LANES)),
        ],
        out_specs=(pl.BlockSpec((1, tq, D_MODEL), lambda b, i: (b, i, 0)),
                   pl.BlockSpec((1, tq, D_MODEL), lambda b, i: (b, i, 0)),
                   pl.BlockSpec((1, tq, LANES), lambda b, i: (b, i, 0))),
        scratch_shapes=[pltpu.VMEM((rows, GROUP), F32) for _ in range(8)]
        + [pltpu.VMEM((tq, D_MODEL), BF16)],
        compiler_params=_cparams(("arbitrary", "arbitrary")),
        name="mix",
    )(y, u, u, u, x, mod, wl["fourier_w"], wl["conv31_w"], wl["conv31_b"], wl["gn_g"], wl["gn_b"],
      wl["avg"], wl["pw_w"], wl["pool_w"], wl["pool_scale"], wl["conv3_w"], wl["sconv_out_w"],
      wl["w_o"], wl["norm2_g"], wl["router_w"], wl["router_b"])


def _route_kernel(lg_ref, e_ref, rank_ref, gate_ref, base_ref, tcnt_ref, cnt_ref, carry, *, tr, sub):
    i = pl.program_id(0)

    @pl.when(i == 0)
    def _():
        carry[...] = jnp.zeros_like(carry)

    lane = lax.broadcasted_iota(jnp.int32, (tr, LANES), 1).astype(F32)
    neg = jnp.float32(-jnp.inf)
    work = jnp.where(lane < N_EXPERTS, lg_ref[...], neg)
    vals, idxs, hots = [], [], []
    for _ in range(TOP_K):
        m = jnp.max(work, axis=1, keepdims=True)
        idx = jnp.min(jnp.where(work == m, lane, float(LANES)), axis=1, keepdims=True)
        hot = lane == idx
        vals.append(m)
        idxs.append(idx.astype(jnp.int32))
        hots.append(hot)
        work = jnp.where(hot, neg, work)
    exps = [jnp.exp(v - vals[0]) for v in vals]
    denom = exps[0] + exps[1] + exps[2] + exps[3]
    member = sum(h.astype(F32) for h in hots)
    r_io = lax.broadcasted_iota(jnp.int32, (tr, tr), 0)
    c_io = lax.broadcasted_iota(jnp.int32, (tr, tr), 1)
    lower = (c_io < r_io).astype(BF16)
    before = _dot(lower, member.astype(BF16)) + carry[...]
    col = lax.broadcasted_iota(jnp.int32, (tr, TOP_K), 1)
    e_out = jnp.zeros((tr, TOP_K), jnp.int32)
    r_out = jnp.zeros((tr, TOP_K), jnp.int32)
    g_out = jnp.zeros((tr, TOP_K), F32)
    for k in range(TOP_K):
        rk = jnp.sum(jnp.where(hots[k], before, 0.0), axis=1, keepdims=True).astype(jnp.int32)
        e_out = jnp.where(col == k, idxs[k], e_out)
        r_out = jnp.where(col == k, rk, r_out)
        g_out = jnp.where(col == k, exps[k] / denom, g_out)
    e_ref[...] = e_out
    rank_ref[...] = r_out
    gate_ref[...] = g_out
    running = carry[...]
    for q in range(tr // sub):
        cnt_q = jnp.sum(member[q * sub:(q + 1) * sub, :], axis=0, keepdims=True)
        base_ref[q] = running.astype(jnp.int32)
        tcnt_ref[q] = cnt_q.astype(jnp.int32)
        running = running + cnt_q
    carry[...] = running
    cnt_ref[...] = running


def _route(logits, sub):
    t_tok = logits.shape[0]
    tr = min(2 * sub, t_tok)
    n_steps = t_tok // tr
    n_tiles = t_tok // sub
    kern = functools.partial(_route_kernel, tr=tr, sub=sub)
    tile = lambda w: pl.BlockSpec((tr, w), lambda i: (i, 0))
    per_tile = pl.BlockSpec((tr // sub, 1, LANES), lambda i: (i, 0, 0))
    return pl.pallas_call(
        kern,
        out_shape=(jax.ShapeDtypeStruct((t_tok, TOP_K), jnp.int32),
                   jax.ShapeDtypeStruct((t_tok, TOP_K), jnp.int32),
                   jax.ShapeDtypeStruct((t_tok, TOP_K), F32),
                   jax.ShapeDtypeStruct((n_tiles, 1, LANES), jnp.int32),
                   jax.ShapeDtypeStruct((n_tiles, 1, LANES), jnp.int32),
                   jax.ShapeDtypeStruct((1, LANES), F32)),
        grid=(n_steps,),
        in_specs=[tile(LANES)],
        out_specs=(tile(TOP_K), tile(TOP_K), tile(TOP_K), per_tile, per_tile,
                   pl.BlockSpec((1, LANES), lambda i: (0, 0))),
        scratch_shapes=[pltpu.VMEM((1, LANES), F32)],
        compiler_params=_cparams(("arbitrary",)),
        name="route",
    )(logits)


F32_TILE = 8
DISP_ROWS = TOP_K * 256 + N_EXPERTS * 2 * F32_TILE
CARRY_ROWS = N_EXPERTS * F32_TILE
DISP_WIN = 64
XS_COLS = D_MODEL // 2


def _pack_pairs(x):
    lo = lax.bitcast_convert_type(x[:, :XS_COLS], jnp.uint32)
    hi = lax.bitcast_convert_type(x[:, XS_COLS:], jnp.uint32)
    return (hi & jnp.uint32(0xFFFF0000)) | (lo >> 16)


def _unpack_pairs(w):
    lo = lax.bitcast_convert_type(w << 16, F32).astype(BF16)
    hi = lax.bitcast_convert_type(w & jnp.uint32(0xFFFF0000), F32).astype(BF16)
    return lo, hi


def _dispatch_kernel(src_ref, dst_ref, nch_ref, head_ref, nwin_ref, nextra_ref, pend_ref, npad_ref,
                     e_ref, rank_ref, off_ref, h_ref, xs_hbm, outbuf, carry, zeros, sems, sem_z, *, td):
    i = pl.program_id(0)
    n_steps = pl.num_programs(0)
    slot = i % 2

    def zero_blk(row):
        return pltpu.make_async_copy(zeros, xs_hbm.at[pl.ds(pl.multiple_of(row, MOE_BLOCK), MOE_BLOCK)], sem_z)

    @pl.when(i == 0)
    def _():
        zeros[...] = jnp.zeros_like(zeros)
        carry[...] = jnp.zeros_like(carry)
        outbuf[...] = jnp.zeros_like(outbuf)
        first_unused = pend_ref[N_EXPERTS - 1] // MOE_BLOCK
        n_blocks = xs_hbm.shape[0] // MOE_BLOCK

        def start(e, c):
            @pl.when(npad_ref[e] > 0)
            def _():
                zero_blk(pend_ref[e] - MOE_BLOCK).start()

            @pl.when(npad_ref[e] > MOE_BLOCK)
            def _():
                zero_blk(pend_ref[e] - 2 * MOE_BLOCK).start()
            return c

        def wait(e, c):
            @pl.when(npad_ref[e] > 0)
            def _():
                zero_blk(pend_ref[e] - MOE_BLOCK).wait()

            @pl.when(npad_ref[e] > MOE_BLOCK)
            def _():
                zero_blk(pend_ref[e] - 2 * MOE_BLOCK).wait()
            return c

        def start_tail(b, c):
            zero_blk(b * MOE_BLOCK).start()
            return c

        def wait_tail(b, c):
            zero_blk(b * MOE_BLOCK).wait()
            return c

        lax.fori_loop(0, N_EXPERTS, start, 0)
        lax.fori_loop(first_unused, n_blocks, start_tail, 0)
        lax.fori_loop(0, N_EXPERTS, wait, 0)
        lax.fori_loop(first_unused, n_blocks, wait_tail, 0)

    def window_copies(tile, sl, act):
        for e in range(N_EXPERTS):
            n_ch = nch_ref[tile * N_EXPERTS + e]
            src0 = pl.multiple_of(src_ref[tile * N_EXPERTS + e], F32_TILE)
            dst0 = pl.multiple_of(dst_ref[tile * N_EXPERTS + e], F32_TILE)

            @pl.when(n_ch > 0)
            def _():
                act(pltpu.make_async_copy(outbuf.at[sl, pl.ds(src0, DISP_WIN)],
                                          xs_hbm.at[pl.ds(dst0, DISP_WIN)], sems.at[sl]))

            def per_chunk(ci, c2):
                src = pl.multiple_of(src0 + ci * F32_TILE, F32_TILE)
                dst = pl.multiple_of(dst0 + ci * F32_TILE, F32_TILE)
                act(pltpu.make_async_copy(outbuf.at[sl, pl.ds(src, F32_TILE)],
                                          xs_hbm.at[pl.ds(dst, F32_TILE)], sems.at[sl]))
                return c2

            lax.fori_loop(DISP_WIN // F32_TILE, n_ch, per_chunk, 0)

    lane_td = lax.broadcasted_iota(jnp.int32, (td, LANES), 1)
    off_f = off_ref[0].astype(F32)
    pos_cols = jnp.full((td, LANES), -1.0, F32)
    for k in range(TOP_K):
        off_k = jnp.sum(jnp.where(lane_td == e_ref[:, k:k + 1], off_f, 0.0), axis=1, keepdims=True)
        pos_cols = jnp.where(lane_td == k, rank_ref[:, k:k + 1].astype(F32) + off_k, pos_cols)
    pos_t = jnp.transpose(pos_cols).astype(jnp.int32)
    pos = [jnp.broadcast_to(pos_t[k:k + 1, :], (GROUP, td)) for k in range(TOP_K)]
    row_io = lax.broadcasted_iota(jnp.int32, (GROUP, td), 0)
    h_bf = h_ref[...].astype(BF16)
    for rc in range(DISP_ROWS // GROUP):
        r = row_io + rc * GROUP
        tok = jnp.zeros((GROUP, td), F32)
        for k in range(TOP_K):
            tok = jnp.where(r == pos[k], 1.0, tok)
        outbuf[slot, rc * GROUP:(rc + 1) * GROUP, :] = _pack_pairs(_dot(tok.astype(BF16), h_bf))

    sub_io = lax.broadcasted_iota(jnp.int32, (F32_TILE, XS_COLS), 0)
    for e in range(N_EXPERTS):
        es = slice(e * F32_TILE, (e + 1) * F32_TILE)
        n_ch = nch_ref[i * N_EXPERTS + e]
        first = pl.multiple_of(src_ref[i * N_EXPERTS + e], F32_TILE)
        outbuf[slot, pl.ds(first, F32_TILE), :] = jnp.where(
            sub_io < head_ref[i * N_EXPERTS + e], carry[es, :], outbuf[slot, pl.ds(first, F32_TILE), :])
        last = pl.multiple_of(first + jnp.maximum(n_ch - 1, 0) * F32_TILE, F32_TILE)
        carry[es, :] = jnp.where(n_ch > 0, outbuf[slot, pl.ds(last, F32_TILE), :], carry[es, :])

    def wait_copies(tile, sl):
        def wait_window(c, carry_):
            pltpu.make_async_copy(outbuf.at[sl, pl.ds(0, DISP_WIN)], xs_hbm.at[pl.ds(0, DISP_WIN)],
                                  sems.at[sl]).wait()
            return carry_

        def wait_chunk(c, carry_):
            pltpu.make_async_copy(outbuf.at[sl, pl.ds(0, F32_TILE)], xs_hbm.at[pl.ds(0, F32_TILE)],
                                  sems.at[sl]).wait()
            return carry_

        lax.fori_loop(0, nwin_ref[tile], wait_window, 0)
        lax.fori_loop(0, nextra_ref[tile], wait_chunk, 0)

    @pl.when(i > 0)
    def _():
        wait_copies(i - 1, 1 - slot)

    window_copies(i, slot, lambda cp: cp.start())

    @pl.when(i == n_steps - 1)
    def _():
        wait_copies(i, slot)


def _dispatch(chunk_src, chunk_dst, n_chunks, head, n_windows, n_extra, pend, npad, e_idx, rank, col_off, h2,
              n_slots, td):
    t_tok = h2.shape[0]
    assert DISP_ROWS >= TOP_K * td + N_EXPERTS * 2 * (F32_TILE - 1) + F32_TILE and DISP_ROWS % GROUP == 0
    kern = functools.partial(_dispatch_kernel, td=td)
    tok4 = lambda: pl.BlockSpec((td, TOP_K), lambda i, *_: (i, 0))
    return pl.pallas_call(
        kern,
        out_shape=jax.ShapeDtypeStruct((n_slots, XS_COLS), jnp.uint32),
        grid_spec=pltpu.PrefetchScalarGridSpec(
            num_scalar_prefetch=8,
            grid=(t_tok // td,),
            in_specs=[tok4(), tok4(),
                      pl.BlockSpec((1, 1, LANES), lambda i, *_: (i, 0, 0)),
                      pl.BlockSpec((td, D_MODEL), lambda i, *_: (i, 0))],
            out_specs=pl.BlockSpec(memory_space=pl.ANY),
            scratch_shapes=[pltpu.VMEM((2, DISP_ROWS + DISP_WIN, XS_COLS), jnp.uint32),
                            pltpu.VMEM((CARRY_ROWS, XS_COLS), jnp.uint32),
                            pltpu.VMEM((MOE_BLOCK, XS_COLS), jnp.uint32),
                            pltpu.SemaphoreType.DMA((2,)),
                            pltpu.SemaphoreType.DMA]),
        compiler_params=_cparams(("arbitrary",)),
        name="dispatch",
    )(chunk_src, chunk_dst, n_chunks, head, n_windows, n_extra, pend, npad, e_idx, rank, col_off, h2)


def _experts_kernel(be_ref, nu_ref, xs_ref, wg_ref, bg_ref, wu_ref, bu_ref, wd_ref, bd_ref, o_ref, act):
    i = pl.program_id(0)

    @pl.when(i < nu_ref[0])
    def _():
        x_lo, x_hi = _unpack_pairs(xs_ref[...])

        def proj(w_ref, fs):
            return _dot(x_lo, w_ref[0, :XS_COLS, fs]) + _dot(x_hi, w_ref[0, XS_COLS:, fs])

        for f in range(D_FF // GROUP):
            fs = slice(f * GROUP, (f + 1) * GROUP)
            gt = jnp.minimum(proj(wg_ref, fs) + bg_ref[0, :, fs], SWIGLU_LIMIT)
            up = jnp.clip(proj(wu_ref, fs) + bu_ref[0, :, fs], -SWIGLU_LIMIT, SWIGLU_LIMIT)
            act[:, fs] = ((up + 1.0) * (gt * _sigmoid(SWIGLU_ALPHA * gt))).astype(BF16)
        o_ref[...] = (_dot(act[...], wd_ref[0]) + bd_ref[0]).astype(BF16)

    @pl.when(i >= nu_ref[0])
    def _():
        o_ref[...] = jnp.zeros_like(o_ref)


def _experts(block_e, n_used, xs, wl, layer):
    n_slots = xs.shape[0]
    n_blocks = n_slots // MOE_BLOCK

    def expert(i, be, nu):
        return layer * N_EXPERTS + be[jnp.minimum(i, nu[0] - 1)]

    wspec = lambda: pl.BlockSpec((1, D_MODEL, D_FF), lambda i, be, nu: (expert(i, be, nu), 0, 0))
    bspec = lambda: pl.BlockSpec((1, 1, D_FF), lambda i, be, nu: (expert(i, be, nu), 0, 0))
    return pl.pallas_call(
        _experts_kernel,
        out_shape=jax.ShapeDtypeStruct((n_slots, D_MODEL), BF16),
        grid_spec=pltpu.PrefetchScalarGridSpec(
            num_scalar_prefetch=2,
            grid=(n_blocks,),
            in_specs=[pl.BlockSpec((MOE_BLOCK, XS_COLS), lambda i, be, nu: (jnp.minimum(i, nu[0] - 1), 0)),
                      wspec(), bspec(), wspec(), bspec(), wspec(), bspec()],
            out_specs=pl.BlockSpec((MOE_BLOCK, D_MODEL), lambda i, be, nu: (i, 0)),
            scratch_shapes=[pltpu.VMEM((MOE_BLOCK, D_FF), BF16)]),
        compiler_params=_cparams(("arbitrary",)),
        name="experts",
    )(block_e, n_used, xs, wl["exp_gate_w"], wl["exp_gate_b"], wl["exp_up_w"], wl["exp_up_b"],
      wl["exp_down_w"], wl["exp_down_b"])


ROW_TILE = 16
SEG_ROWS = TOP_K * 256 + N_EXPERTS * 2 * ROW_TILE


def _combine_kernel(src_ref, dst_ref, nch_ref, tot_ref, e_ref, rank_ref, gate_ref, off_ref, x1_ref, mod_ref,
                    fg_ref, ys_hbm, o_ref, segbuf, gmat, sems, *, tc, final):
    j = pl.program_id(0)
    n_tiles = pl.num_programs(0)
    slot = j % 2

    def chunks(tile, sl, act):
        def per_expert(e, c):
            src0 = src_ref[tile * N_EXPERTS + e]
            dst0 = dst_ref[tile * N_EXPERTS + e]

            def per_chunk(ci, c2):
                src = pl.multiple_of(src0 + ci * ROW_TILE, ROW_TILE)
                dst = pl.multiple_of(dst0 + ci * ROW_TILE, ROW_TILE)
                act(pltpu.make_async_copy(ys_hbm.at[pl.ds(src, ROW_TILE)],
                                          segbuf.at[sl, pl.ds(dst, ROW_TILE)], sems.at[sl]))
                return c2

            lax.fori_loop(0, nch_ref[tile * N_EXPERTS + e], per_chunk, 0)
            return c

        lax.fori_loop(0, N_EXPERTS, per_expert, 0)

    @pl.when(j == 0)
    def _():
        segbuf[...] = jnp.zeros_like(segbuf)
        chunks(0, 0, lambda cp: cp.start())

    @pl.when(j + 1 < n_tiles)
    def _():
        chunks(j + 1, 1 - slot, lambda cp: cp.start())

    lane_tc = lax.broadcasted_iota(jnp.int32, (tc, LANES), 1)
    off_f = off_ref[0].astype(F32)
    cols = []
    for k in range(TOP_K):
        off_k = jnp.sum(jnp.where(lane_tc == e_ref[:, k:k + 1], off_f, 0.0), axis=1, keepdims=True)
        cols.append(rank_ref[:, k:k + 1] + off_k.astype(jnp.int32))
    colb = [jnp.broadcast_to(c, (tc, LANES)) for c in cols]
    gateb = [jnp.broadcast_to(gate_ref[:, k:k + 1], (tc, LANES)) for k in range(TOP_K)]
    for p in range(SEG_ROWS // LANES):
        col_io = lane_tc + p * LANES
        g = jnp.zeros((tc, LANES), F32)
        for k in reversed(range(TOP_K)):
            g = jnp.where(col_io == colb[k], gateb[k], g)
        gmat[:, p * LANES:(p + 1) * LANES] = g.astype(BF16)

    def wait_chunk(c, carry):
        pltpu.make_async_copy(ys_hbm.at[pl.ds(0, ROW_TILE)], segbuf.at[slot, pl.ds(0, ROW_TILE)],
                              sems.at[slot]).wait()
        return carry

    lax.fori_loop(0, tot_ref[j], wait_chunk, 0)
    moe = _dot(gmat[...], segbuf[slot])

    x2 = x1_ref[...] + mod_ref[0, 5:6, :] * moe
    if final:
        ms = jnp.mean(x2 * x2, axis=-1, keepdims=True)
        x2 = x2 * lax.rsqrt(ms + RMS_EPS) * fg_ref[...]
    o_ref[...] = x2


def _combine(chunk_src, chunk_dst, n_chunks, tile_chunks, e_idx, rank, gates, col_off, x1, mod, final_g, ys,
             seq_len, tc, final):
    t_tok = x1.shape[0]
    assert SEG_ROWS >= TOP_K * tc + N_EXPERTS * 2 * (ROW_TILE - 1) and SEG_ROWS % GROUP == 0
    kern = functools.partial(_combine_kernel, tc=tc, final=final)
    tok4 = lambda: pl.BlockSpec((tc, TOP_K), lambda i, *_: (i, 0))
    return pl.pallas_call(
        kern,
        out_shape=jax.ShapeDtypeStruct((t_tok, D_MODEL), F32),
        grid_spec=pltpu.PrefetchScalarGridSpec(
            num_scalar_prefetch=4,
            grid=(t_tok // tc,),
            in_specs=[tok4(), tok4(), tok4(),
                      pl.BlockSpec((1, 1, LANES), lambda i, *_: (i, 0, 0)),
                      pl.BlockSpec((tc, D_MODEL), lambda i, *_: (i, 0)),
                      pl.BlockSpec((1, 8, D_MODEL), lambda i, *_: ((i * tc) // seq_len, 0, 0)),
                      pl.BlockSpec((1, D_MODEL), lambda i, *_: (0, 0)),
                      pl.BlockSpec(memory_space=pl.ANY)],
            out_specs=pl.BlockSpec((tc, D_MODEL), lambda i, *_: (i, 0)),
            scratch_shapes=[pltpu.VMEM((2, SEG_ROWS, D_MODEL), BF16),
                            pltpu.VMEM((tc, SEG_ROWS), BF16),
                            pltpu.SemaphoreType.DMA((2,))]),
        compiler_params=_cparams(("arbitrary",)),
        name="combine",
    )(chunk_src, chunk_dst, n_chunks, tile_chunks, e_idx, rank, gates, col_off, x1, mod, final_g, ys)


def _dft_tables(n):
    j = jnp.arange(n, dtype=jnp.int32)
    ang = ((j[:, None] * j[None, :]) % n).astype(F32) * (2.0 * math.pi / n)
    scale = 1.0 / math.sqrt(n)
    return jnp.cos(ang) * scale, jnp.sin(ang) * scale


def _block_diag(blocks):
    n, r, c = blocks.shape
    rows = lax.broadcasted_iota(jnp.int32, (n * r, n * c), 0) // r
    cols = lax.broadcasted_iota(jnp.int32, (n * r, n * c), 1) // c
    return jnp.where(rows == cols, jnp.tile(blocks.reshape(n * r, c), (1, n)), 0)


def _prep_weights(w):
    depth = w["w_in"].shape[0]
    c64, s64 = _dft_tables(HEAD_DIM)
    n_heads = GROUP // HEAD_DIM
    chan_dft = jnp.concatenate([_block_diag(jnp.stack([c64] * n_heads)),
                                _block_diag(jnp.stack([s64] * n_heads))], axis=1).astype(BF16)
    avg = _block_diag(jnp.full((n_heads, HEAD_DIM, HEAD_DIM), 1.0 / HEAD_DIM, F32)).astype(BF16)
    stacked = dict(
        exp_gate_w=w["exp_gate_w"].astype(BF16).reshape(depth * N_EXPERTS, D_MODEL, D_FF),
        exp_up_w=w["exp_up_w"].astype(BF16).reshape(depth * N_EXPERTS, D_MODEL, D_FF),
        exp_down_w=w["exp_down_w"].astype(BF16).reshape(depth * N_EXPERTS, D_FF, D_MODEL),
        exp_gate_b=w["exp_gate_b"].reshape(depth * N_EXPERTS, 1, D_FF),
        exp_up_b=w["exp_up_b"].reshape(depth * N_EXPERTS, 1, D_FF),
        exp_down_b=w["exp_down_b"].reshape(depth * N_EXPERTS, 1, D_MODEL))
    layers = []
    for l in range(depth):
        layers.append(dict(
            stacked,
            norm1_g=w["norm1_g"][l].reshape(1, D_MODEL),
            norm2_g=w["norm2_g"][l].reshape(1, D_MODEL),
            w_in=w["w_in"][l].astype(BF16),
            fourier_w=w["fourier_w"][l].astype(BF16),
            conv31_w=jnp.pad(w["conv31_w"][l], ((0, 1), (0, 0))),
            conv31_b=w["conv31_b"][l].reshape(1, GROUP),
            gn_g=w["gn_g"][l].reshape(1, GROUP),
            gn_b=w["gn_b"][l].reshape(1, GROUP),
            avg=avg,
            pw_w=w["pw_w"][l].astype(BF16),
            pool_w=_block_diag(w["pool_w"][l]).astype(BF16),
            pool_scale=w["pool_scale"][l].reshape(1, GROUP),
            conv3_w=jnp.pad(w["conv3_w"][l], ((0, 5), (0, 0))),
            sconv_out_w=w["sconv_out_w"][l].astype(BF16),
            w_o=w["w_o"][l].astype(BF16),
            router_w=jnp.pad(w["router_w"][l], ((0, 0), (0, LANES - N_EXPERTS))).astype(BF16),
            router_b=jnp.pad(w["router_b"][l], (0, LANES - N_EXPERTS)).reshape(1, LANES),
        ))
    return chan_dft, layers


def _tiles(bsz, s):
    t_seq = min(512, s)
    t_dft_m = min(s, max(128, (8 * 1024 * 1024) // (bsz * GROUP * 4)))
    t_dft_k = min(2048, s)
    return dict(seq=t_seq, dft_m=t_dft_m, dft_k=t_dft_k, tok=min(256, bsz * s))


def _moe(h2, logits, x1, mod, final_g, wl, layer, seq_len, tiles, final):
    t_tok = h2.shape[0]
    tok = tiles["tok"]
    n_assign = t_tok * TOP_K
    n_blocks = -(-(n_assign + N_EXPERTS * DISP_WIN) // MOE_BLOCK) + N_EXPERTS
    n_slots = n_blocks * MOE_BLOCK
    e_idx, rank, gates, base, tile_cnt, counts = _route(logits, tok)
    counts = counts[0, :N_EXPERTS].astype(jnp.int32)
    padded = jnp.where(counts > 0, (counts + DISP_WIN + MOE_BLOCK - 1) // MOE_BLOCK * MOE_BLOCK, 0)
    pend = jnp.cumsum(padded)
    pstart = pend - padded
    npad = padded - counts
    block_start = jnp.arange(n_blocks, dtype=jnp.int32) * MOE_BLOCK
    block_e = jnp.minimum(jnp.sum(block_start[:, None] >= pend[None, :], axis=1), N_EXPERTS - 1).astype(jnp.int32)
    n_used = (pend[-1:] // MOE_BLOCK).astype(jnp.int32)
    seg_lo = pstart[None, :] + base[:, 0, :N_EXPERTS]
    seg_n = tile_cnt[:, 0, :N_EXPERTS]
    win_lo = seg_lo - (seg_lo & (ROW_TILE - 1))
    win_rows = jnp.where(seg_n > 0, -(-(seg_lo + seg_n - win_lo) // ROW_TILE) * ROW_TILE, 0)
    win_dst = jnp.cumsum(win_rows, axis=1) - win_rows
    col_off = jnp.pad(win_dst - win_lo + pstart[None, :], ((0, 0), (0, LANES - N_EXPERTS)))
    col_off = col_off.reshape(-1, 1, LANES).astype(jnp.int32)
    flat = lambda a: a.reshape(-1).astype(jnp.int32)
    head = seg_lo & (F32_TILE - 1)
    dwin_lo = seg_lo - head
    dwin_rows = jnp.where(seg_n > 0, -(-(seg_lo + seg_n - dwin_lo) // F32_TILE) * F32_TILE, 0)
    dwin_src = jnp.cumsum(dwin_rows, axis=1) - dwin_rows
    dcol_off = jnp.pad(dwin_src - dwin_lo + pstart[None, :], ((0, 0), (0, LANES - N_EXPERTS)))
    dcol_off = dcol_off.reshape(-1, 1, LANES).astype(jnp.int32)
    d_chunks = dwin_rows // F32_TILE
    xs = _dispatch(flat(dwin_src), flat(dwin_lo), flat(d_chunks), flat(jnp.where(seg_n > 0, head, 0)),
                   flat(jnp.sum(d_chunks > 0, axis=1)),
                   flat(jnp.sum(jnp.maximum(d_chunks - DISP_WIN // F32_TILE, 0), axis=1)),
                   pend.astype(jnp.int32), npad.astype(jnp.int32), e_idx, rank, dcol_off, h2, n_slots, tok)
    ys = _experts(block_e, n_used, xs, wl, layer)
    return _combine(flat(win_lo), flat(win_dst), flat(win_rows // ROW_TILE),
                    flat(jnp.sum(win_rows // ROW_TILE, axis=1)), e_idx, rank, gates, col_off,
                    x1, mod, final_g, ys, seq_len, tok, final)


def _encoder(x, c, w, chan_dft, layers, tables, tiles=None):
    bsz, s, _ = x.shape
    tiles = tiles or _tiles(bsz, s)
    cos_t, nsin_t = tables
    depth = len(layers)
    final_g = w["final_g"].reshape(1, D_MODEL)
    ada_b = w["ada_b"].reshape(depth, 1, 6 * D_MODEL)
    for l, wl in enumerate(layers):
        mod = _ada(c, w["ada_w"], ada_b, l).reshape(bsz, 6, D_MODEL)
        mod = jnp.pad(mod, ((0, 0), (0, 2), (0, 0)))
        p, u = _inproj(x, mod, wl["norm1_g"], wl["w_in"], chan_dft, tiles["seq"])
        y = _seqdft(cos_t, nsin_t, p, tiles["dft_m"], tiles["dft_k"])
        x1, h2, logits = _mix(y, u, x, mod, wl, tiles["seq"])
        x = _moe(h2.reshape(bsz * s, D_MODEL), logits.reshape(bsz * s, LANES),
                 x1.reshape(bsz * s, D_MODEL), mod, final_g, wl, l, s, tiles,
                 final=(l == depth - 1)).reshape(bsz, s, D_MODEL)
    return x


def _seq_tables(s):
    k = jnp.arange(s, dtype=jnp.int32)[None, :]

    def cos_sin(j):
        ang = ((j[:, None] * k) % s).astype(F32) * (2.0 * math.pi / s)
        return jnp.cos(ang), jnp.sin(ang)

    ca, sa = cos_sin(jnp.arange(s // LANES, dtype=jnp.int32) * LANES)
    cb, sb = cos_sin(jnp.arange(LANES, dtype=jnp.int32))
    scale = 1.0 / math.sqrt(s)
    cos_t = (ca[:, None, :] * cb[None, :, :] - sa[:, None, :] * sb[None, :, :]) * scale
    nsin_t = (sa[:, None, :] * cb[None, :, :] + ca[:, None, :] * sb[None, :, :]) * (-scale)
    return cos_t.reshape(s, s).astype(BF16), nsin_t.reshape(s, s).astype(BF16)


def kernel(x_prompt, x_sample, c_prompt, c_sample, norm1_g, norm2_g, ada_w, ada_b, w_in, fourier_w, conv31_w, conv31_b, gn_g, gn_b, pw_w, pool_w, pool_scale, conv3_w, sconv_out_w, w_o, router_w, router_b, exp_gate_w, exp_gate_b, exp_up_w, exp_up_b, exp_down_w, exp_down_b, final_g):
    w = dict(norm1_g=norm1_g, norm2_g=norm2_g, ada_w=ada_w, ada_b=ada_b, w_in=w_in, fourier_w=fourier_w,
             conv31_w=conv31_w, conv31_b=conv31_b, gn_g=gn_g, gn_b=gn_b, pw_w=pw_w, pool_w=pool_w,
             pool_scale=pool_scale, conv3_w=conv3_w, sconv_out_w=sconv_out_w, w_o=w_o, router_w=router_w,
             router_b=router_b, exp_gate_w=exp_gate_w, exp_gate_b=exp_gate_b, exp_up_w=exp_up_w,
             exp_up_b=exp_up_b, exp_down_w=exp_down_w, exp_down_b=exp_down_b, final_g=final_g)
    chan_dft, layers = _prep_weights(w)
    y_prompt = _encoder(x_prompt, c_prompt, w, chan_dft, layers, _seq_tables(x_prompt.shape[1]))
    y_sample = _encoder(x_sample, c_sample, w, chan_dft, layers, _seq_tables(x_sample.shape[1]))
    return (y_prompt, y_sample)
```

```python
import functools
import math

import jax
import jax.numpy as jnp
from jax import lax
from jax.experimental import pallas as pl
from jax.experimental.pallas import tpu as pltpu

D_MODEL = 1024
GROUP = 256
HEAD_DIM = 64
CONV31 = 31
POOL_WINDOWS = (2, 4, 8, 16)
N_EXPERTS = 32
TOP_K = 4
D_FF = 1024
SWIGLU_ALPHA = 1.702
SWIGLU_LIMIT = 7.0
MOE_BLOCK = 512
RMS_EPS = 1e-6
GN_EPS = 1e-5
HALO = 16
LANES = 128
VMEM_LIMIT = 48 * 1024 * 1024

F32 = jnp.float32
BF16 = jnp.bfloat16


def _cparams(sem):
    return pltpu.CompilerParams(dimension_semantics=sem, vmem_limit_bytes=VMEM_LIMIT)


def _dot(a, b):
    return jnp.dot(a, b, preferred_element_type=F32)


def _split_bf16(x):
    hi = x.astype(BF16)
    lo = (x - hi.astype(F32)).astype(BF16)
    return hi, lo


def _dot_hilo(x, w_bf16):
    hi, lo = _split_bf16(x)
    return _dot(hi, w_bf16) + _dot(lo, w_bf16)


def _sigmoid(x):
    return 1.0 / (1.0 + jnp.exp(-x))


def _ada_kernel(c_ref, w_ref, b_ref, o_ref):
    c = c_ref[...]
    cs = c * _sigmoid(c)
    c_hi, c_lo = _split_bf16(cs)
    w_hi, w_lo = _split_bf16(w_ref[0])
    o_ref[...] = _dot(c_hi, w_hi) + _dot(c_lo, w_hi) + _dot(c_hi, w_lo) + b_ref[0]


def _ada(c, ada_w, ada_b, layer):
    bsz = c.shape[0]
    n_out = ada_w.shape[2]
    return pl.pallas_call(
        _ada_kernel,
        out_shape=jax.ShapeDtypeStruct((bsz, n_out), F32),
        grid=(n_out // D_MODEL,),
        in_specs=[
            pl.BlockSpec((bsz, D_MODEL), lambda j: (0, 0)),
            pl.BlockSpec((1, D_MODEL, D_MODEL), lambda j: (layer, 0, j)),
            pl.BlockSpec((1, 1, D_MODEL), lambda j: (layer, 0, j)),
        ],
        out_specs=pl.BlockSpec((bsz, D_MODEL), lambda j: (0, j)),
        compiler_params=_cparams(("arbitrary",)),
        name="ada",
    )(c, ada_w, ada_b)


def _rms_mod(x, g, scale, shift):
    ms = jnp.mean(x * x, axis=-1, keepdims=True)
    y = x * lax.rsqrt(ms + RMS_EPS) * g
    return y * (1.0 + scale) + shift


def _inproj_kernel(x_ref, mod_ref, g_ref, w_ref, cs_ref, p_ref, u_ref):
    x = x_ref[0]
    h = _rms_mod(x, g_ref[...], mod_ref[0, 1:2, :], mod_ref[0, 0:1, :]).astype(BF16)
    a = _dot(h, w_ref[:, 0:GROUP]).astype(BF16)
    p_ref[0] = _dot(a, cs_ref[...]).astype(BF16)
    for j in range(6):
        u_ref[0, :, j * GROUP:(j + 1) * GROUP] = _dot(
            h, w_ref[:, (j + 1) * GROUP:(j + 2) * GROUP]).astype(BF16)


def _inproj(x, mod, g, w_in, chan_dft, tm):
    bsz, s, _ = x.shape
    return pl.pallas_call(
        _inproj_kernel,
        out_shape=(jax.ShapeDtypeStruct((bsz, s, 2 * GROUP), BF16),
                   jax.ShapeDtypeStruct((bsz, s, 6 * GROUP), BF16)),
        grid=(bsz, s // tm),
        in_specs=[
            pl.BlockSpec((1, tm, D_MODEL), lambda b, i: (b, i, 0)),
            pl.BlockSpec((1, 8, D_MODEL), lambda b, i: (b, 0, 0)),
            pl.BlockSpec((1, D_MODEL), lambda b, i: (0, 0)),
            pl.BlockSpec((D_MODEL, 7 * GROUP), lambda b, i: (0, 0)),
            pl.BlockSpec((GROUP, 2 * GROUP), lambda b, i: (0, 0)),
        ],
        out_specs=(pl.BlockSpec((1, tm, 2 * GROUP), lambda b, i: (b, i, 0)),
                   pl.BlockSpec((1, tm, 6 * GROUP), lambda b, i: (b, i, 0))),
        compiler_params=_cparams(("arbitrary", "arbitrary")),
        name="inproj",
    )(x, mod, g, w_in, chan_dft)


def _seqdft_kernel(c_ref, s_ref, p_ref, o_ref, acc_ref):
    k = pl.program_id(1)
    b = pl.program_id(2)
    contrib = _dot(c_ref[...], p_ref[0, :, 0:GROUP]) + _dot(s_ref[...], p_ref[0, :, GROUP:2 * GROUP])

    @pl.when(k == 0)
    def _():
        acc_ref[b] = contrib

    @pl.when(k > 0)
    def _():
        acc_ref[b] = acc_ref[b] + contrib

    @pl.when(k == pl.num_programs(1) - 1)
    def _():
        o_ref[b] = acc_ref[b].astype(BF16)


def _seqdft(cos_t, nsin_t, p, tm, tk):
    bsz, s, _ = p.shape
    return pl.pallas_call(
        _seqdft_kernel,
        out_shape=jax.ShapeDtypeStruct((bsz, s, GROUP), BF16),
        grid=(s // tm, s // tk, bsz),
        in_specs=[
            pl.BlockSpec((tm, tk), lambda i, k, b: (i, k)),
            pl.BlockSpec((tm, tk), lambda i, k, b: (i, k)),
            pl.BlockSpec((1, tk, 2 * GROUP), lambda i, k, b: (b, k, 0)),
        ],
        out_specs=pl.BlockSpec((bsz, tm, GROUP), lambda i, k, b: (0, i, 0)),
        scratch_shapes=[pltpu.VMEM((bsz, tm, GROUP), F32)],
        compiler_params=_cparams(("arbitrary", "arbitrary", "arbitrary")),
        name="seqdft",
    )(cos_t, nsin_t, p)


U_V, U_G, U_P, U_BG, U_CG, U_XV = (j * GROUP for j in range(6))


def _mix_kernel(y_ref, u_ref, up_ref, un_ref, x_ref, mod_ref,
                fw_ref, c31w_ref, c31b_ref, gng_ref, gnb_ref, avg_ref, pww_ref,
                poolw_ref, pscale_ref, c3w_ref, sow_ref, wo_ref, n2g_ref, rw_ref, rb_ref,
                x1_ref, h2_ref, lg_ref,
                zext, zsh, pext, s2, s4, s8, s16, qext, cat, *, tq, seq_len):
    i = pl.program_id(1)
    has_prev = (i > 0).astype(F32)
    has_next = (i < pl.num_programs(1) - 1).astype(F32)
    rows = tq + 2 * HALO

    def cols(ref, c0):
        return ref[0, :, c0:c0 + GROUP].astype(F32)

    def fill_ext(dst, fn):
        dst[0:HALO, :] = fn(up_ref) * has_prev
        dst[HALO:HALO + tq, :] = fn(u_ref)
        dst[HALO + tq:rows, :] = fn(un_ref) * has_next

    cat[:, 0:GROUP] = _dot(y_ref[0], fw_ref[...]).astype(BF16)

    fill_ext(zext, lambda r: cols(r, U_V) * _sigmoid(cols(r, U_G)))
    conv = jnp.broadcast_to(c31b_ref[...], (tq, GROUP))
    first = HALO - CONV31 // 2
    for res in range(8):
        taps = [j for j in range(CONV31) if (first + j) % 8 == res]
        span = (first + taps[-1]) // 8 * 8 + tq
        zsh[0:span, :] = zext[pl.ds(res, span), :]
        for j in taps:
            q = (first + j) // 8 * 8
            conv = conv + c31w_ref[j:j + 1, :] * zsh[q:q + tq, :]
    mu = _dot_hilo(conv, avg_ref[...])
    dev = conv - mu
    var = _dot_hilo(dev * dev, avg_ref[...])
    zn = dev * lax.rsqrt(var + GN_EPS) * gng_ref[...] + gnb_ref[...]
    cat[:, GROUP:2 * GROUP] = _dot((zn * _sigmoid(zn)).astype(BF16), pww_ref[...]).astype(BF16)

    fill_ext(pext, lambda r: cols(r, U_P))
    n = rows - 8
    zeros8 = jnp.zeros((8, GROUP), F32)
    s2[0:n, :] = pext[0:n, :] + pext[pl.ds(1, n), :]
    s2[n:rows, :] = zeros8
    s4[0:n, :] = s2[0:n, :] + s2[pl.ds(2, n), :]
    s4[n:rows, :] = zeros8
    s8[0:n, :] = s4[0:n, :] + s4[pl.ds(4, n), :]
    s8[n:rows, :] = zeros8
    s16[0:n, :] = s8[0:n, :] + s8[pl.ds(8, n), :]
    lane = lax.broadcasted_iota(jnp.int32, (tq, GROUP), 1)
    pos = lax.broadcasted_iota(jnp.int32, (tq, GROUP), 0) + i * tq
    win = jnp.where(lane < HEAD_DIM, s2[pl.ds(HALO - 1, tq), :],
                    jnp.where(lane < 2 * HEAD_DIM, s4[pl.ds(HALO - 2, tq), :],
                              jnp.where(lane < 3 * HEAD_DIM, s8[pl.ds(HALO - 4, tq), :],
                                        s16[pl.ds(HALO - 8, tq), :])))
    half = jnp.where(lane < HEAD_DIM, 1,
                     jnp.where(lane < 2 * HEAD_DIM, 2, jnp.where(lane < 3 * HEAD_DIM, 4, 8)))
    cnt = jnp.minimum(pos + half, seq_len) - jnp.maximum(pos - half, 0)
    dpool = win / cnt.astype(F32) - pext[HALO:HALO + tq, :]
    cat[:, 2 * GROUP:3 * GROUP] = (_dot(dpool.astype(BF16), poolw_ref[...]) * pscale_ref[...]).astype(BF16)

    fill_ext(qext, lambda r: cols(r, U_CG) * cols(r, U_XV))
    c3 = (c3w_ref[0:1, :] * qext[pl.ds(HALO - 1, tq), :]
          + c3w_ref[1:2, :] * qext[HALO:HALO + tq, :]
          + c3w_ref[2:3, :] * qext[pl.ds(HALO + 1, tq), :])
    cat[:, 3 * GROUP:4 * GROUP] = _dot((cols(u_ref, U_BG) * c3).astype(BF16), sow_ref[...]).astype(BF16)

    mixed = _dot(cat[...], wo_ref[...])
    x1 = x_ref[0] + mod_ref[0, 2:3, :] * mixed
    x1_ref[0] = x1
    h2 = _rms_mod(x1, n2g_ref[...], mod_ref[0, 4:5, :], mod_ref[0, 3:4, :])
    h2_ref[0] = h2
    lg_ref[0] = _dot(h2.astype(BF16), rw_ref[...]) + rb_ref[...]


def _mix(y, u, x, mod, wl, tq):
    bsz, s, _ = x.shape
    rows = tq + 2 * HALO
    hb = tq // HALO
    n_hblk = s // HALO
    full = lambda shape: pl.BlockSpec(shape, lambda b, i: tuple(0 for _ in shape))
    kern = functools.partial(_mix_kernel, tq=tq, seq_len=s)
    return pl.pallas_call(
        kern,
        out_shape=(jax.ShapeDtypeStruct((bsz, s, D_MODEL), F32),
                   jax.ShapeDtypeStruct((bsz, s, D_MODEL), F32),
                   jax.ShapeDtypeStruct((bsz, s, LANES), F32)),
        grid=(bsz, s // tq),
        in_specs=[
            pl.BlockSpec((1, tq, GROUP), lambda b, i: (b, i, 0)),
            pl.BlockSpec((1, tq, 6 * GROUP), lambda b, i: (b, i, 0)),
            pl.BlockSpec((1, HALO, 6 * GROUP), lambda b, i: (b, jnp.maximum(i * hb - 1, 0), 0)),
            pl.BlockSpec((1, HALO, 6 * GROUP), lambda b, i: (b, jnp.minimum((i + 1) * hb, n_hblk - 1), 0)),
            pl.BlockSpec((1, tq, D_MODEL), lambda b, i: (b, i, 0)),
            pl.BlockSpec((1, 8, D_MODEL), lambda b, i: (b, 0, 0)),
            full((GROUP, GROUP)),
            full((32, GROUP)),
            full((1, GROUP)),
            full((1, GROUP)),
            full((1, GROUP)),
            full((GROUP, GROUP)),
            full((GROUP, GROUP)),
            full((GROUP, GROUP)),
            full((1, GROUP)),
            full((8, GROUP)),
            full((GROUP, GROUP)),
            full((D_MODEL, D_MODEL)),
            full((1, D_MODEL)),
            full((D_MODEL, LANES)),
            full((1, LANES)),
        ],
        out_specs=(pl.BlockSpec((1, tq, D_MODEL), lambda b, i: (b, i, 0)),
                   pl.BlockSpec((1, tq, D_MODEL), lambda b, i: (b, i, 0)),
                   pl.BlockSpec((1, tq, LANES), lambda b, i: (b, i, 0))),
        scratch_shapes=[pltpu.VMEM((rows, GROUP), F32) for _ in range(8)]
        + [pltpu.VMEM((tq, D_MODEL), BF16)],
        compiler_params=_cparams(("arbitrary", "arbitrary")),
        name="mix",
    )(y, u, u, u, x, mod, wl["fourier_w"], wl["conv31_w"], wl["conv31_b"], wl["gn_g"], wl["gn_b"],
      wl["avg"], wl["pw_w"], wl["pool_w"], wl["pool_scale"], wl["conv3_w"], wl["sconv_out_w"],
      wl["w_o"], wl["norm2_g"], wl["router_w"], wl["router_b"])


def _route_kernel(lg_ref, e_ref, rank_ref, gate_ref, base_ref, tcnt_ref, cnt_ref, carry, *, tr, sub):
    i = pl.program_id(0)

    @pl.when(i == 0)
    def _():
        carry[...] = jnp.zeros_like(carry)

    lane = lax.broadcasted_iota(jnp.int32, (tr, LANES), 1).astype(F32)
    neg = jnp.float32(-jnp.inf)
    work = jnp.where(lane < N_EXPERTS, lg_ref[...], neg)
    vals, idxs, hots = [], [], []
    for _ in range(TOP_K):
        m = jnp.max(work, axis=1, keepdims=True)
        idx = jnp.min(jnp.where(work == m, lane, float(LANES)), axis=1, keepdims=True)
        hot = lane == idx
        vals.append(m)
        idxs.append(idx.astype(jnp.int32))
        hots.append(hot)
        work = jnp.where(hot, neg, work)
    exps = [jnp.exp(v - vals[0]) for v in vals]
    denom = exps[0] + exps[1] + exps[2] + exps[3]
    member = sum(h.astype(F32) for h in hots)
    r_io = lax.broadcasted_iota(jnp.int32, (tr, tr), 0)
    c_io = lax.broadcasted_iota(jnp.int32, (tr, tr), 1)
    lower = (c_io < r_io).astype(BF16)
    before = _dot(lower, member.astype(BF16)) + carry[...]
    col = lax.broadcasted_iota(jnp.int32, (tr, TOP_K), 1)
    e_out = jnp.zeros((tr, TOP_K), jnp.int32)
    r_out = jnp.zeros((tr, TOP_K), jnp.int32)
    g_out = jnp.zeros((tr, TOP_K), F32)
    for k in range(TOP_K):
        rk = jnp.sum(jnp.where(hots[k], before, 0.0), axis=1, keepdims=True).astype(jnp.int32)
        e_out = jnp.where(col == k, idxs[k], e_out)
        r_out = jnp.where(col == k, rk, r_out)
        g_out = jnp.where(col == k, exps[k] / denom, g_out)
    e_ref[...] = e_out
    rank_ref[...] = r_out
    gate_ref[...] = g_out
    running = carry[...]
    for q in range(tr // sub):
        cnt_q = jnp.sum(member[q * sub:(q + 1) * sub, :], axis=0, keepdims=True)
        base_ref[q] = running.astype(jnp.int32)
        tcnt_ref[q] = cnt_q.astype(jnp.int32)
        running = running + cnt_q
    carry[...] = running
    cnt_ref[...] = running


def _route(logits, sub):
    t_tok = logits.shape[0]
    tr = min(2 * sub, t_tok)
    n_steps = t_tok // tr
    n_tiles = t_tok // sub
    kern = functools.partial(_route_kernel, tr=tr, sub=sub)
    tile = lambda w: pl.BlockSpec((tr, w), lambda i: (i, 0))
    per_tile = pl.BlockSpec((tr // sub, 1, LANES), lambda i: (i, 0, 0))
    return pl.pallas_call(
        kern,
        out_shape=(jax.ShapeDtypeStruct((t_tok, TOP_K), jnp.int32),
                   jax.ShapeDtypeStruct((t_tok, TOP_K), jnp.int32),
                   jax.ShapeDtypeStruct((t_tok, TOP_K), F32),
                   jax.ShapeDtypeStruct((n_tiles, 1, LANES), jnp.int32),
                   jax.ShapeDtypeStruct((n_tiles, 1, LANES), jnp.int32),
                   jax.ShapeDtypeStruct((1, LANES), F32)),
        grid=(n_steps,),
        in_specs=[tile(LANES)],
        out_specs=(tile(TOP_K), tile(TOP_K), tile(TOP_K), per_tile, per_tile,
                   pl.BlockSpec((1, LANES), lambda i: (0, 0))),
        scratch_shapes=[pltpu.VMEM((1, LANES), F32)],
        compiler_params=_cparams(("arbitrary",)),
        name="route",
    )(logits)


F32_TILE = 8
DISP_ROWS = TOP_K * 256 + N_EXPERTS * 2 * F32_TILE
CARRY_ROWS = N_EXPERTS * F32_TILE
DISP_WIN = 64
XS_COLS = D_MODEL // 2


def _pack_pairs(x):
    lo = lax.bitcast_convert_type(x[:, :XS_COLS], jnp.uint32)
    hi = lax.bitcast_convert_type(x[:, XS_COLS:], jnp.uint32)
    return (hi & jnp.uint32(0xFFFF0000)) | (lo >> 16)


def _unpack_pairs(w):
    lo = lax.bitcast_convert_type(w << 16, F32).astype(BF16)
    hi = lax.bitcast_convert_type(w & jnp.uint32(0xFFFF0000), F32).astype(BF16)
    return lo, hi


def _dispatch_kernel(src_ref, dst_ref, nch_ref, head_ref, nwin_ref, nextra_ref, used_ref, pend_ref, npad_ref,
                     e_ref, rank_ref, off_ref, h_ref, xs_hbm, outbuf, carry, zeros, sems, sem_z, *, td):
    i = pl.program_id(0)
    n_steps = pl.num_programs(0)
    slot = i % 2

    def zero_blk(row):
        return pltpu.make_async_copy(zeros, xs_hbm.at[pl.ds(pl.multiple_of(row, MOE_BLOCK), MOE_BLOCK)], sem_z)

    @pl.when(i == 0)
    def _():
        zeros[...] = jnp.zeros_like(zeros)
        carry[...] = jnp.zeros_like(carry)
        outbuf[...] = jnp.zeros_like(outbuf)
        first_unused = pend_ref[N_EXPERTS - 1] // MOE_BLOCK
        n_blocks = xs_hbm.shape[0] // MOE_BLOCK

        def start(e, c):
            @pl.when(npad_ref[e] > 0)
            def _():
                zero_blk(pend_ref[e] - MOE_BLOCK).start()

            @pl.when(npad_ref[e] > MOE_BLOCK)
            def _():
                zero_blk(pend_ref[e] - 2 * MOE_BLOCK).start()
            return c

        def wait(e, c):
            @pl.when(npad_ref[e] > 0)
            def _():
                zero_blk(pend_ref[e] - MOE_BLOCK).wait()

            @pl.when(npad_ref[e] > MOE_BLOCK)
            def _():
                zero_blk(pend_ref[e] - 2 * MOE_BLOCK).wait()
            return c

        def start_tail(b, c):
            zero_blk(b * MOE_BLOCK).start()
            return c

        def wait_tail(b, c):
            zero_blk(b * MOE_BLOCK).wait()
            return c

        lax.fori_loop(0, N_EXPERTS, start, 0)
        lax.fori_loop(first_unused, n_blocks, start_tail, 0)
        lax.fori_loop(0, N_EXPERTS, wait, 0)
        lax.fori_loop(first_unused, n_blocks, wait_tail, 0)

    def window_copies(tile, sl, act):
        for e in range(N_EXPERTS):
            n_ch = nch_ref[tile * N_EXPERTS + e]
            src0 = pl.multiple_of(src_ref[tile * N_EXPERTS + e], F32_TILE)
            dst0 = pl.multiple_of(dst_ref[tile * N_EXPERTS + e], F32_TILE)

            @pl.when(n_ch > 0)
            def _():
                act(pltpu.make_async_copy(outbuf.at[sl, pl.ds(src0, DISP_WIN)],
                                          xs_hbm.at[pl.ds(dst0, DISP_WIN)], sems.at[sl]))

            def per_chunk(ci, c2):
                src = pl.multiple_of(src0 + ci * F32_TILE, F32_TILE)
                dst = pl.multiple_of(dst0 + ci * F32_TILE, F32_TILE)
                act(pltpu.make_async_copy(outbuf.at[sl, pl.ds(src, F32_TILE)],
                                          xs_hbm.at[pl.ds(dst, F32_TILE)], sems.at[sl]))
                return c2

            lax.fori_loop(DISP_WIN // F32_TILE, n_ch, per_chunk, 0)

    lane_td = lax.broadcasted_iota(jnp.int32, (td, LANES), 1)
    off_f = off_ref[0].astype(F32)
    pos_cols = jnp.full((td, LANES), -1.0, F32)
    for k in range(TOP_K):
        off_k = jnp.sum(jnp.where(lane_td == e_ref[:, k:k + 1], off_f, 0.0), axis=1, keepdims=True)
        pos_cols = jnp.where(lane_td == k, rank_ref[:, k:k + 1].astype(F32) + off_k, pos_cols)
    pos_t = jnp.transpose(pos_cols).astype(jnp.int32)
    pos = [jnp.broadcast_to(pos_t[k:k + 1, :], (GROUP, td)) for k in range(TOP_K)]
    row_io = lax.broadcasted_iota(jnp.int32, (GROUP, td), 0)
    h_bf = h_ref[...].astype(BF16)
    def select_rows(rc):
        r = row_io + rc * GROUP
        tok = jnp.zeros((GROUP, td), F32)
        for k in range(TOP_K):
            tok = jnp.where(r == pos[k], 1.0, tok)
        outbuf[slot, rc * GROUP:(rc + 1) * GROUP, :] = _pack_pairs(_dot(tok.astype(BF16), h_bf))

    for rc in range(DISP_ROWS // GROUP):
        if rc * GROUP < TOP_K * td:
            select_rows(rc)
        else:
            pl.when(used_ref[i] > rc * GROUP)(functools.partial(select_rows, rc))

    sub_io = lax.broadcasted_iota(jnp.int32, (F32_TILE, XS_COLS), 0)
    for e in range(N_EXPERTS):
        es = slice(e * F32_TILE, (e + 1) * F32_TILE)
        n_ch = nch_ref[i * N_EXPERTS + e]
        first = pl.multiple_of(src_ref[i * N_EXPERTS + e], F32_TILE)
        outbuf[slot, pl.ds(first, F32_TILE), :] = jnp.where(
            sub_io < head_ref[i * N_EXPERTS + e], carry[es, :], outbuf[slot, pl.ds(first, F32_TILE), :])
        last = pl.multiple_of(first + jnp.maximum(n_ch - 1, 0) * F32_TILE, F32_TILE)
        carry[es, :] = jnp.where(n_ch > 0, outbuf[slot, pl.ds(last, F32_TILE), :], carry[es, :])

    def wait_copies(tile, sl):
        def wait_window(c, carry_):
            pltpu.make_async_copy(outbuf.at[sl, pl.ds(0, DISP_WIN)], xs_hbm.at[pl.ds(0, DISP_WIN)],
                                  sems.at[sl]).wait()
            return carry_

        def wait_chunk(c, carry_):
            pltpu.make_async_copy(outbuf.at[sl, pl.ds(0, F32_TILE)], xs_hbm.at[pl.ds(0, F32_TILE)],
                                  sems.at[sl]).wait()
            return carry_

        lax.fori_loop(0, nwin_ref[tile], wait_window, 0)
        lax.fori_loop(0, nextra_ref[tile], wait_chunk, 0)

    @pl.when(i > 0)
    def _():
        wait_copies(i - 1, 1 - slot)

    window_copies(i, slot, lambda cp: cp.start())

    @pl.when(i == n_steps - 1)
    def _():
        wait_copies(i, slot)


def _dispatch(chunk_src, chunk_dst, n_chunks, head, n_windows, n_extra, used_rows, pend, npad, e_idx, rank,
              col_off, h2, n_slots, td):
    t_tok = h2.shape[0]
    assert DISP_ROWS >= TOP_K * td + N_EXPERTS * 2 * (F32_TILE - 1) + F32_TILE and DISP_ROWS % GROUP == 0
    kern = functools.partial(_dispatch_kernel, td=td)
    tok4 = lambda: pl.BlockSpec((td, TOP_K), lambda i, *_: (i, 0))
    return pl.pallas_call(
        kern,
        out_shape=jax.ShapeDtypeStruct((n_slots, XS_COLS), jnp.uint32),
        grid_spec=pltpu.PrefetchScalarGridSpec(
            num_scalar_prefetch=9,
            grid=(t_tok // td,),
            in_specs=[tok4(), tok4(),
                      pl.BlockSpec((1, 1, LANES), lambda i, *_: (i, 0, 0)),
                      pl.BlockSpec((td, D_MODEL), lambda i, *_: (i, 0))],
            out_specs=pl.BlockSpec(memory_space=pl.ANY),
            scratch_shapes=[pltpu.VMEM((2, DISP_ROWS + DISP_WIN, XS_COLS), jnp.uint32),
                            pltpu.VMEM((CARRY_ROWS, XS_COLS), jnp.uint32),
                            pltpu.VMEM((MOE_BLOCK, XS_COLS), jnp.uint32),
                            pltpu.SemaphoreType.DMA((2,)),
                            pltpu.SemaphoreType.DMA]),
        compiler_params=_cparams(("arbitrary",)),
        name="dispatch",
    )(chunk_src, chunk_dst, n_chunks, head, n_windows, n_extra, used_rows, pend, npad, e_idx, rank, col_off, h2)


def _experts_kernel(be_ref, nu_ref, xs_ref, wg_ref, bg_ref, wu_ref, bu_ref, wd_ref, bd_ref, o_ref, act):
    i = pl.program_id(0)

    @pl.when(i < nu_ref[0])
    def _():
        x_lo, x_hi = _unpack_pairs(xs_ref[...])

        def proj(w_ref, fs):
            return _dot(x_lo, w_ref[0, :XS_COLS, fs]) + _dot(x_hi, w_ref[0, XS_COLS:, fs])

        for f in range(D_FF // GROUP):
            fs = slice(f * GROUP, (f + 1) * GROUP)
            gt = jnp.minimum(proj(wg_ref, fs) + bg_ref[0, :, fs], SWIGLU_LIMIT)
            up = jnp.clip(proj(wu_ref, fs) + bu_ref[0, :, fs], -SWIGLU_LIMIT, SWIGLU_LIMIT)
            act[:, fs] = ((up + 1.0) * (gt * _sigmoid(SWIGLU_ALPHA * gt))).astype(BF16)
        o_ref[...] = (_dot(act[...], wd_ref[0]) + bd_ref[0]).astype(BF16)

    @pl.when(i >= nu_ref[0])
    def _():
        o_ref[...] = jnp.zeros_like(o_ref)


def _experts(block_e, n_used, xs, wl, layer):
    n_slots = xs.shape[0]
    n_blocks = n_slots // MOE_BLOCK

    def expert(i, be, nu):
        return layer * N_EXPERTS + be[jnp.minimum(i, nu[0] - 1)]

    wspec = lambda: pl.BlockSpec((1, D_MODEL, D_FF), lambda i, be, nu: (expert(i, be, nu), 0, 0))
    bspec = lambda: pl.BlockSpec((1, 1, D_FF), lambda i, be, nu: (expert(i, be, nu), 0, 0))
    return pl.pallas_call(
        _experts_kernel,
        out_shape=jax.ShapeDtypeStruct((n_slots, D_MODEL), BF16),
        grid_spec=pltpu.PrefetchScalarGridSpec(
            num_scalar_prefetch=2,
            grid=(n_blocks,),
            in_specs=[pl.BlockSpec((MOE_BLOCK, XS_COLS), lambda i, be, nu: (jnp.minimum(i, nu[0] - 1), 0)),
                      wspec(), bspec(), wspec(), bspec(), wspec(), bspec()],
            out_specs=pl.BlockSpec((MOE_BLOCK, D_MODEL), lambda i, be, nu: (i, 0)),
            scratch_shapes=[pltpu.VMEM((MOE_BLOCK, D_FF), BF16)]),
        compiler_params=_cparams(("arbitrary",)),
        name="experts",
    )(block_e, n_used, xs, wl["exp_gate_w"], wl["exp_gate_b"], wl["exp_up_w"], wl["exp_up_b"],
      wl["exp_down_w"], wl["exp_down_b"])


ROW_TILE = 16
SEG_ROWS = TOP_K * 256 + N_EXPERTS * 2 * ROW_TILE
K_CHUNK = 512


def _combine_kernel(src_ref, dst_ref, nch_ref, tot_ref, e_ref, rank_ref, gate_ref, off_ref, x1_ref, mod_ref,
                    fg_ref, ys_hbm, o_ref, segbuf, gmat, acc, sems, *, tc, final):
    j = pl.program_id(0)
    n_tiles = pl.num_programs(0)
    slot = j % 2

    def chunks(tile, sl, act):
        def per_expert(e, c):
            src0 = src_ref[tile * N_EXPERTS + e]
            dst0 = dst_ref[tile * N_EXPERTS + e]

            def per_chunk(ci, c2):
                src = pl.multiple_of(src0 + ci * ROW_TILE, ROW_TILE)
                dst = pl.multiple_of(dst0 + ci * ROW_TILE, ROW_TILE)
                act(pltpu.make_async_copy(ys_hbm.at[pl.ds(src, ROW_TILE)],
                                          segbuf.at[sl, pl.ds(dst, ROW_TILE)], sems.at[sl]))
                return c2

            lax.fori_loop(0, nch_ref[tile * N_EXPERTS + e], per_chunk, 0)
            return c

        lax.fori_loop(0, N_EXPERTS, per_expert, 0)

    @pl.when(j == 0)
    def _():
        segbuf[...] = jnp.zeros_like(segbuf)
        chunks(0, 0, lambda cp: cp.start())

    @pl.when(j + 1 < n_tiles)
    def _():
        chunks(j + 1, 1 - slot, lambda cp: cp.start())

    lane_tc = lax.broadcasted_iota(jnp.int32, (tc, LANES), 1)
    off_f = off_ref[0].astype(F32)
    cols = []
    for k in range(TOP_K):
        off_k = jnp.sum(jnp.where(lane_tc == e_ref[:, k:k + 1], off_f, 0.0), axis=1, keepdims=True)
        cols.append(rank_ref[:, k:k + 1] + off_k.astype(jnp.int32))
    colb = [jnp.broadcast_to(c, (tc, LANES)) for c in cols]
    gateb = [jnp.broadcast_to(gate_ref[:, k:k + 1], (tc, LANES)) for k in range(TOP_K)]

    def build(p0, p1):
        for p in range(p0, p1):
            col_io = lane_tc + p * LANES
            g = jnp.zeros((tc, LANES), F32)
            for k in reversed(range(TOP_K)):
                g = jnp.where(col_io == colb[k], gateb[k], g)
            gmat[:, p * LANES:(p + 1) * LANES] = g.astype(BF16)

    def wait_chunk(c, carry):
        pltpu.make_async_copy(ys_hbm.at[pl.ds(0, ROW_TILE)], segbuf.at[slot, pl.ds(0, ROW_TILE)],
                              sems.at[slot]).wait()
        return carry

    always = TOP_K * tc
    build(0, always // LANES)
    lax.fori_loop(0, tot_ref[j], wait_chunk, 0)
    acc[...] = _dot(gmat[:, 0:always], segbuf[slot, 0:always, :])
    used = tot_ref[j] * ROW_TILE
    for c0 in range(always, SEG_ROWS, K_CHUNK):
        @pl.when(used > c0)
        def _():
            build(c0 // LANES, (c0 + K_CHUNK) // LANES)
            acc[...] = acc[...] + _dot(gmat[:, c0:c0 + K_CHUNK], segbuf[slot, c0:c0 + K_CHUNK, :])
    moe = acc[...]

    x2 = x1_ref[...] + mod_ref[0, 5:6, :] * moe
    if final:
        ms = jnp.mean(x2 * x2, axis=-1, keepdims=True)
        x2 = x2 * lax.rsqrt(ms + RMS_EPS) * fg_ref[...]
    o_ref[...] = x2


def _combine(chunk_src, chunk_dst, n_chunks, tile_chunks, e_idx, rank, gates, col_off, x1, mod, final_g, ys,
             seq_len, tc, final):
    t_tok = x1.shape[0]
    assert SEG_ROWS >= TOP_K * tc + N_EXPERTS * 2 * (ROW_TILE - 1) and (SEG_ROWS - TOP_K * tc) % K_CHUNK == 0
    kern = functools.partial(_combine_kernel, tc=tc, final=final)
    tok4 = lambda: pl.BlockSpec((tc, TOP_K), lambda i, *_: (i, 0))
    return pl.pallas_call(
        kern,
        out_shape=jax.ShapeDtypeStruct((t_tok, D_MODEL), F32),
        grid_spec=pltpu.PrefetchScalarGridSpec(
            num_scalar_prefetch=4,
            grid=(t_tok // tc,),
            in_specs=[tok4(), tok4(), tok4(),
                      pl.BlockSpec((1, 1, LANES), lambda i, *_: (i, 0, 0)),
                      pl.BlockSpec((tc, D_MODEL), lambda i, *_: (i, 0)),
                      pl.BlockSpec((1, 8, D_MODEL), lambda i, *_: ((i * tc) // seq_len, 0, 0)),
                      pl.BlockSpec((1, D_MODEL), lambda i, *_: (0, 0)),
                      pl.BlockSpec(memory_space=pl.ANY)],
            out_specs=pl.BlockSpec((tc, D_MODEL), lambda i, *_: (i, 0)),
            scratch_shapes=[pltpu.VMEM((2, SEG_ROWS, D_MODEL), BF16),
                            pltpu.VMEM((tc, SEG_ROWS), BF16),
                            pltpu.VMEM((tc, D_MODEL), F32),
                            pltpu.SemaphoreType.DMA((2,))]),
        compiler_params=_cparams(("arbitrary",)),
        name="combine",
    )(chunk_src, chunk_dst, n_chunks, tile_chunks, e_idx, rank, gates, col_off, x1, mod, final_g, ys)


def _dft_tables(n):
    j = jnp.arange(n, dtype=jnp.int32)
    ang = ((j[:, None] * j[None, :]) % n).astype(F32) * (2.0 * math.pi / n)
    scale = 1.0 / math.sqrt(n)
    return jnp.cos(ang) * scale, jnp.sin(ang) * scale


def _block_diag(blocks):
    n, r, c = blocks.shape
    rows = lax.broadcasted_iota(jnp.int32, (n * r, n * c), 0) // r
    cols = lax.broadcasted_iota(jnp.int32, (n * r, n * c), 1) // c
    return jnp.where(rows == cols, jnp.tile(blocks.reshape(n * r, c), (1, n)), 0)


def _prep_weights(w):
    depth = w["w_in"].shape[0]
    c64, s64 = _dft_tables(HEAD_DIM)
    n_heads = GROUP // HEAD_DIM
    chan_dft = jnp.concatenate([_block_diag(jnp.stack([c64] * n_heads)),
                                _block_diag(jnp.stack([s64] * n_heads))], axis=1).astype(BF16)
    avg = _block_diag(jnp.full((n_heads, HEAD_DIM, HEAD_DIM), 1.0 / HEAD_DIM, F32)).astype(BF16)
    stacked = dict(
        exp_gate_w=w["exp_gate_w"].astype(BF16).reshape(depth * N_EXPERTS, D_MODEL, D_FF),
        exp_up_w=w["exp_up_w"].astype(BF16).reshape(depth * N_EXPERTS, D_MODEL, D_FF),
        exp_down_w=w["exp_down_w"].astype(BF16).reshape(depth * N_EXPERTS, D_FF, D_MODEL),
        exp_gate_b=w["exp_gate_b"].reshape(depth * N_EXPERTS, 1, D_FF),
        exp_up_b=w["exp_up_b"].reshape(depth * N_EXPERTS, 1, D_FF),
        exp_down_b=w["exp_down_b"].reshape(depth * N_EXPERTS, 1, D_MODEL))
    layers = []
    for l in range(depth):
        layers.append(dict(
            stacked,
            norm1_g=w["norm1_g"][l].reshape(1, D_MODEL),
            norm2_g=w["norm2_g"][l].reshape(1, D_MODEL),
            w_in=w["w_in"][l].astype(BF16),
            fourier_w=w["fourier_w"][l].astype(BF16),
            conv31_w=jnp.pad(w["conv31_w"][l], ((0, 1), (0, 0))),
            conv31_b=w["conv31_b"][l].reshape(1, GROUP),
            gn_g=w["gn_g"][l].reshape(1, GROUP),
            gn_b=w["gn_b"][l].reshape(1, GROUP),
            avg=avg,
            pw_w=w["pw_w"][l].astype(BF16),
            pool_w=_block_diag(w["pool_w"][l]).astype(BF16),
            pool_scale=w["pool_scale"][l].reshape(1, GROUP),
            conv3_w=jnp.pad(w["conv3_w"][l], ((0, 5), (0, 0))),
            sconv_out_w=w["sconv_out_w"][l].astype(BF16),
            w_o=w["w_o"][l].astype(BF16),
            router_w=jnp.pad(w["router_w"][l], ((0, 0), (0, LANES - N_EXPERTS))).astype(BF16),
            router_b=jnp.pad(w["router_b"][l], (0, LANES - N_EXPERTS)).reshape(1, LANES),
        ))
    return chan_dft, layers


def _tiles(bsz, s):
    t_seq = min(512, s)
    t_dft_m = min(s, max(128, (8 * 1024 * 1024) // (bsz * GROUP * 4)))
    t_dft_k = min(2048, s)
    return dict(seq=t_seq, dft_m=t_dft_m, dft_k=t_dft_k, tok=min(256, bsz * s))


def _moe(h2, logits, x1, mod, final_g, wl, layer, seq_len, tiles, final):
    t_tok = h2.shape[0]
    tok = tiles["tok"]
    n_assign = t_tok * TOP_K
    n_blocks = -(-(n_assign + N_EXPERTS * DISP_WIN) // MOE_BLOCK) + N_EXPERTS
    n_slots = n_blocks * MOE_BLOCK
    e_idx, rank, gates, base, tile_cnt, counts = _route(logits, tok)
    counts = counts[0, :N_EXPERTS].astype(jnp.int32)
    padded = jnp.where(counts > 0, (counts + DISP_WIN + MOE_BLOCK - 1) // MOE_BLOCK * MOE_BLOCK, 0)
    pend = jnp.cumsum(padded)
    pstart = pend - padded
    npad = padded - counts
    block_start = jnp.arange(n_blocks, dtype=jnp.int32) * MOE_BLOCK
    block_e = jnp.minimum(jnp.sum(block_start[:, None] >= pend[None, :], axis=1), N_EXPERTS - 1).astype(jnp.int32)
    n_used = (pend[-1:] // MOE_BLOCK).astype(jnp.int32)
    seg_lo = pstart[None, :] + base[:, 0, :N_EXPERTS]
    seg_n = tile_cnt[:, 0, :N_EXPERTS]
    win_lo = seg_lo - (seg_lo & (ROW_TILE - 1))
    win_rows = jnp.where(seg_n > 0, -(-(seg_lo + seg_n - win_lo) // ROW_TILE) * ROW_TILE, 0)
    win_dst = jnp.cumsum(win_rows, axis=1) - win_rows
    col_off = jnp.pad(win_dst - win_lo + pstart[None, :], ((0, 0), (0, LANES - N_EXPERTS)))
    col_off = col_off.reshape(-1, 1, LANES).astype(jnp.int32)
    flat = lambda a: a.reshape(-1).astype(jnp.int32)
    head = seg_lo & (F32_TILE - 1)
    dwin_lo = seg_lo - head
    dwin_rows = jnp.where(seg_n > 0, -(-(seg_lo + seg_n - dwin_lo) // F32_TILE) * F32_TILE, 0)
    dwin_src = jnp.cumsum(dwin_rows, axis=1) - dwin_rows
    dcol_off = jnp.pad(dwin_src - dwin_lo + pstart[None, :], ((0, 0), (0, LANES - N_EXPERTS)))
    dcol_off = dcol_off.reshape(-1, 1, LANES).astype(jnp.int32)
    d_chunks = dwin_rows // F32_TILE
    xs = _dispatch(flat(dwin_src), flat(dwin_lo), flat(d_chunks), flat(jnp.where(seg_n > 0, head, 0)),
                   flat(jnp.sum(d_chunks > 0, axis=1)),
                   flat(jnp.sum(jnp.maximum(d_chunks - DISP_WIN // F32_TILE, 0), axis=1)),
                   flat(jnp.sum(dwin_rows, axis=1)),
                   pend.astype(jnp.int32), npad.astype(jnp.int32), e_idx, rank, dcol_off, h2, n_slots, tok)
    ys = _experts(block_e, n_used, xs, wl, layer)
    return _combine(flat(win_lo), flat(win_dst), flat(win_rows // ROW_TILE),
                    flat(jnp.sum(win_rows // ROW_TILE, axis=1)), e_idx, rank, gates, col_off,
                    x1, mod, final_g, ys, seq_len, tok, final)


def _encoder(x, c, w, chan_dft, layers, tables, tiles=None):
    bsz, s, _ = x.shape
    tiles = tiles or _tiles(bsz, s)
    cos_t, nsin_t = tables
    depth = len(layers)
    final_g = w["final_g"].reshape(1, D_MODEL)
    ada_b = w["ada_b"].reshape(depth, 1, 6 * D_MODEL)
    for l, wl in enumerate(layers):
        mod = _ada(c, w["ada_w"], ada_b, l).reshape(bsz, 6, D_MODEL)
        mod = jnp.pad(mod, ((0, 0), (0, 2), (0, 0)))
        p, u = _inproj(x, mod, wl["norm1_g"], wl["w_in"], chan_dft, tiles["seq"])
        y = _seqdft(cos_t, nsin_t, p, tiles["dft_m"], tiles["dft_k"])
        x1, h2, logits = _mix(y, u, x, mod, wl, tiles["seq"])
        x = _moe(h2.reshape(bsz * s, D_MODEL), logits.reshape(bsz * s, LANES),
                 x1.reshape(bsz * s, D_MODEL), mod, final_g, wl, l, s, tiles,
                 final=(l == depth - 1)).reshape(bsz, s, D_MODEL)
    return x


def _seq_tables(s):
    k = jnp.arange(s, dtype=jnp.int32)[None, :]

    def cos_sin(j):
        ang = ((j[:, None] * k) % s).astype(F32) * (2.0 * math.pi / s)
        return jnp.cos(ang), jnp.sin(ang)

    ca, sa = cos_sin(jnp.arange(s // LANES, dtype=jnp.int32) * LANES)
    cb, sb = cos_sin(jnp.arange(LANES, dtype=jnp.int32))
    scale = 1.0 / math.sqrt(s)
    cos_t = (ca[:, None, :] * cb[None, :, :] - sa[:, None, :] * sb[None, :, :]) * scale
    nsin_t = (sa[:, None, :] * cb[None, :, :] + ca[:, None, :] * sb[None, :, :]) * (-scale)
    return cos_t.reshape(s, s).astype(BF16), nsin_t.reshape(s, s).astype(BF16)


def kernel(x_prompt, x_sample, c_prompt, c_sample, norm1_g, norm2_g, ada_w, ada_b, w_in, fourier_w, conv31_w, conv31_b, gn_g, gn_b, pw_w, pool_w, pool_scale, conv3_w, sconv_out_w, w_o, router_w, router_b, exp_gate_w, exp_gate_b, exp_up_w, exp_up_b, exp_down_w, exp_down_b, final_g):
    w = dict(norm1_g=norm1_g, norm2_g=norm2_g, ada_w=ada_w, ada_b=ada_b, w_in=w_in, fourier_w=fourier_w,
             conv31_w=conv31_w, conv31_b=conv31_b, gn_g=gn_g, gn_b=gn_b, pw_w=pw_w, pool_w=pool_w,
             pool_scale=pool_scale, conv3_w=conv3_w, sconv_out_w=sconv_out_w, w_o=w_o, router_w=router_w,
             router_b=router_b, exp_gate_w=exp_gate_w, exp_gate_b=exp_gate_b, exp_up_w=exp_up_w,
             exp_up_b=exp_up_b, exp_down_w=exp_down_w, exp_down_b=exp_down_b, final_g=final_g)
    chan_dft, layers = _prep_weights(w)
    y_prompt = _encoder(x_prompt, c_prompt, w, chan_dft, layers, _seq_tables(x_prompt.shape[1]))
    y_sample = _encoder(x_sample, c_sample, w, chan_dft, layers, _seq_tables(x_sample.shape[1]))
    return (y_prompt, y_sample)
```

```python
import functools
import math

import jax
import jax.numpy as jnp
from jax import lax
from jax.experimental import pallas as pl
from jax.experimental.pallas import tpu as pltpu

D_MODEL = 1024
GROUP = 256
HEAD_DIM = 64
CONV31 = 31
POOL_WINDOWS = (2, 4, 8, 16)
N_EXPERTS = 32
TOP_K = 4
D_FF = 1024
SWIGLU_ALPHA = 1.702
SWIGLU_LIMIT = 7.0
MOE_BLOCK = 512
RMS_EPS = 1e-6
GN_EPS = 1e-5
HALO = 16
LANES = 128
VMEM_LIMIT = 48 * 1024 * 1024

F32 = jnp.float32
BF16 = jnp.bfloat16


def _cparams(sem):
    return pltpu.CompilerParams(dimension_semantics=sem, vmem_limit_bytes=VMEM_LIMIT)


def _dot(a, b):
    return jnp.dot(a, b, preferred_element_type=F32)


def _split_bf16(x):
    hi = x.astype(BF16)
    lo = (x - hi.astype(F32)).astype(BF16)
    return hi, lo


def _dot_hilo(x, w_bf16):
    hi, lo = _split_bf16(x)
    return _dot(hi, w_bf16) + _dot(lo, w_bf16)


def _sigmoid(x):
    return 1.0 / (1.0 + jnp.exp(-x))


def _ada_kernel(c_ref, w_ref, b_ref, o_ref):
    c = c_ref[...]
    cs = c * _sigmoid(c)
    c_hi, c_lo = _split_bf16(cs)
    w_hi, w_lo = _split_bf16(w_ref[0])
    o_ref[...] = _dot(c_hi, w_hi) + _dot(c_lo, w_hi) + _dot(c_hi, w_lo) + b_ref[0]


def _ada(c, ada_w, ada_b, layer):
    bsz = c.shape[0]
    n_out = ada_w.shape[2]
    return pl.pallas_call(
        _ada_kernel,
        out_shape=jax.ShapeDtypeStruct((bsz, n_out), F32),
        grid=(n_out // D_MODEL,),
        in_specs=[
            pl.BlockSpec((bsz, D_MODEL), lambda j: (0, 0)),
            pl.BlockSpec((1, D_MODEL, D_MODEL), lambda j: (layer, 0, j)),
            pl.BlockSpec((1, 1, D_MODEL), lambda j: (layer, 0, j)),
        ],
        out_specs=pl.BlockSpec((bsz, D_MODEL), lambda j: (0, j)),
        compiler_params=_cparams(("arbitrary",)),
        name="ada",
    )(c, ada_w, ada_b)


def _rms_mod(x, g, scale, shift):
    ms = jnp.mean(x * x, axis=-1, keepdims=True)
    y = x * lax.rsqrt(ms + RMS_EPS) * g
    return y * (1.0 + scale) + shift


def _inproj_kernel(x_ref, mod_ref, g_ref, w_ref, cs_ref, p_ref, u_ref):
    x = x_ref[0]
    h = _rms_mod(x, g_ref[...], mod_ref[0, 1:2, :], mod_ref[0, 0:1, :]).astype(BF16)
    a = _dot(h, w_ref[:, 0:GROUP]).astype(BF16)
    p_ref[0] = _dot(a, cs_ref[...]).astype(BF16)
    for j in range(6):
        u_ref[0, :, j * GROUP:(j + 1) * GROUP] = _dot(
            h, w_ref[:, (j + 1) * GROUP:(j + 2) * GROUP]).astype(BF16)


def _inproj(x, mod, g, w_in, chan_dft, tm):
    bsz, s, _ = x.shape
    return pl.pallas_call(
        _inproj_kernel,
        out_shape=(jax.ShapeDtypeStruct((bsz, s, 2 * GROUP), BF16),
                   jax.ShapeDtypeStruct((bsz, s, 6 * GROUP), BF16)),
        grid=(bsz, s // tm),
        in_specs=[
            pl.BlockSpec((1, tm, D_MODEL), lambda b, i: (b, i, 0)),
            pl.BlockSpec((1, 8, D_MODEL), lambda b, i: (b, 0, 0)),
            pl.BlockSpec((1, D_MODEL), lambda b, i: (0, 0)),
            pl.BlockSpec((D_MODEL, 7 * GROUP), lambda b, i: (0, 0)),
            pl.BlockSpec((GROUP, 2 * GROUP), lambda b, i: (0, 0)),
        ],
        out_specs=(pl.BlockSpec((1, tm, 2 * GROUP), lambda b, i: (b, i, 0)),
                   pl.BlockSpec((1, tm, 6 * GROUP), lambda b, i: (b, i, 0))),
        compiler_params=_cparams(("arbitrary", "arbitrary")),
        name="inproj",
    )(x, mod, g, w_in, chan_dft)


def _fold_kernel(cur_ref, mir_ref, nxt_ref, e_ref, *, tf):
    kt = pl.program_id(1)
    r_io = lax.broadcasted_iota(jnp.int32, (tf, tf), 0)
    c_io = lax.broadcasted_iota(jnp.int32, (tf, tf), 1)
    flip = (c_io == tf - r_io).astype(BF16)
    rev = _dot(flip, mir_ref[0])
    cur = cur_ref[0].astype(F32)
    lane = lax.broadcasted_iota(jnp.int32, (1, 2 * GROUP), 1)
    first_tile_row0 = jnp.where(lane < GROUP, 0.0, cur[0:1, :])
    row0 = jnp.where(kt == 0, first_tile_row0, nxt_ref[0, 0:1, :].astype(F32))
    row = lax.broadcasted_iota(jnp.int32, (tf, 2 * GROUP), 0)
    rev = jnp.where(row == 0, row0, rev)
    sign = jnp.where(lax.broadcasted_iota(jnp.int32, (tf, 2 * GROUP), 1) < GROUP, 1.0, -1.0)
    e_ref[0] = (cur + sign * rev).astype(BF16)


def _fold(p, tf):
    bsz, s, _ = p.shape
    n_t = s // tf
    kern = functools.partial(_fold_kernel, tf=tf)
    return pl.pallas_call(
        kern,
        out_shape=jax.ShapeDtypeStruct((bsz, s // 2, 2 * GROUP), BF16),
        grid=(bsz, n_t // 2),
        in_specs=[
            pl.BlockSpec((1, tf, 2 * GROUP), lambda b, k: (b, k, 0)),
            pl.BlockSpec((1, tf, 2 * GROUP), lambda b, k: (b, n_t - 1 - k, 0)),
            pl.BlockSpec((1, 8, 2 * GROUP), lambda b, k: (b, ((n_t - k) % n_t) * (tf // 8), 0)),
        ],
        out_specs=pl.BlockSpec((1, tf, 2 * GROUP), lambda b, k: (b, k, 0)),
        compiler_params=_cparams(("arbitrary", "arbitrary")),
        name="fold",
    )(p, p, p)


def _seqdft_kernel(c_ref, s_ref, e_ref, mid_ref, o_ref, acc_ref, *, tm, seq_len):
    i = pl.program_id(0)
    k = pl.program_id(1)
    b = pl.program_id(2)
    contrib = _dot(c_ref[...], e_ref[0, :, 0:GROUP]) + _dot(s_ref[...], e_ref[0, :, GROUP:2 * GROUP])

    @pl.when(k == 0)
    def _():
        acc_ref[b] = contrib

    @pl.when(k > 0)
    def _():
        acc_ref[b] = acc_ref[b] + contrib

    @pl.when(k == pl.num_programs(1) - 1)
    def _():
        j = lax.broadcasted_iota(jnp.int32, (tm, GROUP), 0) + i * tm
        sign = jnp.where((j & 1) == 0, 1.0, -1.0) * (1.0 / math.sqrt(seq_len))
        o_ref[b] = (acc_ref[b] + sign * mid_ref[0, 0:1, 0:GROUP].astype(F32)).astype(BF16)


def _seqdft(cos_t, nsin_t, p, e, tm, tk):
    bsz, s, _ = p.shape
    half = s // 2
    kern = functools.partial(_seqdft_kernel, tm=tm, seq_len=s)
    return pl.pallas_call(
        kern,
        out_shape=jax.ShapeDtypeStruct((bsz, s, GROUP), BF16),
        grid=(s // tm, half // tk, bsz),
        in_specs=[
            pl.BlockSpec((tm, tk), lambda i, k, b: (i, k)),
            pl.BlockSpec((tm, tk), lambda i, k, b: (i, k)),
            pl.BlockSpec((1, tk, 2 * GROUP), lambda i, k, b: (b, k, 0)),
            pl.BlockSpec((1, 8, 2 * GROUP), lambda i, k, b: (b, half // 8, 0)),
        ],
        out_specs=pl.BlockSpec((bsz, tm, GROUP), lambda i, k, b: (0, i, 0)),
        scratch_shapes=[pltpu.VMEM((bsz, tm, GROUP), F32)],
        compiler_params=_cparams(("arbitrary", "arbitrary", "arbitrary")),
        name="seqdft",
    )(cos_t, nsin_t, e, p)


U_V, U_G, U_P, U_BG, U_CG, U_XV = (j * GROUP for j in range(6))


def _mix_kernel(y_ref, u_ref, up_ref, un_ref, x_ref, mod_ref,
                fw_ref, c31w_ref, c31b_ref, gng_ref, gnb_ref, avg_ref, pww_ref,
                poolw_ref, pscale_ref, c3w_ref, sow_ref, wo_ref, n2g_ref, rw_ref, rb_ref,
                x1_ref, h2_ref, lg_ref,
                zext, zsh, pext, s2, s4, s8, s16, qext, cat, *, tq, seq_len):
    i = pl.program_id(1)
    has_prev = (i > 0).astype(F32)
    has_next = (i < pl.num_programs(1) - 1).astype(F32)
    rows = tq + 2 * HALO

    def cols(ref, c0):
        return ref[0, :, c0:c0 + GROUP].astype(F32)

    def fill_ext(dst, fn):
        dst[0:HALO, :] = fn(up_ref) * has_prev
        dst[HALO:HALO + tq, :] = fn(u_ref)
        dst[HALO + tq:rows, :] = fn(un_ref) * has_next

    cat[:, 0:GROUP] = _dot(y_ref[0], fw_ref[...]).astype(BF16)

    fill_ext(zext, lambda r: cols(r, U_V) * _sigmoid(cols(r, U_G)))
    conv = jnp.broadcast_to(c31b_ref[...], (tq, GROUP))
    first = HALO - CONV31 // 2
    for res in range(8):
        taps = [j for j in range(CONV31) if (first + j) % 8 == res]
        span = (first + taps[-1]) // 8 * 8 + tq
        zsh[0:span, :] = zext[pl.ds(res, span), :]
        for j in taps:
            q = (first + j) // 8 * 8
            conv = conv + c31w_ref[j:j + 1, :] * zsh[q:q + tq, :]
    mu = _dot_hilo(conv, avg_ref[...])
    dev = conv - mu
    var = _dot_hilo(dev * dev, avg_ref[...])
    zn = dev * lax.rsqrt(var + GN_EPS) * gng_ref[...] + gnb_ref[...]
    cat[:, GROUP:2 * GROUP] = _dot((zn * _sigmoid(zn)).astype(BF16), pww_ref[...]).astype(BF16)

    fill_ext(pext, lambda r: cols(r, U_P))
    n = rows - 8
    zeros8 = jnp.zeros((8, GROUP), F32)
    s2[0:n, :] = pext[0:n, :] + pext[pl.ds(1, n), :]
    s2[n:rows, :] = zeros8
    s4[0:n, :] = s2[0:n, :] + s2[pl.ds(2, n), :]
    s4[n:rows, :] = zeros8
    s8[0:n, :] = s4[0:n, :] + s4[pl.ds(4, n), :]
    s8[n:rows, :] = zeros8
    s16[0:n, :] = s8[0:n, :] + s8[pl.ds(8, n), :]
    lane = lax.broadcasted_iota(jnp.int32, (tq, GROUP), 1)
    pos = lax.broadcasted_iota(jnp.int32, (tq, GROUP), 0) + i * tq
    win = jnp.where(lane < HEAD_DIM, s2[pl.ds(HALO - 1, tq), :],
                    jnp.where(lane < 2 * HEAD_DIM, s4[pl.ds(HALO - 2, tq), :],
                              jnp.where(lane < 3 * HEAD_DIM, s8[pl.ds(HALO - 4, tq), :],
                                        s16[pl.ds(HALO - 8, tq), :])))
    half = jnp.where(lane < HEAD_DIM, 1,
                     jnp.where(lane < 2 * HEAD_DIM, 2, jnp.where(lane < 3 * HEAD_DIM, 4, 8)))
    cnt = jnp.minimum(pos + half, seq_len) - jnp.maximum(pos - half, 0)
    dpool = win / cnt.astype(F32) - pext[HALO:HALO + tq, :]
    cat[:, 2 * GROUP:3 * GROUP] = (_dot(dpool.astype(BF16), poolw_ref[...]) * pscale_ref[...]).astype(BF16)

    fill_ext(qext, lambda r: cols(r, U_CG) * cols(r, U_XV))
    c3 = (c3w_ref[0:1, :] * qext[pl.ds(HALO - 1, tq), :]
          + c3w_ref[1:2, :] * qext[HALO:HALO + tq, :]
          + c3w_ref[2:3, :] * qext[pl.ds(HALO + 1, tq), :])
    cat[:, 3 * GROUP:4 * GROUP] = _dot((cols(u_ref, U_BG) * c3).astype(BF16), sow_ref[...]).astype(BF16)

    mixed = _dot(cat[...], wo_ref[...])
    x1 = x_ref[0] + mod_ref[0, 2:3, :] * mixed
    x1_ref[0] = x1
    h2 = _rms_mod(x1, n2g_ref[...], mod_ref[0, 4:5, :], mod_ref[0, 3:4, :])
    h2_ref[0] = h2
    lg_ref[0] = _dot(h2.astype(BF16), rw_ref[...]) + rb_ref[...]


def _mix(y, u, x, mod, wl, tq):
    bsz, s, _ = x.shape
    rows = tq + 2 * HALO
    hb = tq // HALO
    n_hblk = s // HALO
    full = lambda shape: pl.BlockSpec(shape, lambda b, i: tuple(0 for _ in shape))
    kern = functools.partial(_mix_kernel, tq=tq, seq_len=s)
    return pl.pallas_call(
        kern,
        out_shape=(jax.ShapeDtypeStruct((bsz, s, D_MODEL), F32),
                   jax.ShapeDtypeStruct((bsz, s, D_MODEL), F32),
                   jax.ShapeDtypeStruct((bsz, s, LANES), F32)),
        grid=(bsz, s // tq),
        in_specs=[
            pl.BlockSpec((1, tq, GROUP), lambda b, i: (b, i, 0)),
            pl.BlockSpec((1, tq, 6 * GROUP), lambda b, i: (b, i, 0)),
            pl.BlockSpec((1, HALO, 6 * GROUP), lambda b, i: (b, jnp.maximum(i * hb - 1, 0), 0)),
            pl.BlockSpec((1, HALO, 6 * GROUP), lambda b, i: (b, jnp.minimum((i + 1) * hb, n_hblk - 1), 0)),
            pl.BlockSpec((1, tq, D_MODEL), lambda b, i: (b, i, 0)),
            pl.BlockSpec((1, 8, D_MODEL), lambda b, i: (b, 0, 0)),
            full((GROUP, GROUP)),
            full((32, GROUP)),
            full((1, GROUP)),
            full((1, GROUP)),
            full((1, GROUP)),
            full((GROUP, GROUP)),
            full((GROUP, GROUP)),
            full((GROUP, GROUP)),
            full((1, GROUP)),
            full((8, GROUP)),
            full((GROUP, GROUP)),
            full((D_MODEL, D_MODEL)),
            full((1, D_MODEL)),
            full((D_MODEL, LANES)),
            full((1, LANES)),
        ],
        out_specs=(pl.BlockSpec((1, tq, D_MODEL), lambda b, i: (b, i, 0)),
                   pl.BlockSpec((1, tq, D_MODEL), lambda b, i: (b, i, 0)),
                   pl.BlockSpec((1, tq, LANES), lambda b, i: (b, i, 0))),
        scratch_shapes=[pltpu.VMEM((rows, GROUP), F32) for _ in range(8)]
        + [pltpu.VMEM((tq, D_MODEL), BF16)],
        compiler_params=_cparams(("arbitrary", "arbitrary")),
        name="mix",
    )(y, u, u, u, x, mod, wl["fourier_w"], wl["conv31_w"], wl["conv31_b"], wl["gn_g"], wl["gn_b"],
      wl["avg"], wl["pw_w"], wl["pool_w"], wl["pool_scale"], wl["conv3_w"], wl["sconv_out_w"],
      wl["w_o"], wl["norm2_g"], wl["router_w"], wl["router_b"])


def _route_kernel(lg_ref, e_ref, rank_ref, gate_ref, base_ref, tcnt_ref, cnt_ref, carry, *, tr, sub):
    i = pl.program_id(0)

    @pl.when(i == 0)
    def _():
        carry[...] = jnp.zeros_like(carry)

    lane = lax.broadcasted_iota(jnp.int32, (tr, LANES), 1).astype(F32)
    neg = jnp.float32(-jnp.inf)
    work = jnp.where(lane < N_EXPERTS, lg_ref[...], neg)
    vals, idxs, hots = [], [], []
    for _ in range(TOP_K):
        m = jnp.max(work, axis=1, keepdims=True)
        idx = jnp.min(jnp.where(work == m, lane, float(LANES)), axis=1, keepdims=True)
        hot = lane == idx
        vals.append(m)
        idxs.append(idx.astype(jnp.int32))
        hots.append(hot)
        work = jnp.where(hot, neg, work)
    exps = [jnp.exp(v - vals[0]) for v in vals]
    denom = exps[0] + exps[1] + exps[2] + exps[3]
    member = sum(h.astype(F32) for h in hots)
    r_io = lax.broadcasted_iota(jnp.int32, (tr, tr), 0)
    c_io = lax.broadcasted_iota(jnp.int32, (tr, tr), 1)
    lower = (c_io < r_io).astype(BF16)
    before = _dot(lower, member.astype(BF16)) + carry[...]
    col = lax.broadcasted_iota(jnp.int32, (tr, TOP_K), 1)
    e_out = jnp.zeros((tr, TOP_K), jnp.int32)
    r_out = jnp.zeros((tr, TOP_K), jnp.int32)
    g_out = jnp.zeros((tr, TOP_K), F32)
    for k in range(TOP_K):
        rk = jnp.sum(jnp.where(hots[k], before, 0.0), axis=1, keepdims=True).astype(jnp.int32)
        e_out = jnp.where(col == k, idxs[k], e_out)
        r_out = jnp.where(col == k, rk, r_out)
        g_out = jnp.where(col == k, exps[k] / denom, g_out)
    e_ref[...] = e_out
    rank_ref[...] = r_out
    gate_ref[...] = g_out
    running = carry[...]
    for q in range(tr // sub):
        cnt_q = jnp.sum(member[q * sub:(q + 1) * sub, :], axis=0, keepdims=True)
        base_ref[q] = running.astype(jnp.int32)
        tcnt_ref[q] = cnt_q.astype(jnp.int32)
        running = running + cnt_q
    carry[...] = running
    cnt_ref[...] = running


def _route(logits, sub):
    t_tok = logits.shape[0]
    tr = min(2 * sub, t_tok)
    n_steps = t_tok // tr
    n_tiles = t_tok // sub
    kern = functools.partial(_route_kernel, tr=tr, sub=sub)
    tile = lambda w: pl.BlockSpec((tr, w), lambda i: (i, 0))
    per_tile = pl.BlockSpec((tr // sub, 1, LANES), lambda i: (i, 0, 0))
    return pl.pallas_call(
        kern,
        out_shape=(jax.ShapeDtypeStruct((t_tok, TOP_K), jnp.int32),
                   jax.ShapeDtypeStruct((t_tok, TOP_K), jnp.int32),
                   jax.ShapeDtypeStruct((t_tok, TOP_K), F32),
                   jax.ShapeDtypeStruct((n_tiles, 1, LANES), jnp.int32),
                   jax.ShapeDtypeStruct((n_tiles, 1, LANES), jnp.int32),
                   jax.ShapeDtypeStruct((1, LANES), F32)),
        grid=(n_steps,),
        in_specs=[tile(LANES)],
        out_specs=(tile(TOP_K), tile(TOP_K), tile(TOP_K), per_tile, per_tile,
                   pl.BlockSpec((1, LANES), lambda i: (0, 0))),
        scratch_shapes=[pltpu.VMEM((1, LANES), F32)],
        compiler_params=_cparams(("arbitrary",)),
        name="route",
    )(logits)


F32_TILE = 8
DISP_ROWS = TOP_K * 256 + N_EXPERTS * 2 * F32_TILE
CARRY_ROWS = N_EXPERTS * F32_TILE
DISP_WIN = 64
XS_COLS = D_MODEL // 2


def _pack_pairs(x):
    lo = lax.bitcast_convert_type(x[:, :XS_COLS], jnp.uint32)
    hi = lax.bitcast_convert_type(x[:, XS_COLS:], jnp.uint32)
    return (hi & jnp.uint32(0xFFFF0000)) | (lo >> 16)


def _unpack_pairs(w):
    lo = lax.bitcast_convert_type(w << 16, F32).astype(BF16)
    hi = lax.bitcast_convert_type(w & jnp.uint32(0xFFFF0000), F32).astype(BF16)
    return lo, hi


def _dispatch_kernel(src_ref, dst_ref, nch_ref, head_ref, nwin_ref, nextra_ref, used_ref, pend_ref, npad_ref,
                     e_ref, rank_ref, off_ref, h_ref, xs_hbm, outbuf, carry, zeros, sems, sem_z, *, td):
    i = pl.program_id(0)
    n_steps = pl.num_programs(0)
    slot = i % 2

    def zero_blk(row):
        return pltpu.make_async_copy(zeros, xs_hbm.at[pl.ds(pl.multiple_of(row, MOE_BLOCK), MOE_BLOCK)], sem_z)

    @pl.when(i == 0)
    def _():
        zeros[...] = jnp.zeros_like(zeros)
        carry[...] = jnp.zeros_like(carry)
        outbuf[...] = jnp.zeros_like(outbuf)
        first_unused = pend_ref[N_EXPERTS - 1] // MOE_BLOCK
        n_blocks = xs_hbm.shape[0] // MOE_BLOCK

        def start(e, c):
            @pl.when(npad_ref[e] > 0)
            def _():
                zero_blk(pend_ref[e] - MOE_BLOCK).start()

            @pl.when(npad_ref[e] > MOE_BLOCK)
            def _():
                zero_blk(pend_ref[e] - 2 * MOE_BLOCK).start()
            return c

        def wait(e, c):
            @pl.when(npad_ref[e] > 0)
            def _():
                zero_blk(pend_ref[e] - MOE_BLOCK).wait()

            @pl.when(npad_ref[e] > MOE_BLOCK)
            def _():
                zero_blk(pend_ref[e] - 2 * MOE_BLOCK).wait()
            return c

        def start_tail(b, c):
            zero_blk(b * MOE_BLOCK).start()
            return c

        def wait_tail(b, c):
            zero_blk(b * MOE_BLOCK).wait()
            return c

        lax.fori_loop(0, N_EXPERTS, start, 0)
        lax.fori_loop(first_unused, n_blocks, start_tail, 0)
        lax.fori_loop(0, N_EXPERTS, wait, 0)
        lax.fori_loop(first_unused, n_blocks, wait_tail, 0)

    def window_copies(tile, sl, act):
        for e in range(N_EXPERTS):
            n_ch = nch_ref[tile * N_EXPERTS + e]
            src0 = pl.multiple_of(src_ref[tile * N_EXPERTS + e], F32_TILE)
            dst0 = pl.multiple_of(dst_ref[tile * N_EXPERTS + e], F32_TILE)

            @pl.when(n_ch > 0)
            def _():
                act(pltpu.make_async_copy(outbuf.at[sl, pl.ds(src0, DISP_WIN)],
                                          xs_hbm.at[pl.ds(dst0, DISP_WIN)], sems.at[sl]))

            def per_chunk(ci, c2):
                src = pl.multiple_of(src0 + ci * F32_TILE, F32_TILE)
                dst = pl.multiple_of(dst0 + ci * F32_TILE, F32_TILE)
                act(pltpu.make_async_copy(outbuf.at[sl, pl.ds(src, F32_TILE)],
                                          xs_hbm.at[pl.ds(dst, F32_TILE)], sems.at[sl]))
                return c2

            lax.fori_loop(DISP_WIN // F32_TILE, n_ch, per_chunk, 0)

    lane_td = lax.broadcasted_iota(jnp.int32, (td, LANES), 1)
    off_f = off_ref[0].astype(F32)
    pos_cols = jnp.full((td, LANES), -1.0, F32)
    for k in range(TOP_K):
        off_k = jnp.sum(jnp.where(lane_td == e_ref[:, k:k + 1], off_f, 0.0), axis=1, keepdims=True)
        pos_cols = jnp.where(lane_td == k, rank_ref[:, k:k + 1].astype(F32) + off_k, pos_cols)
    pos_t = jnp.transpose(pos_cols).astype(jnp.int32)
    pos = [jnp.broadcast_to(pos_t[k:k + 1, :], (GROUP, td)) for k in range(TOP_K)]
    row_io = lax.broadcasted_iota(jnp.int32, (GROUP, td), 0)
    h_bf = h_ref[...].astype(BF16)
    def select_rows(rc):
        r = row_io + rc * GROUP
        tok = jnp.zeros((GROUP, td), F32)
        for k in range(TOP_K):
            tok = jnp.where(r == pos[k], 1.0, tok)
        outbuf[slot, rc * GROUP:(rc + 1) * GROUP, :] = _pack_pairs(_dot(tok.astype(BF16), h_bf))

    for rc in range(DISP_ROWS // GROUP):
        if rc * GROUP < TOP_K * td:
            select_rows(rc)
        else:
            pl.when(used_ref[i] > rc * GROUP)(functools.partial(select_rows, rc))

    sub_io = lax.broadcasted_iota(jnp.int32, (F32_TILE, XS_COLS), 0)
    for e in range(N_EXPERTS):
        es = slice(e * F32_TILE, (e + 1) * F32_TILE)
        n_ch = nch_ref[i * N_EXPERTS + e]
        first = pl.multiple_of(src_ref[i * N_EXPERTS + e], F32_TILE)
        outbuf[slot, pl.ds(first, F32_TILE), :] = jnp.where(
            sub_io < head_ref[i * N_EXPERTS + e], carry[es, :], outbuf[slot, pl.ds(first, F32_TILE), :])
        last = pl.multiple_of(first + jnp.maximum(n_ch - 1, 0) * F32_TILE, F32_TILE)
        carry[es, :] = jnp.where(n_ch > 0, outbuf[slot, pl.ds(last, F32_TILE), :], carry[es, :])

    def wait_copies(tile, sl):
        def wait_window(c, carry_):
            pltpu.make_async_copy(outbuf.at[sl, pl.ds(0, DISP_WIN)], xs_hbm.at[pl.ds(0, DISP_WIN)],
                                  sems.at[sl]).wait()
            return carry_

        def wait_chunk(c, carry_):
            pltpu.make_async_copy(outbuf.at[sl, pl.ds(0, F32_TILE)], xs_hbm.at[pl.ds(0, F32_TILE)],
                                  sems.at[sl]).wait()
            return carry_

        lax.fori_loop(0, nwin_ref[tile], wait_window, 0)
        lax.fori_loop(0, nextra_ref[tile], wait_chunk, 0)

    @pl.when(i > 0)
    def _():
        wait_copies(i - 1, 1 - slot)

    window_copies(i, slot, lambda cp: cp.start())

    @pl.when(i == n_steps - 1)
    def _():
        wait_copies(i, slot)


def _dispatch(chunk_src, chunk_dst, n_chunks, head, n_windows, n_extra, used_rows, pend, npad, e_idx, rank,
              col_off, h2, n_slots, td):
    t_tok = h2.shape[0]
    assert DISP_ROWS >= TOP_K * td + N_EXPERTS * 2 * (F32_TILE - 1) + F32_TILE and DISP_ROWS % GROUP == 0
    kern = functools.partial(_dispatch_kernel, td=td)
    tok4 = lambda: pl.BlockSpec((td, TOP_K), lambda i, *_: (i, 0))
    return pl.pallas_call(
        kern,
        out_shape=jax.ShapeDtypeStruct((n_slots, XS_COLS), jnp.uint32),
        grid_spec=pltpu.PrefetchScalarGridSpec(
            num_scalar_prefetch=9,
            grid=(t_tok // td,),
            in_specs=[tok4(), tok4(),
                      pl.BlockSpec((1, 1, LANES), lambda i, *_: (i, 0, 0)),
                      pl.BlockSpec((td, D_MODEL), lambda i, *_: (i, 0))],
            out_specs=pl.BlockSpec(memory_space=pl.ANY),
            scratch_shapes=[pltpu.VMEM((2, DISP_ROWS + DISP_WIN, XS_COLS), jnp.uint32),
                            pltpu.VMEM((CARRY_ROWS, XS_COLS), jnp.uint32),
                            pltpu.VMEM((MOE_BLOCK, XS_COLS), jnp.uint32),
                            pltpu.SemaphoreType.DMA((2,)),
                            pltpu.SemaphoreType.DMA]),
        compiler_params=_cparams(("arbitrary",)),
        name="dispatch",
    )(chunk_src, chunk_dst, n_chunks, head, n_windows, n_extra, used_rows, pend, npad, e_idx, rank, col_off, h2)


def _experts_kernel(be_ref, nu_ref, xs_ref, wg_ref, bg_ref, wu_ref, bu_ref, wd_ref, bd_ref, o_ref, act):
    i = pl.program_id(0)

    @pl.when(i < nu_ref[0])
    def _():
        x_lo, x_hi = _unpack_pairs(xs_ref[...])

        def proj(w_ref, fs):
            return _dot(x_lo, w_ref[0, :XS_COLS, fs]) + _dot(x_hi, w_ref[0, XS_COLS:, fs])

        for f in range(D_FF // GROUP):
            fs = slice(f * GROUP, (f + 1) * GROUP)
            gt = jnp.minimum(proj(wg_ref, fs) + bg_ref[0, :, fs], SWIGLU_LIMIT)
            up = jnp.clip(proj(wu_ref, fs) + bu_ref[0, :, fs], -SWIGLU_LIMIT, SWIGLU_LIMIT)
            act[:, fs] = ((up + 1.0) * (gt * _sigmoid(SWIGLU_ALPHA * gt))).astype(BF16)
        o_ref[...] = (_dot(act[...], wd_ref[0]) + bd_ref[0]).astype(BF16)

    @pl.when(i >= nu_ref[0])
    def _():
        o_ref[...] = jnp.zeros_like(o_ref)


def _experts(block_e, n_used, xs, wl, layer):
    n_slots = xs.shape[0]
    n_blocks = n_slots // MOE_BLOCK

    def expert(i, be, nu):
        return layer * N_EXPERTS + be[jnp.minimum(i, nu[0] - 1)]

    wspec = lambda: pl.BlockSpec((1, D_MODEL, D_FF), lambda i, be, nu: (expert(i, be, nu), 0, 0))
    bspec = lambda: pl.BlockSpec((1, 1, D_FF), lambda i, be, nu: (expert(i, be, nu), 0, 0))
    return pl.pallas_call(
        _experts_kernel,
        out_shape=jax.ShapeDtypeStruct((n_slots, D_MODEL), BF16),
        grid_spec=pltpu.PrefetchScalarGridSpec(
            num_scalar_prefetch=2,
            grid=(n_blocks,),
            in_specs=[pl.BlockSpec((MOE_BLOCK, XS_COLS), lambda i, be, nu: (jnp.minimum(i, nu[0] - 1), 0)),
                      wspec(), bspec(), wspec(), bspec(), wspec(), bspec()],
            out_specs=pl.BlockSpec((MOE_BLOCK, D_MODEL), lambda i, be, nu: (i, 0)),
            scratch_shapes=[pltpu.VMEM((MOE_BLOCK, D_FF), BF16)]),
        compiler_params=_cparams(("arbitrary",)),
        name="experts",
    )(block_e, n_used, xs, wl["exp_gate_w"], wl["exp_gate_b"], wl["exp_up_w"], wl["exp_up_b"],
      wl["exp_down_w"], wl["exp_down_b"])


ROW_TILE = 16
SEG_ROWS = TOP_K * 256 + N_EXPERTS * 2 * ROW_TILE
K_CHUNK = 512


def _combine_kernel(src_ref, dst_ref, nch_ref, tot_ref, e_ref, rank_ref, gate_ref, off_ref, x1_ref, mod_ref,
                    fg_ref, ys_hbm, o_ref, segbuf, gmat, acc, sems, *, tc, final):
    j = pl.program_id(0)
    n_tiles = pl.num_programs(0)
    slot = j % 2

    def chunks(tile, sl, act):
        def per_expert(e, c):
            src0 = src_ref[tile * N_EXPERTS + e]
            dst0 = dst_ref[tile * N_EXPERTS + e]

            def per_chunk(ci, c2):
                src = pl.multiple_of(src0 + ci * ROW_TILE, ROW_TILE)
                dst = pl.multiple_of(dst0 + ci * ROW_TILE, ROW_TILE)
                act(pltpu.make_async_copy(ys_hbm.at[pl.ds(src, ROW_TILE)],
                                          segbuf.at[sl, pl.ds(dst, ROW_TILE)], sems.at[sl]))
                return c2

            lax.fori_loop(0, nch_ref[tile * N_EXPERTS + e], per_chunk, 0)
            return c

        lax.fori_loop(0, N_EXPERTS, per_expert, 0)

    @pl.when(j == 0)
    def _():
        segbuf[...] = jnp.zeros_like(segbuf)
        chunks(0, 0, lambda cp: cp.start())

    @pl.when(j + 1 < n_tiles)
    def _():
        chunks(j + 1, 1 - slot, lambda cp: cp.start())

    lane_tc = lax.broadcasted_iota(jnp.int32, (tc, LANES), 1)
    off_f = off_ref[0].astype(F32)
    cols = []
    for k in range(TOP_K):
        off_k = jnp.sum(jnp.where(lane_tc == e_ref[:, k:k + 1], off_f, 0.0), axis=1, keepdims=True)
        cols.append(rank_ref[:, k:k + 1] + off_k.astype(jnp.int32))
    colb = [jnp.broadcast_to(c, (tc, LANES)) for c in cols]
    gateb = [jnp.broadcast_to(gate_ref[:, k:k + 1], (tc, LANES)) for k in range(TOP_K)]

    def build(p0, p1):
        for p in range(p0, p1):
            col_io = lane_tc + p * LANES
            g = jnp.zeros((tc, LANES), F32)
            for k in reversed(range(TOP_K)):
                g = jnp.where(col_io == colb[k], gateb[k], g)
            gmat[:, p * LANES:(p + 1) * LANES] = g.astype(BF16)

    def wait_chunk(c, carry):
        pltpu.make_async_copy(ys_hbm.at[pl.ds(0, ROW_TILE)], segbuf.at[slot, pl.ds(0, ROW_TILE)],
                              sems.at[slot]).wait()
        return carry

    always = TOP_K * tc
    build(0, always // LANES)
    lax.fori_loop(0, tot_ref[j], wait_chunk, 0)
    acc[...] = _dot(gmat[:, 0:always], segbuf[slot, 0:always, :])
    used = tot_ref[j] * ROW_TILE
    for c0 in range(always, SEG_ROWS, K_CHUNK):
        @pl.when(used > c0)
        def _():
            build(c0 // LANES, (c0 + K_CHUNK) // LANES)
            acc[...] = acc[...] + _dot(gmat[:, c0:c0 + K_CHUNK], segbuf[slot, c0:c0 + K_CHUNK, :])
    moe = acc[...]

    x2 = x1_ref[...] + mod_ref[0, 5:6, :] * moe
    if final:
        ms = jnp.mean(x2 * x2, axis=-1, keepdims=True)
        x2 = x2 * lax.rsqrt(ms + RMS_EPS) * fg_ref[...]
    o_ref[...] = x2


def _combine(chunk_src, chunk_dst, n_chunks, tile_chunks, e_idx, rank, gates, col_off, x1, mod, final_g, ys,
             seq_len, tc, final):
    t_tok = x1.shape[0]
    assert SEG_ROWS >= TOP_K * tc + N_EXPERTS * 2 * (ROW_TILE - 1) and (SEG_ROWS - TOP_K * tc) % K_CHUNK == 0
    kern = functools.partial(_combine_kernel, tc=tc, final=final)
    tok4 = lambda: pl.BlockSpec((tc, TOP_K), lambda i, *_: (i, 0))
    return pl.pallas_call(
        kern,
        out_shape=jax.ShapeDtypeStruct((t_tok, D_MODEL), F32),
        grid_spec=pltpu.PrefetchScalarGridSpec(
            num_scalar_prefetch=4,
            grid=(t_tok // tc,),
            in_specs=[tok4(), tok4(), tok4(),
                      pl.BlockSpec((1, 1, LANES), lambda i, *_: (i, 0, 0)),
                      pl.BlockSpec((tc, D_MODEL), lambda i, *_: (i, 0)),
                      pl.BlockSpec((1, 8, D_MODEL), lambda i, *_: ((i * tc) // seq_len, 0, 0)),
                      pl.BlockSpec((1, D_MODEL), lambda i, *_: (0, 0)),
                      pl.BlockSpec(memory_space=pl.ANY)],
            out_specs=pl.BlockSpec((tc, D_MODEL), lambda i, *_: (i, 0)),
            scratch_shapes=[pltpu.VMEM((2, SEG_ROWS, D_MODEL), BF16),
                            pltpu.VMEM((tc, SEG_ROWS), BF16),
                            pltpu.VMEM((tc, D_MODEL), F32),
                            pltpu.SemaphoreType.DMA((2,))]),
        compiler_params=_cparams(("arbitrary",)),
        name="combine",
    )(chunk_src, chunk_dst, n_chunks, tile_chunks, e_idx, rank, gates, col_off, x1, mod, final_g, ys)


def _dft_tables(n):
    j = jnp.arange(n, dtype=jnp.int32)
    ang = ((j[:, None] * j[None, :]) % n).astype(F32) * (2.0 * math.pi / n)
    scale = 1.0 / math.sqrt(n)
    return jnp.cos(ang) * scale, jnp.sin(ang) * scale


def _block_diag(blocks):
    n, r, c = blocks.shape
    rows = lax.broadcasted_iota(jnp.int32, (n * r, n * c), 0) // r
    cols = lax.broadcasted_iota(jnp.int32, (n * r, n * c), 1) // c
    return jnp.where(rows == cols, jnp.tile(blocks.reshape(n * r, c), (1, n)), 0)


def _prep_weights(w):
    depth = w["w_in"].shape[0]
    c64, s64 = _dft_tables(HEAD_DIM)
    n_heads = GROUP // HEAD_DIM
    chan_dft = jnp.concatenate([_block_diag(jnp.stack([c64] * n_heads)),
                                _block_diag(jnp.stack([s64] * n_heads))], axis=1).astype(BF16)
    avg = _block_diag(jnp.full((n_heads, HEAD_DIM, HEAD_DIM), 1.0 / HEAD_DIM, F32)).astype(BF16)
    stacked = dict(
        exp_gate_w=w["exp_gate_w"].astype(BF16).reshape(depth * N_EXPERTS, D_MODEL, D_FF),
        exp_up_w=w["exp_up_w"].astype(BF16).reshape(depth * N_EXPERTS, D_MODEL, D_FF),
        exp_down_w=w["exp_down_w"].astype(BF16).reshape(depth * N_EXPERTS, D_FF, D_MODEL),
        exp_gate_b=w["exp_gate_b"].reshape(depth * N_EXPERTS, 1, D_FF),
        exp_up_b=w["exp_up_b"].reshape(depth * N_EXPERTS, 1, D_FF),
        exp_down_b=w["exp_down_b"].reshape(depth * N_EXPERTS, 1, D_MODEL))
    layers = []
    for l in range(depth):
        layers.append(dict(
            stacked,
            norm1_g=w["norm1_g"][l].reshape(1, D_MODEL),
            norm2_g=w["norm2_g"][l].reshape(1, D_MODEL),
            w_in=w["w_in"][l].astype(BF16),
            fourier_w=w["fourier_w"][l].astype(BF16),
            conv31_w=jnp.pad(w["conv31_w"][l], ((0, 1), (0, 0))),
            conv31_b=w["conv31_b"][l].reshape(1, GROUP),
            gn_g=w["gn_g"][l].reshape(1, GROUP),
            gn_b=w["gn_b"][l].reshape(1, GROUP),
            avg=avg,
            pw_w=w["pw_w"][l].astype(BF16),
            pool_w=_block_diag(w["pool_w"][l]).astype(BF16),
            pool_scale=w["pool_scale"][l].reshape(1, GROUP),
            conv3_w=jnp.pad(w["conv3_w"][l], ((0, 5), (0, 0))),
            sconv_out_w=w["sconv_out_w"][l].astype(BF16),
            w_o=w["w_o"][l].astype(BF16),
            router_w=jnp.pad(w["router_w"][l], ((0, 0), (0, LANES - N_EXPERTS))).astype(BF16),
            router_b=jnp.pad(w["router_b"][l], (0, LANES - N_EXPERTS)).reshape(1, LANES),
        ))
    return chan_dft, layers


def _tiles(bsz, s):
    t_seq = min(512, s)
    t_dft_m = min(s, max(128, (8 * 1024 * 1024) // (bsz * GROUP * 4)))
    t_dft_k = min(2048, s // 2)
    return dict(seq=t_seq, dft_m=t_dft_m, dft_k=t_dft_k, fold=min(512, s // 2), tok=min(256, bsz * s))


def _moe(h2, logits, x1, mod, final_g, wl, layer, seq_len, tiles, final):
    t_tok = h2.shape[0]
    tok = tiles["tok"]
    n_assign = t_tok * TOP_K
    n_blocks = -(-(n_assign + N_EXPERTS * DISP_WIN) // MOE_BLOCK) + N_EXPERTS
    n_slots = n_blocks * MOE_BLOCK
    e_idx, rank, gates, base, tile_cnt, counts = _route(logits, tok)
    counts = counts[0, :N_EXPERTS].astype(jnp.int32)
    padded = jnp.where(counts > 0, (counts + DISP_WIN + MOE_BLOCK - 1) // MOE_BLOCK * MOE_BLOCK, 0)
    pend = jnp.cumsum(padded)
    pstart = pend - padded
    npad = padded - counts
    block_start = jnp.arange(n_blocks, dtype=jnp.int32) * MOE_BLOCK
    block_e = jnp.minimum(jnp.sum(block_start[:, None] >= pend[None, :], axis=1), N_EXPERTS - 1).astype(jnp.int32)
    n_used = (pend[-1:] // MOE_BLOCK).astype(jnp.int32)
    seg_lo = pstart[None, :] + base[:, 0, :N_EXPERTS]
    seg_n = tile_cnt[:, 0, :N_EXPERTS]
    win_lo = seg_lo - (seg_lo & (ROW_TILE - 1))
    win_rows = jnp.where(seg_n > 0, -(-(seg_lo + seg_n - win_lo) // ROW_TILE) * ROW_TILE, 0)
    win_dst = jnp.cumsum(win_rows, axis=1) - win_rows
    col_off = jnp.pad(win_dst - win_lo + pstart[None, :], ((0, 0), (0, LANES - N_EXPERTS)))
    col_off = col_off.reshape(-1, 1, LANES).astype(jnp.int32)
    flat = lambda a: a.reshape(-1).astype(jnp.int32)
    head = seg_lo & (F32_TILE - 1)
    dwin_lo = seg_lo - head
    dwin_rows = jnp.where(seg_n > 0, -(-(seg_lo + seg_n - dwin_lo) // F32_TILE) * F32_TILE, 0)
    dwin_src = jnp.cumsum(dwin_rows, axis=1) - dwin_rows
    dcol_off = jnp.pad(dwin_src - dwin_lo + pstart[None, :], ((0, 0), (0, LANES - N_EXPERTS)))
    dcol_off = dcol_off.reshape(-1, 1, LANES).astype(jnp.int32)
    d_chunks = dwin_rows // F32_TILE
    xs = _dispatch(flat(dwin_src), flat(dwin_lo), flat(d_chunks), flat(jnp.where(seg_n > 0, head, 0)),
                   flat(jnp.sum(d_chunks > 0, axis=1)),
                   flat(jnp.sum(jnp.maximum(d_chunks - DISP_WIN // F32_TILE, 0), axis=1)),
                   flat(jnp.sum(dwin_rows, axis=1)),
                   pend.astype(jnp.int32), npad.astype(jnp.int32), e_idx, rank, dcol_off, h2, n_slots, tok)
    ys = _experts(block_e, n_used, xs, wl, layer)
    return _combine(flat(win_lo), flat(win_dst), flat(win_rows // ROW_TILE),
                    flat(jnp.sum(win_rows // ROW_TILE, axis=1)), e_idx, rank, gates, col_off,
                    x1, mod, final_g, ys, seq_len, tok, final)


def _encoder(x, c, w, chan_dft, layers, tables, tiles=None):
    bsz, s, _ = x.shape
    tiles = tiles or _tiles(bsz, s)
    cos_t, nsin_t = tables
    depth = len(layers)
    final_g = w["final_g"].reshape(1, D_MODEL)
    ada_b = w["ada_b"].reshape(depth, 1, 6 * D_MODEL)
    for l, wl in enumerate(layers):
        mod = _ada(c, w["ada_w"], ada_b, l).reshape(bsz, 6, D_MODEL)
        mod = jnp.pad(mod, ((0, 0), (0, 2), (0, 0)))
        p, u = _inproj(x, mod, wl["norm1_g"], wl["w_in"], chan_dft, tiles["seq"])
        y = _seqdft(cos_t, nsin_t, p, _fold(p, tiles["fold"]), tiles["dft_m"], tiles["dft_k"])
        x1, h2, logits = _mix(y, u, x, mod, wl, tiles["seq"])
        x = _moe(h2.reshape(bsz * s, D_MODEL), logits.reshape(bsz * s, LANES),
                 x1.reshape(bsz * s, D_MODEL), mod, final_g, wl, l, s, tiles,
                 final=(l == depth - 1)).reshape(bsz, s, D_MODEL)
    return x


def _seq_tables(s):
    k = jnp.arange(s // 2, dtype=jnp.int32)[None, :]

    def cos_sin(j):
        ang = ((j[:, None] * k) % s).astype(F32) * (2.0 * math.pi / s)
        return jnp.cos(ang), jnp.sin(ang)

    ca, sa = cos_sin(jnp.arange(s // LANES, dtype=jnp.int32) * LANES)
    cb, sb = cos_sin(jnp.arange(LANES, dtype=jnp.int32))
    scale = 1.0 / math.sqrt(s)
    cos_t = (ca[:, None, :] * cb[None, :, :] - sa[:, None, :] * sb[None, :, :]) * scale
    nsin_t = (sa[:, None, :] * cb[None, :, :] + ca[:, None, :] * sb[None, :, :]) * (-scale)
    return cos_t.reshape(s, s // 2).astype(BF16), nsin_t.reshape(s, s // 2).astype(BF16)


def kernel(x_prompt, x_sample, c_prompt, c_sample, norm1_g, norm2_g, ada_w, ada_b, w_in, fourier_w, conv31_w, conv31_b, gn_g, gn_b, pw_w, pool_w, pool_scale, conv3_w, sconv_out_w, w_o, router_w, router_b, exp_gate_w, exp_gate_b, exp_up_w, exp_up_b, exp_down_w, exp_down_b, final_g):
    w = dict(norm1_g=norm1_g, norm2_g=norm2_g, ada_w=ada_w, ada_b=ada_b, w_in=w_in, fourier_w=fourier_w,
             conv31_w=conv31_w, conv31_b=conv31_b, gn_g=gn_g, gn_b=gn_b, pw_w=pw_w, pool_w=pool_w,
             pool_scale=pool_scale, conv3_w=conv3_w, sconv_out_w=sconv_out_w, w_o=w_o, router_w=router_w,
             router_b=router_b, exp_gate_w=exp_gate_w, exp_gate_b=exp_gate_b, exp_up_w=exp_up_w,
             exp_up_b=exp_up_b, exp_down_w=exp_down_w, exp_down_b=exp_down_b, final_g=final_g)
    chan_dft, layers = _prep_weights(w)
    y_prompt = _encoder(x_prompt, c_prompt, w, chan_dft, layers, _seq_tables(x_prompt.shape[1]))
    y_sample = _encoder(x_sample, c_sample, w, chan_dft, layers, _seq_tables(x_sample.shape[1]))
    return (y_prompt, y_sample)
```

```python
import functools
import math

import jax
import jax.numpy as jnp
from jax import lax
from jax.experimental import pallas as pl
from jax.experimental.pallas import tpu as pltpu

D_MODEL = 1024
GROUP = 256
HEAD_DIM = 64
CONV31 = 31
POOL_WINDOWS = (2, 4, 8, 16)
N_EXPERTS = 32
TOP_K = 4
D_FF = 1024
SWIGLU_ALPHA = 1.702
SWIGLU_LIMIT = 7.0
MOE_BLOCK = 512
RMS_EPS = 1e-6
GN_EPS = 1e-5
HALO = 16
LANES = 128
VMEM_LIMIT = 48 * 1024 * 1024

F32 = jnp.float32
BF16 = jnp.bfloat16


def _cparams(sem):
    return pltpu.CompilerParams(dimension_semantics=sem, vmem_limit_bytes=VMEM_LIMIT)


def _dot(a, b):
    return jnp.dot(a, b, preferred_element_type=F32)


def _split_bf16(x):
    hi = x.astype(BF16)
    lo = (x - hi.astype(F32)).astype(BF16)
    return hi, lo


def _dot_hilo(x, w_bf16):
    hi, lo = _split_bf16(x)
    return _dot(hi, w_bf16) + _dot(lo, w_bf16)


def _sigmoid(x):
    return 1.0 / (1.0 + jnp.exp(-x))


def _ada_kernel(c_ref, w_ref, b_ref, o_ref):
    c = c_ref[...]
    cs = c * _sigmoid(c)
    c_hi, c_lo = _split_bf16(cs)
    w_hi, w_lo = _split_bf16(w_ref[0])
    o_ref[...] = _dot(c_hi, w_hi) + _dot(c_lo, w_hi) + _dot(c_hi, w_lo) + b_ref[0]


def _ada(c, ada_w, ada_b, layer):
    bsz = c.shape[0]
    n_out = ada_w.shape[2]
    return pl.pallas_call(
        _ada_kernel,
        out_shape=jax.ShapeDtypeStruct((bsz, n_out), F32),
        grid=(n_out // D_MODEL,),
        in_specs=[
            pl.BlockSpec((bsz, D_MODEL), lambda j: (0, 0)),
            pl.BlockSpec((1, D_MODEL, D_MODEL), lambda j: (layer, 0, j)),
            pl.BlockSpec((1, 1, D_MODEL), lambda j: (layer, 0, j)),
        ],
        out_specs=pl.BlockSpec((bsz, D_MODEL), lambda j: (0, j)),
        compiler_params=_cparams(("arbitrary",)),
        name="ada",
    )(c, ada_w, ada_b)


def _rms_mod(x, g, scale, shift):
    ms = jnp.mean(x * x, axis=-1, keepdims=True)
    y = x * lax.rsqrt(ms + RMS_EPS) * g
    return y * (1.0 + scale) + shift


def _inproj_kernel(x_ref, mod_ref, g_ref, w_ref, cs_ref, p_ref, u_ref):
    x = x_ref[0]
    h = _rms_mod(x, g_ref[...], mod_ref[0, 1:2, :], mod_ref[0, 0:1, :]).astype(BF16)
    a = _dot(h, w_ref[:, 0:GROUP]).astype(BF16)
    p_ref[0] = _dot(a, cs_ref[...]).astype(BF16)
    for j in range(6):
        u_ref[0, :, j * GROUP:(j + 1) * GROUP] = _dot(
            h, w_ref[:, (j + 1) * GROUP:(j + 2) * GROUP]).astype(BF16)


def _inproj(x, mod, g, w_in, chan_dft, tm):
    bsz, s, _ = x.shape
    return pl.pallas_call(
        _inproj_kernel,
        out_shape=(jax.ShapeDtypeStruct((bsz, s, 2 * GROUP), BF16),
                   jax.ShapeDtypeStruct((bsz, s, 6 * GROUP), BF16)),
        grid=(bsz, s // tm),
        in_specs=[
            pl.BlockSpec((1, tm, D_MODEL), lambda b, i: (b, i, 0)),
            pl.BlockSpec((1, 8, D_MODEL), lambda b, i: (b, 0, 0)),
            pl.BlockSpec((1, D_MODEL), lambda b, i: (0, 0)),
            pl.BlockSpec((D_MODEL, 7 * GROUP), lambda b, i: (0, 0)),
            pl.BlockSpec((GROUP, 2 * GROUP), lambda b, i: (0, 0)),
        ],
        out_specs=(pl.BlockSpec((1, tm, 2 * GROUP), lambda b, i: (b, i, 0)),
                   pl.BlockSpec((1, tm, 6 * GROUP), lambda b, i: (b, i, 0))),
        compiler_params=_cparams(("arbitrary", "arbitrary")),
        name="inproj",
    )(x, mod, g, w_in, chan_dft)


def _fold_kernel(cur_ref, mir_ref, nxt_ref, e_ref, *, tf):
    kt = pl.program_id(1)
    r_io = lax.broadcasted_iota(jnp.int32, (tf, tf), 0)
    c_io = lax.broadcasted_iota(jnp.int32, (tf, tf), 1)
    flip = (c_io == tf - r_io).astype(BF16)
    rev = _dot(flip, mir_ref[0])
    cur = cur_ref[0].astype(F32)
    lane = lax.broadcasted_iota(jnp.int32, (1, 2 * GROUP), 1)
    first_tile_row0 = jnp.where(lane < GROUP, 0.0, cur[0:1, :])
    row0 = jnp.where(kt == 0, first_tile_row0, nxt_ref[0, 0:1, :].astype(F32))
    row = lax.broadcasted_iota(jnp.int32, (tf, 2 * GROUP), 0)
    rev = jnp.where(row == 0, row0, rev)
    sign = jnp.where(lax.broadcasted_iota(jnp.int32, (tf, 2 * GROUP), 1) < GROUP, 1.0, -1.0)
    e_ref[0] = (cur + sign * rev).astype(BF16)


def _fold(p, tf):
    bsz, s, _ = p.shape
    n_t = s // tf
    kern = functools.partial(_fold_kernel, tf=tf)
    return pl.pallas_call(
        kern,
        out_shape=jax.ShapeDtypeStruct((bsz, s // 2, 2 * GROUP), BF16),
        grid=(bsz, n_t // 2),
        in_specs=[
            pl.BlockSpec((1, tf, 2 * GROUP), lambda b, k: (b, k, 0)),
            pl.BlockSpec((1, tf, 2 * GROUP), lambda b, k: (b, n_t - 1 - k, 0)),
            pl.BlockSpec((1, 8, 2 * GROUP), lambda b, k: (b, ((n_t - k) % n_t) * (tf // 8), 0)),
        ],
        out_specs=pl.BlockSpec((1, tf, 2 * GROUP), lambda b, k: (b, k, 0)),
        compiler_params=_cparams(("arbitrary", "arbitrary")),
        name="fold",
    )(p, p, p)


def _seqdft_kernel(c_ref, s_ref, e_ref, mid_ref, o_ref, acc_ref, *, tm, seq_len):
    i = pl.program_id(0)
    k = pl.program_id(1)
    b = pl.program_id(2)
    contrib = _dot(c_ref[...], e_ref[0, :, 0:GROUP]) + _dot(s_ref[...], e_ref[0, :, GROUP:2 * GROUP])

    @pl.when(k == 0)
    def _():
        acc_ref[b] = contrib

    @pl.when(k > 0)
    def _():
        acc_ref[b] = acc_ref[b] + contrib

    @pl.when(k == pl.num_programs(1) - 1)
    def _():
        j = lax.broadcasted_iota(jnp.int32, (tm, GROUP), 0) + i * tm
        sign = jnp.where((j & 1) == 0, 1.0, -1.0) * (1.0 / math.sqrt(seq_len))
        o_ref[b] = (acc_ref[b] + sign * mid_ref[0, 0:1, 0:GROUP].astype(F32)).astype(BF16)


def _seqdft(cos_t, nsin_t, p, e, tm, tk):
    bsz, s, _ = p.shape
    half = s // 2
    kern = functools.partial(_seqdft_kernel, tm=tm, seq_len=s)
    return pl.pallas_call(
        kern,
        out_shape=jax.ShapeDtypeStruct((bsz, s, GROUP), BF16),
        grid=(s // tm, half // tk, bsz),
        in_specs=[
            pl.BlockSpec((tm, tk), lambda i, k, b: (i, k)),
            pl.BlockSpec((tm, tk), lambda i, k, b: (i, k)),
            pl.BlockSpec((1, tk, 2 * GROUP), lambda i, k, b: (b, k, 0)),
            pl.BlockSpec((1, 8, 2 * GROUP), lambda i, k, b: (b, half // 8, 0)),
        ],
        out_specs=pl.BlockSpec((bsz, tm, GROUP), lambda i, k, b: (0, i, 0)),
        scratch_shapes=[pltpu.VMEM((bsz, tm, GROUP), F32)],
        compiler_params=_cparams(("arbitrary", "arbitrary", "arbitrary")),
        name="seqdft",
    )(cos_t, nsin_t, e, p)


U_V, U_G, U_P, U_BG, U_CG, U_XV = (j * GROUP for j in range(6))


def _mix_kernel(y_ref, u_ref, up_ref, un_ref, x_ref, mod_ref,
                fw_ref, c31w_ref, c31b_ref, gng_ref, gnb_ref, avg_ref, pww_ref,
                poolw_ref, pscale_ref, c3w_ref, sow_ref, wo_ref, n2g_ref, rw_ref, rb_ref,
                x1_ref, h2_ref, lg_ref,
                zext, zsh, pext, s2, s4, s8, s16, qext, cat, *, tq, seq_len):
    i = pl.program_id(1)
    has_prev = (i > 0).astype(F32)
    has_next = (i < pl.num_programs(1) - 1).astype(F32)
    rows = tq + 2 * HALO

    def cols(ref, c0):
        return ref[0, :, c0:c0 + GROUP].astype(F32)

    def fill_ext(dst, fn):
        dst[0:HALO, :] = fn(up_ref) * has_prev
        dst[HALO:HALO + tq, :] = fn(u_ref)
        dst[HALO + tq:rows, :] = fn(un_ref) * has_next

    cat[:, 0:GROUP] = _dot(y_ref[0], fw_ref[...]).astype(BF16)

    fill_ext(zext, lambda r: cols(r, U_V) * _sigmoid(cols(r, U_G)))
    conv = jnp.broadcast_to(c31b_ref[...], (tq, GROUP))
    first = HALO - CONV31 // 2
    for res in range(8):
        taps = [j for j in range(CONV31) if (first + j) % 8 == res]
        span = (first + taps[-1]) // 8 * 8 + tq
        zsh[0:span, :] = zext[pl.ds(res, span), :]
        for j in taps:
            q = (first + j) // 8 * 8
            conv = conv + c31w_ref[j:j + 1, :] * zsh[q:q + tq, :]
    mu = _dot_hilo(conv, avg_ref[...])
    dev = conv - mu
    var = _dot_hilo(dev * dev, avg_ref[...])
    zn = dev * lax.rsqrt(var + GN_EPS) * gng_ref[...] + gnb_ref[...]
    cat[:, GROUP:2 * GROUP] = _dot((zn * _sigmoid(zn)).astype(BF16), pww_ref[...]).astype(BF16)

    fill_ext(pext, lambda r: cols(r, U_P))
    n = rows - 8
    zeros8 = jnp.zeros((8, GROUP), F32)
    s2[0:n, :] = pext[0:n, :] + pext[pl.ds(1, n), :]
    s2[n:rows, :] = zeros8
    s4[0:n, :] = s2[0:n, :] + s2[pl.ds(2, n), :]
    s4[n:rows, :] = zeros8
    s8[0:n, :] = s4[0:n, :] + s4[pl.ds(4, n), :]
    s8[n:rows, :] = zeros8
    s16[0:n, :] = s8[0:n, :] + s8[pl.ds(8, n), :]
    lane = lax.broadcasted_iota(jnp.int32, (tq, GROUP), 1)
    pos = lax.broadcasted_iota(jnp.int32, (tq, GROUP), 0) + i * tq
    win = jnp.where(lane < HEAD_DIM, s2[pl.ds(HALO - 1, tq), :],
                    jnp.where(lane < 2 * HEAD_DIM, s4[pl.ds(HALO - 2, tq), :],
                              jnp.where(lane < 3 * HEAD_DIM, s8[pl.ds(HALO - 4, tq), :],
                                        s16[pl.ds(HALO - 8, tq), :])))
    half = jnp.where(lane < HEAD_DIM, 1,
                     jnp.where(lane < 2 * HEAD_DIM, 2, jnp.where(lane < 3 * HEAD_DIM, 4, 8)))
    cnt = jnp.minimum(pos + half, seq_len) - jnp.maximum(pos - half, 0)
    dpool = win / cnt.astype(F32) - pext[HALO:HALO + tq, :]
    cat[:, 2 * GROUP:3 * GROUP] = (_dot(dpool.astype(BF16), poolw_ref[...]) * pscale_ref[...]).astype(BF16)

    fill_ext(qext, lambda r: cols(r, U_CG) * cols(r, U_XV))
    c3 = (c3w_ref[0:1, :] * qext[pl.ds(HALO - 1, tq), :]
          + c3w_ref[1:2, :] * qext[HALO:HALO + tq, :]
          + c3w_ref[2:3, :] * qext[pl.ds(HALO + 1, tq), :])
    cat[:, 3 * GROUP:4 * GROUP] = _dot((cols(u_ref, U_BG) * c3).astype(BF16), sow_ref[...]).astype(BF16)

    mixed = _dot(cat[...], wo_ref[...])
    x1 = x_ref[0] + mod_ref[0, 2:3, :] * mixed
    x1_ref[0] = x1
    h2 = _rms_mod(x1, n2g_ref[...], mod_ref[0, 4:5, :], mod_ref[0, 3:4, :])
    h2_ref[0] = h2
    lg_ref[0] = _dot(h2.astype(BF16), rw_ref[...]) + rb_ref[...]


def _mix(y, u, x, mod, wl, tq):
    bsz, s, _ = x.shape
    rows = tq + 2 * HALO
    hb = tq // HALO
    n_hblk = s // HALO
    full = lambda shape: pl.BlockSpec(shape, lambda b, i: tuple(0 for _ in shape))
    kern = functools.partial(_mix_kernel, tq=tq, seq_len=s)
    return pl.pallas_call(
        kern,
        out_shape=(jax.ShapeDtypeStruct((bsz, s, D_MODEL), F32),
                   jax.ShapeDtypeStruct((bsz, s, D_MODEL), F32),
                   jax.ShapeDtypeStruct((bsz, s, LANES), F32)),
        grid=(bsz, s // tq),
        in_specs=[
            pl.BlockSpec((1, tq, GROUP), lambda b, i: (b, i, 0)),
            pl.BlockSpec((1, tq, 6 * GROUP), lambda b, i: (b, i, 0)),
            pl.BlockSpec((1, HALO, 6 * GROUP), lambda b, i: (b, jnp.maximum(i * hb - 1, 0), 0)),
            pl.BlockSpec((1, HALO, 6 * GROUP), lambda b, i: (b, jnp.minimum((i + 1) * hb, n_hblk - 1), 0)),
            pl.BlockSpec((1, tq, D_MODEL), lambda b, i: (b, i, 0)),
            pl.BlockSpec((1, 8, D_MODEL), lambda b, i: (b, 0, 0)),
            full((GROUP, GROUP)),
            full((32, GROUP)),
            full((1, GROUP)),
            full((1, GROUP)),
            full((1, GROUP)),
            full((GROUP, GROUP)),
            full((GROUP, GROUP)),
            full((GROUP, GROUP)),
            full((1, GROUP)),
            full((8, GROUP)),
            full((GROUP, GROUP)),
            full((D_MODEL, D_MODEL)),
            full((1, D_MODEL)),
            full((D_MODEL, LANES)),
            full((1, LANES)),
        ],
        out_specs=(pl.BlockSpec((1, tq, D_MODEL), lambda b, i: (b, i, 0)),
                   pl.BlockSpec((1, tq, D_MODEL), lambda b, i: (b, i, 0)),
                   pl.BlockSpec((1, tq, LANES), lambda b, i: (b, i, 0))),
        scratch_shapes=[pltpu.VMEM((rows, GROUP), F32) for _ in range(8)]
        + [pltpu.VMEM((tq, D_MODEL), BF16)],
        compiler_params=_cparams(("arbitrary", "arbitrary")),
        name="mix",
    )(y, u, u, u, x, mod, wl["fourier_w"], wl["conv31_w"], wl["conv31_b"], wl["gn_g"], wl["gn_b"],
      wl["avg"], wl["pw_w"], wl["pool_w"], wl["pool_scale"], wl["conv3_w"], wl["sconv_out_w"],
      wl["w_o"], wl["norm2_g"], wl["router_w"], wl["router_b"])


def _route_kernel(lg_ref, e_ref, rank_ref, gate_ref, base_ref, tcnt_ref, cnt_ref, carry, *, tr, sub):
    i = pl.program_id(0)

    @pl.when(i == 0)
    def _():
        carry[...] = jnp.zeros_like(carry)

    lane = lax.broadcasted_iota(jnp.int32, (tr, LANES), 1).astype(F32)
    neg = jnp.float32(-jnp.inf)
    work = jnp.where(lane < N_EXPERTS, lg_ref[...], neg)
    vals, idxs, hots = [], [], []
    for _ in range(TOP_K):
        m = jnp.max(work, axis=1, keepdims=True)
        idx = jnp.min(jnp.where(work == m, lane, float(LANES)), axis=1, keepdims=True)
        hot = lane == idx
        vals.append(m)
        idxs.append(idx.astype(jnp.int32))
        hots.append(hot)
        work = jnp.where(hot, neg, work)
    exps = [jnp.exp(v - vals[0]) for v in vals]
    denom = exps[0] + exps[1] + exps[2] + exps[3]
    member = sum(h.astype(F32) for h in hots)
    r_io = lax.broadcasted_iota(jnp.int32, (tr, tr), 0)
    c_io = lax.broadcasted_iota(jnp.int32, (tr, tr), 1)
    lower = (c_io < r_io).astype(BF16)
    before = _dot(lower, member.astype(BF16)) + carry[...]
    col = lax.broadcasted_iota(jnp.int32, (tr, TOP_K), 1)
    e_out = jnp.zeros((tr, TOP_K), jnp.int32)
    r_out = jnp.zeros((tr, TOP_K), jnp.int32)
    g_out = jnp.zeros((tr, TOP_K), F32)
    for k in range(TOP_K):
        rk = jnp.sum(jnp.where(hots[k], before, 0.0), axis=1, keepdims=True).astype(jnp.int32)
        e_out = jnp.where(col == k, idxs[k], e_out)
        r_out = jnp.where(col == k, rk, r_out)
        g_out = jnp.where(col == k, exps[k] / denom, g_out)
    e_ref[...] = e_out
    rank_ref[...] = r_out
    gate_ref[...] = g_out
    running = carry[...]
    for q in range(tr // sub):
        cnt_q = jnp.sum(member[q * sub:(q + 1) * sub, :], axis=0, keepdims=True)
        base_ref[q] = running.astype(jnp.int32)
        tcnt_ref[q] = cnt_q.astype(jnp.int32)
        running = running + cnt_q
    carry[...] = running
    cnt_ref[...] = running


def _route(logits, sub):
    t_tok = logits.shape[0]
    tr = min(2 * sub, t_tok)
    n_steps = t_tok // tr
    n_tiles = t_tok // sub
    kern = functools.partial(_route_kernel, tr=tr, sub=sub)
    tile = lambda w: pl.BlockSpec((tr, w), lambda i: (i, 0))
    per_tile = pl.BlockSpec((tr // sub, 1, LANES), lambda i: (i, 0, 0))
    return pl.pallas_call(
        kern,
        out_shape=(jax.ShapeDtypeStruct((t_tok, TOP_K), jnp.int32),
                   jax.ShapeDtypeStruct((t_tok, TOP_K), jnp.int32),
                   jax.ShapeDtypeStruct((t_tok, TOP_K), F32),
                   jax.ShapeDtypeStruct((n_tiles, 1, LANES), jnp.int32),
                   jax.ShapeDtypeStruct((n_tiles, 1, LANES), jnp.int32),
                   jax.ShapeDtypeStruct((1, LANES), F32)),
        grid=(n_steps,),
        in_specs=[tile(LANES)],
        out_specs=(tile(TOP_K), tile(TOP_K), tile(TOP_K), per_tile, per_tile,
                   pl.BlockSpec((1, LANES), lambda i: (0, 0))),
        scratch_shapes=[pltpu.VMEM((1, LANES), F32)],
        compiler_params=_cparams(("arbitrary",)),
        name="route",
    )(logits)


F32_TILE = 8
DISP_ROWS = TOP_K * 256 + N_EXPERTS * 2 * F32_TILE
CARRY_ROWS = N_EXPERTS * F32_TILE
DISP_WIN = 64
XS_COLS = D_MODEL // 2


def _pack_pairs(x):
    lo = lax.bitcast_convert_type(x[:, :XS_COLS], jnp.uint32)
    hi = lax.bitcast_convert_type(x[:, XS_COLS:], jnp.uint32)
    return (hi & jnp.uint32(0xFFFF0000)) | (lo >> 16)


def _unpack_pairs(w):
    lo = lax.bitcast_convert_type(w << 16, F32).astype(BF16)
    hi = lax.bitcast_convert_type(w & jnp.uint32(0xFFFF0000), F32).astype(BF16)
    return lo, hi


def _dispatch_kernel(src_ref, dst_ref, nch_ref, head_ref, nwin_ref, nextra_ref, used_ref, pend_ref, npad_ref,
                     e_ref, rank_ref, off_ref, h_ref, xs_hbm, outbuf, carry, zeros, sems, sem_z, *, td):
    i = pl.program_id(0)
    n_steps = pl.num_programs(0)
    slot = i % 2

    def zero_blk(row):
        return pltpu.make_async_copy(zeros, xs_hbm.at[pl.ds(pl.multiple_of(row, MOE_BLOCK), MOE_BLOCK)], sem_z)

    @pl.when(i == 0)
    def _():
        zeros[...] = jnp.zeros_like(zeros)
        carry[...] = jnp.zeros_like(carry)
        outbuf[...] = jnp.zeros_like(outbuf)
        first_unused = pend_ref[N_EXPERTS - 1] // MOE_BLOCK
        n_blocks = xs_hbm.shape[0] // MOE_BLOCK

        def start(e, c):
            @pl.when(npad_ref[e] > 0)
            def _():
                zero_blk(pend_ref[e] - MOE_BLOCK).start()

            @pl.when(npad_ref[e] > MOE_BLOCK)
            def _():
                zero_blk(pend_ref[e] - 2 * MOE_BLOCK).start()
            return c

        def wait(e, c):
            @pl.when(npad_ref[e] > 0)
            def _():
                zero_blk(pend_ref[e] - MOE_BLOCK).wait()

            @pl.when(npad_ref[e] > MOE_BLOCK)
            def _():
                zero_blk(pend_ref[e] - 2 * MOE_BLOCK).wait()
            return c

        def start_tail(b, c):
            zero_blk(b * MOE_BLOCK).start()
            return c

        def wait_tail(b, c):
            zero_blk(b * MOE_BLOCK).wait()
            return c

        lax.fori_loop(0, N_EXPERTS, start, 0)
        lax.fori_loop(first_unused, n_blocks, start_tail, 0)
        lax.fori_loop(0, N_EXPERTS, wait, 0)
        lax.fori_loop(first_unused, n_blocks, wait_tail, 0)

    def window_copies(tile, sl, act):
        for e in range(N_EXPERTS):
            n_ch = nch_ref[tile * N_EXPERTS + e]
            src0 = pl.multiple_of(src_ref[tile * N_EXPERTS + e], F32_TILE)
            dst0 = pl.multiple_of(dst_ref[tile * N_EXPERTS + e], F32_TILE)

            @pl.when(n_ch > 0)
            def _():
                act(pltpu.make_async_copy(outbuf.at[sl, pl.ds(src0, DISP_WIN)],
                                          xs_hbm.at[pl.ds(dst0, DISP_WIN)], sems.at[sl]))

            def per_chunk(ci, c2):
                src = pl.multiple_of(src0 + ci * F32_TILE, F32_TILE)
                dst = pl.multiple_of(dst0 + ci * F32_TILE, F32_TILE)
                act(pltpu.make_async_copy(outbuf.at[sl, pl.ds(src, F32_TILE)],
                                          xs_hbm.at[pl.ds(dst, F32_TILE)], sems.at[sl]))
                return c2

            lax.fori_loop(DISP_WIN // F32_TILE, n_ch, per_chunk, 0)

    lane_td = lax.broadcasted_iota(jnp.int32, (td, LANES), 1)
    off_f = off_ref[0].astype(F32)
    pos_cols = jnp.full((td, LANES), -1.0, F32)
    for k in range(TOP_K):
        off_k = jnp.sum(jnp.where(lane_td == e_ref[:, k:k + 1], off_f, 0.0), axis=1, keepdims=True)
        pos_cols = jnp.where(lane_td == k, rank_ref[:, k:k + 1].astype(F32) + off_k, pos_cols)
    pos_t = jnp.transpose(pos_cols).astype(jnp.int32)
    pos = [jnp.broadcast_to(pos_t[k:k + 1, :], (GROUP, td)) for k in range(TOP_K)]
    row_io = lax.broadcasted_iota(jnp.int32, (GROUP, td), 0)
    h_bf = h_ref[...].astype(BF16)
    def select_rows(rc):
        r = row_io + rc * GROUP
        tok = jnp.zeros((GROUP, td), F32)
        for k in range(TOP_K):
            tok = jnp.where(r == pos[k], 1.0, tok)
        outbuf[slot, rc * GROUP:(rc + 1) * GROUP, :] = _pack_pairs(_dot(tok.astype(BF16), h_bf))

    for rc in range(DISP_ROWS // GROUP):
        if rc * GROUP < TOP_K * td:
            select_rows(rc)
        else:
            pl.when(used_ref[i] > rc * GROUP)(functools.partial(select_rows, rc))

    sub_io = lax.broadcasted_iota(jnp.int32, (F32_TILE, XS_COLS), 0)
    for e in range(N_EXPERTS):
        es = slice(e * F32_TILE, (e + 1) * F32_TILE)
        n_ch = nch_ref[i * N_EXPERTS + e]
        first = pl.multiple_of(src_ref[i * N_EXPERTS + e], F32_TILE)
        outbuf[slot, pl.ds(first, F32_TILE), :] = jnp.where(
            sub_io < head_ref[i * N_EXPERTS + e], carry[es, :], outbuf[slot, pl.ds(first, F32_TILE), :])
        last = pl.multiple_of(first + jnp.maximum(n_ch - 1, 0) * F32_TILE, F32_TILE)
        carry[es, :] = jnp.where(n_ch > 0, outbuf[slot, pl.ds(last, F32_TILE), :], carry[es, :])

    def wait_copies(tile, sl):
        def wait_window(c, carry_):
            pltpu.make_async_copy(outbuf.at[sl, pl.ds(0, DISP_WIN)], xs_hbm.at[pl.ds(0, DISP_WIN)],
                                  sems.at[sl]).wait()
            return carry_

        def wait_chunk(c, carry_):
            pltpu.make_async_copy(outbuf.at[sl, pl.ds(0, F32_TILE)], xs_hbm.at[pl.ds(0, F32_TILE)],
                                  sems.at[sl]).wait()
            return carry_

        lax.fori_loop(0, nwin_ref[tile], wait_window, 0)
        lax.fori_loop(0, nextra_ref[tile], wait_chunk, 0)

    @pl.when(i > 0)
    def _():
        wait_copies(i - 1, 1 - slot)

    window_copies(i, slot, lambda cp: cp.start())

    @pl.when(i == n_steps - 1)
    def _():
        wait_copies(i, slot)


def _dispatch(chunk_src, chunk_dst, n_chunks, head, n_windows, n_extra, used_rows, pend, npad, e_idx, rank,
              col_off, h2, n_slots, td):
    t_tok = h2.shape[0]
    assert DISP_ROWS >= TOP_K * td + N_EXPERTS * 2 * (F32_TILE - 1) + F32_TILE and DISP_ROWS % GROUP == 0
    kern = functools.partial(_dispatch_kernel, td=td)
    tok4 = lambda: pl.BlockSpec((td, TOP_K), lambda i, *_: (i, 0))
    return pl.pallas_call(
        kern,
        out_shape=jax.ShapeDtypeStruct((n_slots, XS_COLS), jnp.uint32),
        grid_spec=pltpu.PrefetchScalarGridSpec(
            num_scalar_prefetch=9,
            grid=(t_tok // td,),
            in_specs=[tok4(), tok4(),
                      pl.BlockSpec((1, 1, LANES), lambda i, *_: (i, 0, 0)),
                      pl.BlockSpec((td, D_MODEL), lambda i, *_: (i, 0))],
            out_specs=pl.BlockSpec(memory_space=pl.ANY),
            scratch_shapes=[pltpu.VMEM((2, DISP_ROWS + DISP_WIN, XS_COLS), jnp.uint32),
                            pltpu.VMEM((CARRY_ROWS, XS_COLS), jnp.uint32),
                            pltpu.VMEM((MOE_BLOCK, XS_COLS), jnp.uint32),
                            pltpu.SemaphoreType.DMA((2,)),
                            pltpu.SemaphoreType.DMA]),
        compiler_params=_cparams(("arbitrary",)),
        name="dispatch",
    )(chunk_src, chunk_dst, n_chunks, head, n_windows, n_extra, used_rows, pend, npad, e_idx, rank, col_off, h2)


def _experts_kernel(be_ref, nu_ref, xs_ref, wg_ref, bg_ref, wu_ref, bu_ref, wd_ref, bd_ref, o_ref, act):
    i = pl.program_id(0)

    @pl.when(i < nu_ref[0])
    def _():
        x_lo, x_hi = _unpack_pairs(xs_ref[...])

        def proj(w_ref, fs):
            return _dot(x_lo, w_ref[0, :XS_COLS, fs]) + _dot(x_hi, w_ref[0, XS_COLS:, fs])

        for f in range(D_FF // GROUP):
            fs = slice(f * GROUP, (f + 1) * GROUP)
            gt = jnp.minimum(proj(wg_ref, fs) + bg_ref[0, :, fs], SWIGLU_LIMIT)
            up = jnp.clip(proj(wu_ref, fs) + bu_ref[0, :, fs], -SWIGLU_LIMIT, SWIGLU_LIMIT)
            act[:, fs] = ((up + 1.0) * (gt * _sigmoid(SWIGLU_ALPHA * gt))).astype(BF16)
        o_ref[...] = (_dot(act[...], wd_ref[0]) + bd_ref[0]).astype(BF16)

    @pl.when(i >= nu_ref[0])
    def _():
        o_ref[...] = jnp.zeros_like(o_ref)


def _experts(block_e, n_used, xs, wl, layer):
    n_slots = xs.shape[0]
    n_blocks = n_slots // MOE_BLOCK

    def expert(i, be, nu):
        return layer * N_EXPERTS + be[jnp.minimum(i, nu[0] - 1)]

    wspec = lambda: pl.BlockSpec((1, D_MODEL, D_FF), lambda i, be, nu: (expert(i, be, nu), 0, 0))
    bspec = lambda: pl.BlockSpec((1, 1, D_FF), lambda i, be, nu: (expert(i, be, nu), 0, 0))
    return pl.pallas_call(
        _experts_kernel,
        out_shape=jax.ShapeDtypeStruct((n_slots, D_MODEL), BF16),
        grid_spec=pltpu.PrefetchScalarGridSpec(
            num_scalar_prefetch=2,
            grid=(n_blocks,),
            in_specs=[pl.BlockSpec((MOE_BLOCK, XS_COLS), lambda i, be, nu: (jnp.minimum(i, nu[0] - 1), 0)),
                      wspec(), bspec(), wspec(), bspec(), wspec(), bspec()],
            out_specs=pl.BlockSpec((MOE_BLOCK, D_MODEL), lambda i, be, nu: (i, 0)),
            scratch_shapes=[pltpu.VMEM((MOE_BLOCK, D_FF), BF16)]),
        compiler_params=_cparams(("arbitrary",)),
        name="experts",
    )(block_e, n_used, xs, wl["exp_gate_w"], wl["exp_gate_b"], wl["exp_up_w"], wl["exp_up_b"],
      wl["exp_down_w"], wl["exp_down_b"])


ROW_TILE = 16
SEG_ROWS = TOP_K * 256 + N_EXPERTS * 2 * ROW_TILE
K_CHUNK = 512
BIG_CHUNK = 2 * ROW_TILE


def _combine_kernel(src_ref, dst_ref, nch_ref, tot_ref, big_ref, small_ref, e_ref, rank_ref, gate_ref, off_ref, x1_ref, mod_ref,
                    fg_ref, ys_hbm, o_ref, segbuf, gmat, acc, sems, *, tc, final):
    j = pl.program_id(0)
    n_tiles = pl.num_programs(0)
    slot = j % 2

    def chunks(tile, sl, act):
        def per_expert(e, c):
            src0 = src_ref[tile * N_EXPERTS + e]
            dst0 = dst_ref[tile * N_EXPERTS + e]
            n_ch = nch_ref[tile * N_EXPERTS + e]

            def per_pair(ci, c2):
                src = pl.multiple_of(src0 + ci * BIG_CHUNK, ROW_TILE)
                dst = pl.multiple_of(dst0 + ci * BIG_CHUNK, ROW_TILE)
                act(pltpu.make_async_copy(ys_hbm.at[pl.ds(src, BIG_CHUNK)],
                                          segbuf.at[sl, pl.ds(dst, BIG_CHUNK)], sems.at[sl]))
                return c2

            lax.fori_loop(0, n_ch // 2, per_pair, 0)

            @pl.when(n_ch % 2 == 1)
            def _():
                src = pl.multiple_of(src0 + (n_ch - 1) * ROW_TILE, ROW_TILE)
                dst = pl.multiple_of(dst0 + (n_ch - 1) * ROW_TILE, ROW_TILE)
                act(pltpu.make_async_copy(ys_hbm.at[pl.ds(src, ROW_TILE)],
                                          segbuf.at[sl, pl.ds(dst, ROW_TILE)], sems.at[sl]))
            return c

        lax.fori_loop(0, N_EXPERTS, per_expert, 0)

    @pl.when(j == 0)
    def _():
        segbuf[...] = jnp.zeros_like(segbuf)
        chunks(0, 0, lambda cp: cp.start())

    @pl.when(j + 1 < n_tiles)
    def _():
        chunks(j + 1, 1 - slot, lambda cp: cp.start())

    lane_tc = lax.broadcasted_iota(jnp.int32, (tc, LANES), 1)
    off_f = off_ref[0].astype(F32)
    cols = []
    for k in range(TOP_K):
        off_k = jnp.sum(jnp.where(lane_tc == e_ref[:, k:k + 1], off_f, 0.0), axis=1, keepdims=True)
        cols.append(rank_ref[:, k:k + 1] + off_k.astype(jnp.int32))
    colb = [jnp.broadcast_to(c, (tc, LANES)) for c in cols]
    gateb = [jnp.broadcast_to(gate_ref[:, k:k + 1], (tc, LANES)) for k in range(TOP_K)]

    def build(p0, p1):
        for p in range(p0, p1):
            col_io = lane_tc + p * LANES
            g = jnp.zeros((tc, LANES), F32)
            for k in reversed(range(TOP_K)):
                g = jnp.where(col_io == colb[k], gateb[k], g)
            gmat[:, p * LANES:(p + 1) * LANES] = g.astype(BF16)

    def wait_big(c, carry):
        pltpu.make_async_copy(ys_hbm.at[pl.ds(0, BIG_CHUNK)], segbuf.at[slot, pl.ds(0, BIG_CHUNK)],
                              sems.at[slot]).wait()
        return carry

    def wait_small(c, carry):
        pltpu.make_async_copy(ys_hbm.at[pl.ds(0, ROW_TILE)], segbuf.at[slot, pl.ds(0, ROW_TILE)],
                              sems.at[slot]).wait()
        return carry

    always = TOP_K * tc
    build(0, always // LANES)
    lax.fori_loop(0, big_ref[j], wait_big, 0)
    lax.fori_loop(0, small_ref[j], wait_small, 0)
    acc[...] = _dot(gmat[:, 0:always], segbuf[slot, 0:always, :])
    used = tot_ref[j] * ROW_TILE
    for c0 in range(always, SEG_ROWS, K_CHUNK):
        @pl.when(used > c0)
        def _():
            build(c0 // LANES, (c0 + K_CHUNK) // LANES)
            acc[...] = acc[...] + _dot(gmat[:, c0:c0 + K_CHUNK], segbuf[slot, c0:c0 + K_CHUNK, :])
    moe = acc[...]

    x2 = x1_ref[...] + mod_ref[0, 5:6, :] * moe
    if final:
        ms = jnp.mean(x2 * x2, axis=-1, keepdims=True)
        x2 = x2 * lax.rsqrt(ms + RMS_EPS) * fg_ref[...]
    o_ref[...] = x2


def _combine(chunk_src, chunk_dst, n_chunks, tile_chunks, tile_big, tile_small, e_idx, rank, gates, col_off, x1, mod, final_g, ys,
             seq_len, tc, final):
    t_tok = x1.shape[0]
    assert SEG_ROWS >= TOP_K * tc + N_EXPERTS * 2 * (ROW_TILE - 1) and (SEG_ROWS - TOP_K * tc) % K_CHUNK == 0
    kern = functools.partial(_combine_kernel, tc=tc, final=final)
    tok4 = lambda: pl.BlockSpec((tc, TOP_K), lambda i, *_: (i, 0))
    return pl.pallas_call(
        kern,
        out_shape=jax.ShapeDtypeStruct((t_tok, D_MODEL), F32),
        grid_spec=pltpu.PrefetchScalarGridSpec(
            num_scalar_prefetch=6,
            grid=(t_tok // tc,),
            in_specs=[tok4(), tok4(), tok4(),
                      pl.BlockSpec((1, 1, LANES), lambda i, *_: (i, 0, 0)),
                      pl.BlockSpec((tc, D_MODEL), lambda i, *_: (i, 0)),
                      pl.BlockSpec((1, 8, D_MODEL), lambda i, *_: ((i * tc) // seq_len, 0, 0)),
                      pl.BlockSpec((1, D_MODEL), lambda i, *_: (0, 0)),
                      pl.BlockSpec(memory_space=pl.ANY)],
            out_specs=pl.BlockSpec((tc, D_MODEL), lambda i, *_: (i, 0)),
            scratch_shapes=[pltpu.VMEM((2, SEG_ROWS, D_MODEL), BF16),
                            pltpu.VMEM((tc, SEG_ROWS), BF16),
                            pltpu.VMEM((tc, D_MODEL), F32),
                            pltpu.SemaphoreType.DMA((2,))]),
        compiler_params=_cparams(("arbitrary",)),
        name="combine",
    )(chunk_src, chunk_dst, n_chunks, tile_chunks, tile_big, tile_small, e_idx, rank, gates, col_off, x1, mod,
      final_g, ys)


def _dft_tables(n):
    j = jnp.arange(n, dtype=jnp.int32)
    ang = ((j[:, None] * j[None, :]) % n).astype(F32) * (2.0 * math.pi / n)
    scale = 1.0 / math.sqrt(n)
    return jnp.cos(ang) * scale, jnp.sin(ang) * scale


def _block_diag(blocks):
    n, r, c = blocks.shape
    rows = lax.broadcasted_iota(jnp.int32, (n * r, n * c), 0) // r
    cols = lax.broadcasted_iota(jnp.int32, (n * r, n * c), 1) // c
    return jnp.where(rows == cols, jnp.tile(blocks.reshape(n * r, c), (1, n)), 0)


def _prep_weights(w):
    depth = w["w_in"].shape[0]
    c64, s64 = _dft_tables(HEAD_DIM)
    n_heads = GROUP // HEAD_DIM
    chan_dft = jnp.concatenate([_block_diag(jnp.stack([c64] * n_heads)),
                                _block_diag(jnp.stack([s64] * n_heads))], axis=1).astype(BF16)
    avg = _block_diag(jnp.full((n_heads, HEAD_DIM, HEAD_DIM), 1.0 / HEAD_DIM, F32)).astype(BF16)
    stacked = dict(
        exp_gate_w=w["exp_gate_w"].astype(BF16).reshape(depth * N_EXPERTS, D_MODEL, D_FF),
        exp_up_w=w["exp_up_w"].astype(BF16).reshape(depth * N_EXPERTS, D_MODEL, D_FF),
        exp_down_w=w["exp_down_w"].astype(BF16).reshape(depth * N_EXPERTS, D_FF, D_MODEL),
        exp_gate_b=w["exp_gate_b"].reshape(depth * N_EXPERTS, 1, D_FF),
        exp_up_b=w["exp_up_b"].reshape(depth * N_EXPERTS, 1, D_FF),
        exp_down_b=w["exp_down_b"].reshape(depth * N_EXPERTS, 1, D_MODEL))
    layers = []
    for l in range(depth):
        layers.append(dict(
            stacked,
            norm1_g=w["norm1_g"][l].reshape(1, D_MODEL),
            norm2_g=w["norm2_g"][l].reshape(1, D_MODEL),
            w_in=w["w_in"][l].astype(BF16),
            fourier_w=w["fourier_w"][l].astype(BF16),
            conv31_w=jnp.pad(w["conv31_w"][l], ((0, 1), (0, 0))),
            conv31_b=w["conv31_b"][l].reshape(1, GROUP),
            gn_g=w["gn_g"][l].reshape(1, GROUP),
            gn_b=w["gn_b"][l].reshape(1, GROUP),
            avg=avg,
            pw_w=w["pw_w"][l].astype(BF16),
            pool_w=_block_diag(w["pool_w"][l]).astype(BF16),
            pool_scale=w["pool_scale"][l].reshape(1, GROUP),
            conv3_w=jnp.pad(w["conv3_w"][l], ((0, 5), (0, 0))),
            sconv_out_w=w["sconv_out_w"][l].astype(BF16),
            w_o=w["w_o"][l].astype(BF16),
            router_w=jnp.pad(w["router_w"][l], ((0, 0), (0, LANES - N_EXPERTS))).astype(BF16),
            router_b=jnp.pad(w["router_b"][l], (0, LANES - N_EXPERTS)).reshape(1, LANES),
        ))
    return chan_dft, layers


def _tiles(bsz, s):
    t_seq = min(512, s)
    t_dft_m = min(s, max(128, (8 * 1024 * 1024) // (bsz * GROUP * 4)))
    t_dft_k = min(2048, s // 2)
    return dict(seq=t_seq, dft_m=t_dft_m, dft_k=t_dft_k, fold=min(512, s // 2), tok=min(256, bsz * s))


def _moe(h2, logits, x1, mod, final_g, wl, layer, seq_len, tiles, final):
    t_tok = h2.shape[0]
    tok = tiles["tok"]
    n_assign = t_tok * TOP_K
    n_blocks = -(-(n_assign + N_EXPERTS * DISP_WIN) // MOE_BLOCK) + N_EXPERTS
    n_slots = n_blocks * MOE_BLOCK
    e_idx, rank, gates, base, tile_cnt, counts = _route(logits, tok)
    counts = counts[0, :N_EXPERTS].astype(jnp.int32)
    padded = jnp.where(counts > 0, (counts + DISP_WIN + MOE_BLOCK - 1) // MOE_BLOCK * MOE_BLOCK, 0)
    pend = jnp.cumsum(padded)
    pstart = pend - padded
    npad = padded - counts
    block_start = jnp.arange(n_blocks, dtype=jnp.int32) * MOE_BLOCK
    block_e = jnp.minimum(jnp.sum(block_start[:, None] >= pend[None, :], axis=1), N_EXPERTS - 1).astype(jnp.int32)
    n_used = (pend[-1:] // MOE_BLOCK).astype(jnp.int32)
    seg_lo = pstart[None, :] + base[:, 0, :N_EXPERTS]
    seg_n = tile_cnt[:, 0, :N_EXPERTS]
    win_lo = seg_lo - (seg_lo & (ROW_TILE - 1))
    win_rows = jnp.where(seg_n > 0, -(-(seg_lo + seg_n - win_lo) // ROW_TILE) * ROW_TILE, 0)
    win_dst = jnp.cumsum(win_rows, axis=1) - win_rows
    col_off = jnp.pad(win_dst - win_lo + pstart[None, :], ((0, 0), (0, LANES - N_EXPERTS)))
    col_off = col_off.reshape(-1, 1, LANES).astype(jnp.int32)
    flat = lambda a: a.reshape(-1).astype(jnp.int32)
    head = seg_lo & (F32_TILE - 1)
    dwin_lo = seg_lo - head
    dwin_rows = jnp.where(seg_n > 0, -(-(seg_lo + seg_n - dwin_lo) // F32_TILE) * F32_TILE, 0)
    dwin_src = jnp.cumsum(dwin_rows, axis=1) - dwin_rows
    dcol_off = jnp.pad(dwin_src - dwin_lo + pstart[None, :], ((0, 0), (0, LANES - N_EXPERTS)))
    dcol_off = dcol_off.reshape(-1, 1, LANES).astype(jnp.int32)
    d_chunks = dwin_rows // F32_TILE
    xs = _dispatch(flat(dwin_src), flat(dwin_lo), flat(d_chunks), flat(jnp.where(seg_n > 0, head, 0)),
                   flat(jnp.sum(d_chunks > 0, axis=1)),
                   flat(jnp.sum(jnp.maximum(d_chunks - DISP_WIN // F32_TILE, 0), axis=1)),
                   flat(jnp.sum(dwin_rows, axis=1)),
                   pend.astype(jnp.int32), npad.astype(jnp.int32), e_idx, rank, dcol_off, h2, n_slots, tok)
    ys = _experts(block_e, n_used, xs, wl, layer)
    w_chunks = win_rows // ROW_TILE
    return _combine(flat(win_lo), flat(win_dst), flat(w_chunks), flat(jnp.sum(w_chunks, axis=1)),
                    flat(jnp.sum(w_chunks // 2, axis=1)), flat(jnp.sum(w_chunks % 2, axis=1)),
                    e_idx, rank, gates, col_off, x1, mod, final_g, ys, seq_len, tok, final)


def _encoder(x, c, w, chan_dft, layers, tables, tiles=None):
    bsz, s, _ = x.shape
    tiles = tiles or _tiles(bsz, s)
    cos_t, nsin_t = tables
    depth = len(layers)
    final_g = w["final_g"].reshape(1, D_MODEL)
    ada_b = w["ada_b"].reshape(depth, 1, 6 * D_MODEL)
    for l, wl in enumerate(layers):
        mod = _ada(c, w["ada_w"], ada_b, l).reshape(bsz, 6, D_MODEL)
        mod = jnp.pad(mod, ((0, 0), (0, 2), (0, 0)))
        p, u = _inproj(x, mod, wl["norm1_g"], wl["w_in"], chan_dft, tiles["seq"])
        y = _seqdft(cos_t, nsin_t, p, _fold(p, tiles["fold"]), tiles["dft_m"], tiles["dft_k"])
        x1, h2, logits = _mix(y, u, x, mod, wl, tiles["seq"])
        x = _moe(h2.reshape(bsz * s, D_MODEL), logits.reshape(bsz * s, LANES),
                 x1.reshape(bsz * s, D_MODEL), mod, final_g, wl, l, s, tiles,
                 final=(l == depth - 1)).reshape(bsz, s, D_MODEL)
    return x


def _seq_tables(s):
    k = jnp.arange(s // 2, dtype=jnp.int32)[None, :]

    def cos_sin(j):
        ang = ((j[:, None] * k) % s).astype(F32) * (2.0 * math.pi / s)
        return jnp.cos(ang), jnp.sin(ang)

    ca, sa = cos_sin(jnp.arange(s // LANES, dtype=jnp.int32) * LANES)
    cb, sb = cos_sin(jnp.arange(LANES, dtype=jnp.int32))
    scale = 1.0 / math.sqrt(s)
    cos_t = (ca[:, None, :] * cb[None, :, :] - sa[:, None, :] * sb[None, :, :]) * scale
    nsin_t = (sa[:, None, :] * cb[None, :, :] + ca[:, None, :] * sb[None, :, :]) * (-scale)
    return cos_t.reshape(s, s // 2).astype(BF16), nsin_t.reshape(s, s // 2).astype(BF16)


def kernel(x_prompt, x_sample, c_prompt, c_sample, norm1_g, norm2_g, ada_w, ada_b, w_in, fourier_w, conv31_w, conv31_b, gn_g, gn_b, pw_w, pool_w, pool_scale, conv3_w, sconv_out_w, w_o, router_w, router_b, exp_gate_w, exp_gate_b, exp_up_w, exp_up_b, exp_down_w, exp_down_b, final_g):
    w = dict(norm1_g=norm1_g, norm2_g=norm2_g, ada_w=ada_w, ada_b=ada_b, w_in=w_in, fourier_w=fourier_w,
             conv31_w=conv31_w, conv31_b=conv31_b, gn_g=gn_g, gn_b=gn_b, pw_w=pw_w, pool_w=pool_w,
             pool_scale=pool_scale, conv3_w=conv3_w, sconv_out_w=sconv_out_w, w_o=w_o, router_w=router_w,
             router_b=router_b, exp_gate_w=exp_gate_w, exp_gate_b=exp_gate_b, exp_up_w=exp_up_w,
             exp_up_b=exp_up_b, exp_down_w=exp_down_w, exp_down_b=exp_down_b, final_g=final_g)
    chan_dft, layers = _prep_weights(w)
    y_prompt = _encoder(x_prompt, c_prompt, w, chan_dft, layers, _seq_tables(x_prompt.shape[1]))
    y_sample = _encoder(x_sample, c_sample, w, chan_dft, layers, _seq_tables(x_sample.shape[1]))
    return (y_prompt, y_sample)
```

```python
import functools
import math

import jax
import jax.numpy as jnp
from jax import lax
from jax.experimental import pallas as pl
from jax.experimental.pallas import tpu as pltpu

D_MODEL = 1024
GROUP = 256
HEAD_DIM = 64
CONV31 = 31
POOL_WINDOWS = (2, 4, 8, 16)
N_EXPERTS = 32
TOP_K = 4
D_FF = 1024
SWIGLU_ALPHA = 1.702
SWIGLU_LIMIT = 7.0
MOE_BLOCK = 512
RMS_EPS = 1e-6
GN_EPS = 1e-5
HALO = 16
LANES = 128
VMEM_LIMIT = 48 * 1024 * 1024

F32 = jnp.float32
BF16 = jnp.bfloat16


def _cparams(sem):
    return pltpu.CompilerParams(dimension_semantics=sem, vmem_limit_bytes=VMEM_LIMIT)


def _dot(a, b):
    return jnp.dot(a, b, preferred_element_type=F32)


def _split_bf16(x):
    hi = x.astype(BF16)
    lo = (x - hi.astype(F32)).astype(BF16)
    return hi, lo


def _dot_hilo(x, w_bf16):
    hi, lo = _split_bf16(x)
    return _dot(hi, w_bf16) + _dot(lo, w_bf16)


def _sigmoid(x):
    return 1.0 / (1.0 + jnp.exp(-x))


def _ada_kernel(c_ref, w_ref, b_ref, o_ref):
    c = c_ref[...]
    cs = c * _sigmoid(c)
    c_hi, c_lo = _split_bf16(cs)
    w_hi, w_lo = _split_bf16(w_ref[0])
    o_ref[...] = _dot(c_hi, w_hi) + _dot(c_lo, w_hi) + _dot(c_hi, w_lo) + b_ref[0]


def _ada(c, ada_w, ada_b, layer):
    bsz = c.shape[0]
    n_out = ada_w.shape[2]
    return pl.pallas_call(
        _ada_kernel,
        out_shape=jax.ShapeDtypeStruct((bsz, n_out), F32),
        grid=(n_out // D_MODEL,),
        in_specs=[
            pl.BlockSpec((bsz, D_MODEL), lambda j: (0, 0)),
            pl.BlockSpec((1, D_MODEL, D_MODEL), lambda j: (layer, 0, j)),
            pl.BlockSpec((1, 1, D_MODEL), lambda j: (layer, 0, j)),
        ],
        out_specs=pl.BlockSpec((bsz, D_MODEL), lambda j: (0, j)),
        compiler_params=_cparams(("arbitrary",)),
        name="ada",
    )(c, ada_w, ada_b)


def _rms_mod(x, g, scale, shift):
    ms = jnp.mean(x * x, axis=-1, keepdims=True)
    y = x * lax.rsqrt(ms + RMS_EPS) * g
    return y * (1.0 + scale) + shift


def _inproj_kernel(x_ref, mod_ref, g_ref, w_ref, cs_ref, p_ref, u_ref):
    x = x_ref[0]
    h = _rms_mod(x, g_ref[...], mod_ref[0, 1:2, :], mod_ref[0, 0:1, :]).astype(BF16)
    a = _dot(h, w_ref[:, 0:GROUP]).astype(BF16)
    p_ref[0] = _dot(a, cs_ref[...]).astype(BF16)
    for j in range(6):
        u_ref[0, :, j * GROUP:(j + 1) * GROUP] = _dot(
            h, w_ref[:, (j + 1) * GROUP:(j + 2) * GROUP]).astype(BF16)


def _inproj(x, mod, g, w_in, chan_dft, tm):
    bsz, s, _ = x.shape
    return pl.pallas_call(
        _inproj_kernel,
        out_shape=(jax.ShapeDtypeStruct((bsz, s, 2 * GROUP), BF16),
                   jax.ShapeDtypeStruct((bsz, s, 6 * GROUP), BF16)),
        grid=(bsz, s // tm),
        in_specs=[
            pl.BlockSpec((1, tm, D_MODEL), lambda b, i: (b, i, 0)),
            pl.BlockSpec((1, 8, D_MODEL), lambda b, i: (b, 0, 0)),
            pl.BlockSpec((1, D_MODEL), lambda b, i: (0, 0)),
            pl.BlockSpec((D_MODEL, 7 * GROUP), lambda b, i: (0, 0)),
            pl.BlockSpec((GROUP, 2 * GROUP), lambda b, i: (0, 0)),
        ],
        out_specs=(pl.BlockSpec((1, tm, 2 * GROUP), lambda b, i: (b, i, 0)),
                   pl.BlockSpec((1, tm, 6 * GROUP), lambda b, i: (b, i, 0))),
        compiler_params=_cparams(("arbitrary", "arbitrary")),
        name="inproj",
    )(x, mod, g, w_in, chan_dft)


def _fold_kernel(cur_ref, mir_ref, nxt_ref, e_ref, *, tf):
    kt = pl.program_id(1)
    r_io = lax.broadcasted_iota(jnp.int32, (tf, tf), 0)
    c_io = lax.broadcasted_iota(jnp.int32, (tf, tf), 1)
    flip = (c_io == tf - r_io).astype(BF16)
    rev = _dot(flip, mir_ref[0])
    cur = cur_ref[0].astype(F32)
    lane = lax.broadcasted_iota(jnp.int32, (1, 2 * GROUP), 1)
    first_tile_row0 = jnp.where(lane < GROUP, 0.0, cur[0:1, :])
    row0 = jnp.where(kt == 0, first_tile_row0, nxt_ref[0, 0:1, :].astype(F32))
    row = lax.broadcasted_iota(jnp.int32, (tf, 2 * GROUP), 0)
    rev = jnp.where(row == 0, row0, rev)
    sign = jnp.where(lax.broadcasted_iota(jnp.int32, (tf, 2 * GROUP), 1) < GROUP, 1.0, -1.0)
    e_ref[0] = (cur + sign * rev).astype(BF16)


def _fold(p, tf):
    bsz, s, _ = p.shape
    n_t = s // tf
    kern = functools.partial(_fold_kernel, tf=tf)
    return pl.pallas_call(
        kern,
        out_shape=jax.ShapeDtypeStruct((bsz, s // 2, 2 * GROUP), BF16),
        grid=(bsz, n_t // 2),
        in_specs=[
            pl.BlockSpec((1, tf, 2 * GROUP), lambda b, k: (b, k, 0)),
            pl.BlockSpec((1, tf, 2 * GROUP), lambda b, k: (b, n_t - 1 - k, 0)),
            pl.BlockSpec((1, 8, 2 * GROUP), lambda b, k: (b, ((n_t - k) % n_t) * (tf // 8), 0)),
        ],
        out_specs=pl.BlockSpec((1, tf, 2 * GROUP), lambda b, k: (b, k, 0)),
        compiler_params=_cparams(("arbitrary", "arbitrary")),
        name="fold",
    )(p, p, p)


def _seqdft_kernel(c_ref, s_ref, e_ref, mid_ref, o_ref, acc_ref, *, tm, seq_len):
    i = pl.program_id(0)
    k = pl.program_id(1)
    b = pl.program_id(2)
    contrib = _dot(c_ref[...], e_ref[0, :, 0:GROUP]) + _dot(s_ref[...], e_ref[0, :, GROUP:2 * GROUP])

    @pl.when(k == 0)
    def _():
        acc_ref[b] = contrib

    @pl.when(k > 0)
    def _():
        acc_ref[b] = acc_ref[b] + contrib

    @pl.when(k == pl.num_programs(1) - 1)
    def _():
        j = lax.broadcasted_iota(jnp.int32, (tm, GROUP), 0) + i * tm
        sign = jnp.where((j & 1) == 0, 1.0, -1.0) * (1.0 / math.sqrt(seq_len))
        o_ref[b] = (acc_ref[b] + sign * mid_ref[0, 0:1, 0:GROUP].astype(F32)).astype(BF16)


def _seqdft(cos_t, nsin_t, p, e, tm, tk):
    bsz, s, _ = p.shape
    half = s // 2
    kern = functools.partial(_seqdft_kernel, tm=tm, seq_len=s)
    return pl.pallas_call(
        kern,
        out_shape=jax.ShapeDtypeStruct((bsz, s, GROUP), BF16),
        grid=(s // tm, half // tk, bsz),
        in_specs=[
            pl.BlockSpec((tm, tk), lambda i, k, b: (i, k)),
            pl.BlockSpec((tm, tk), lambda i, k, b: (i, k)),
            pl.BlockSpec((1, tk, 2 * GROUP), lambda i, k, b: (b, k, 0)),
            pl.BlockSpec((1, 8, 2 * GROUP), lambda i, k, b: (b, half // 8, 0)),
        ],
        out_specs=pl.BlockSpec((bsz, tm, GROUP), lambda i, k, b: (0, i, 0)),
        scratch_shapes=[pltpu.VMEM((bsz, tm, GROUP), F32)],
        compiler_params=_cparams(("arbitrary", "arbitrary", "arbitrary")),
        name="seqdft",
    )(cos_t, nsin_t, e, p)


U_V, U_G, U_P, U_BG, U_CG, U_XV = (j * GROUP for j in range(6))


def _mix_kernel(y_ref, u_ref, up_ref, un_ref, x_ref, mod_ref,
                fw_ref, c31w_ref, c31b_ref, gng_ref, gnb_ref, avg_ref, pww_ref,
                poolw_ref, pscale_ref, c3w_ref, sow_ref, wo_ref, n2g_ref, rw_ref, rb_ref,
                x1_ref, h2_ref, lg_ref,
                zext, zsh, pext, s2, s4, s8, s16, qext, cat, *, tq, seq_len):
    i = pl.program_id(1)
    has_prev = (i > 0).astype(F32)
    has_next = (i < pl.num_programs(1) - 1).astype(F32)
    rows = tq + 2 * HALO

    def cols(ref, c0):
        return ref[0, :, c0:c0 + GROUP].astype(F32)

    def fill_ext(dst, fn):
        dst[0:HALO, :] = fn(up_ref) * has_prev
        dst[HALO:HALO + tq, :] = fn(u_ref)
        dst[HALO + tq:rows, :] = fn(un_ref) * has_next

    cat[:, 0:GROUP] = _dot(y_ref[0], fw_ref[...]).astype(BF16)

    fill_ext(zext, lambda r: cols(r, U_V) * _sigmoid(cols(r, U_G)))
    conv = jnp.broadcast_to(c31b_ref[...], (tq, GROUP))
    first = HALO - CONV31 // 2
    for res in range(8):
        taps = [j for j in range(CONV31) if (first + j) % 8 == res]
        span = (first + taps[-1]) // 8 * 8 + tq
        zsh[0:span, :] = zext[pl.ds(res, span), :]
        for j in taps:
            q = (first + j) // 8 * 8
            conv = conv + c31w_ref[j:j + 1, :] * zsh[q:q + tq, :]
    mu = _dot_hilo(conv, avg_ref[...])
    dev = conv - mu
    var = _dot_hilo(dev * dev, avg_ref[...])
    zn = dev * lax.rsqrt(var + GN_EPS) * gng_ref[...] + gnb_ref[...]
    cat[:, GROUP:2 * GROUP] = _dot((zn * _sigmoid(zn)).astype(BF16), pww_ref[...]).astype(BF16)

    fill_ext(pext, lambda r: cols(r, U_P))
    n = rows - 8
    zeros8 = jnp.zeros((8, GROUP), F32)
    s2[0:n, :] = pext[0:n, :] + pext[pl.ds(1, n), :]
    s2[n:rows, :] = zeros8
    s4[0:n, :] = s2[0:n, :] + s2[pl.ds(2, n), :]
    s4[n:rows, :] = zeros8
    s8[0:n, :] = s4[0:n, :] + s4[pl.ds(4, n), :]
    s8[n:rows, :] = zeros8
    s16[0:n, :] = s8[0:n, :] + s8[pl.ds(8, n), :]
    lane = lax.broadcasted_iota(jnp.int32, (tq, GROUP), 1)
    pos = lax.broadcasted_iota(jnp.int32, (tq, GROUP), 0) + i * tq
    win = jnp.where(lane < HEAD_DIM, s2[pl.ds(HALO - 1, tq), :],
                    jnp.where(lane < 2 * HEAD_DIM, s4[pl.ds(HALO - 2, tq), :],
                              jnp.where(lane < 3 * HEAD_DIM, s8[pl.ds(HALO - 4, tq), :],
                                        s16[pl.ds(HALO - 8, tq), :])))
    half = jnp.where(lane < HEAD_DIM, 1,
                     jnp.where(lane < 2 * HEAD_DIM, 2, jnp.where(lane < 3 * HEAD_DIM, 4, 8)))
    cnt = jnp.minimum(pos + half, seq_len) - jnp.maximum(pos - half, 0)
    dpool = win / cnt.astype(F32) - pext[HALO:HALO + tq, :]
    cat[:, 2 * GROUP:3 * GROUP] = (_dot(dpool.astype(BF16), poolw_ref[...]) * pscale_ref[...]).astype(BF16)

    fill_ext(qext, lambda r: cols(r, U_CG) * cols(r, U_XV))
    c3 = (c3w_ref[0:1, :] * qext[pl.ds(HALO - 1, tq), :]
          + c3w_ref[1:2, :] * qext[HALO:HALO + tq, :]
          + c3w_ref[2:3, :] * qext[pl.ds(HALO + 1, tq), :])
    cat[:, 3 * GROUP:4 * GROUP] = _dot((cols(u_ref, U_BG) * c3).astype(BF16), sow_ref[...]).astype(BF16)

    mixed = _dot(cat[...], wo_ref[...])
    x1 = x_ref[0] + mod_ref[0, 2:3, :] * mixed
    x1_ref[0] = x1
    h2 = _rms_mod(x1, n2g_ref[...], mod_ref[0, 4:5, :], mod_ref[0, 3:4, :])
    h2_ref[0] = h2
    lg_ref[0] = _dot(h2.astype(BF16), rw_ref[...]) + rb_ref[...]


def _mix(y, u, x, mod, wl, tq):
    bsz, s, _ = x.shape
    rows = tq + 2 * HALO
    hb = tq // HALO
    n_hblk = s // HALO
    full = lambda shape: pl.BlockSpec(shape, lambda b, i: tuple(0 for _ in shape))
    kern = functools.partial(_mix_kernel, tq=tq, seq_len=s)
    return pl.pallas_call(
        kern,
        out_shape=(jax.ShapeDtypeStruct((bsz, s, D_MODEL), F32),
                   jax.ShapeDtypeStruct((bsz, s, D_MODEL), F32),
                   jax.ShapeDtypeStruct((bsz, s, LANES), F32)),
        grid=(bsz, s // tq),
        in_specs=[
            pl.BlockSpec((1, tq, GROUP), lambda b, i: (b, i, 0)),
            pl.BlockSpec((1, tq, 6 * GROUP), lambda b, i: (b, i, 0)),
            pl.BlockSpec((1, HALO, 6 * GROUP), lambda b, i: (b, jnp.maximum(i * hb - 1, 0), 0)),
            pl.BlockSpec((1, HALO, 6 * GROUP), lambda b, i: (b, jnp.minimum((i + 1) * hb, n_hblk - 1), 0)),
            pl.BlockSpec((1, tq, D_MODEL), lambda b, i: (b, i, 0)),
            pl.BlockSpec((1, 8, D_MODEL), lambda b, i: (b, 0, 0)),
            full((GROUP, GROUP)),
            full((32, GROUP)),
            full((1, GROUP)),
            full((1, GROUP)),
            full((1, GROUP)),
            full((GROUP, GROUP)),
            full((GROUP, GROUP)),
            full((GROUP, GROUP)),
            full((1, GROUP)),
            full((8, GROUP)),
            full((GROUP, GROUP)),
            full((D_MODEL, D_MODEL)),
            full((1, D_MODEL)),
            full((D_MODEL, LANES)),
            full((1, LANES)),
        ],
        out_specs=(pl.BlockSpec((1, tq, D_MODEL), lambda b, i: (b, i, 0)),
                   pl.BlockSpec((1, tq, D_MODEL), lambda b, i: (b, i, 0)),
                   pl.BlockSpec((1, tq, LANES), lambda b, i: (b, i, 0))),
        scratch_shapes=[pltpu.VMEM((rows, GROUP), F32) for _ in range(8)]
        + [pltpu.VMEM((tq, D_MODEL), BF16)],
        compiler_params=_cparams(("arbitrary", "arbitrary")),
        name="mix",
    )(y, u, u, u, x, mod, wl["fourier_w"], wl["conv31_w"], wl["conv31_b"], wl["gn_g"], wl["gn_b"],
      wl["avg"], wl["pw_w"], wl["pool_w"], wl["pool_scale"], wl["conv3_w"], wl["sconv_out_w"],
      wl["w_o"], wl["norm2_g"], wl["router_w"], wl["router_b"])


def _route_kernel(lg_ref, e_ref, rank_ref, gate_ref, base_ref, tcnt_ref, cnt_ref, carry, *, tr, sub):
    i = pl.program_id(0)

    @pl.when(i == 0)
    def _():
        carry[...] = jnp.zeros_like(carry)

    lane = lax.broadcasted_iota(jnp.int32, (tr, LANES), 1).astype(F32)
    neg = jnp.float32(-jnp.inf)
    work = jnp.where(lane < N_EXPERTS, lg_ref[...], neg)
    vals, idxs, hots = [], [], []
    for _ in range(TOP_K):
        m = jnp.max(work, axis=1, keepdims=True)
        idx = jnp.min(jnp.where(work == m, lane, float(LANES)), axis=1, keepdims=True)
        hot = lane == idx
        vals.append(m)
        idxs.append(idx.astype(jnp.int32))
        hots.append(hot)
        work = jnp.where(hot, neg, work)
    exps = [jnp.exp(v - vals[0]) for v in vals]
    denom = exps[0] + exps[1] + exps[2] + exps[3]
    member = sum(h.astype(F32) for h in hots)
    r_io = lax.broadcasted_iota(jnp.int32, (tr, tr), 0)
    c_io = lax.broadcasted_iota(jnp.int32, (tr, tr), 1)
    lower = (c_io < r_io).astype(BF16)
    before = _dot(lower, member.astype(BF16)) + carry[...]
    col = lax.broadcasted_iota(jnp.int32, (tr, TOP_K), 1)
    e_out = jnp.zeros((tr, TOP_K), jnp.int32)
    r_out = jnp.zeros((tr, TOP_K), jnp.int32)
    g_out = jnp.zeros((tr, TOP_K), F32)
    for k in range(TOP_K):
        rk = jnp.sum(jnp.where(hots[k], before, 0.0), axis=1, keepdims=True).astype(jnp.int32)
        e_out = jnp.where(col == k, idxs[k], e_out)
        r_out = jnp.where(col == k, rk, r_out)
        g_out = jnp.where(col == k, exps[k] / denom, g_out)
    e_ref[...] = e_out
    rank_ref[...] = r_out
    gate_ref[...] = g_out
    running = carry[...]
    for q in range(tr // sub):
        cnt_q = jnp.sum(member[q * sub:(q + 1) * sub, :], axis=0, keepdims=True)
        base_ref[q] = running.astype(jnp.int32)
        tcnt_ref[q] = cnt_q.astype(jnp.int32)
        running = running + cnt_q
    carry[...] = running
    cnt_ref[...] = running


def _route(logits, sub):
    t_tok = logits.shape[0]
    tr = min(2 * sub, t_tok)
    n_steps = t_tok // tr
    n_tiles = t_tok // sub
    kern = functools.partial(_route_kernel, tr=tr, sub=sub)
    tile = lambda w: pl.BlockSpec((tr, w), lambda i: (i, 0))
    per_tile = pl.BlockSpec((tr // sub, 1, LANES), lambda i: (i, 0, 0))
    return pl.pallas_call(
        kern,
        out_shape=(jax.ShapeDtypeStruct((t_tok, TOP_K), jnp.int32),
                   jax.ShapeDtypeStruct((t_tok, TOP_K), jnp.int32),
                   jax.ShapeDtypeStruct((t_tok, TOP_K), F32),
                   jax.ShapeDtypeStruct((n_tiles, 1, LANES), jnp.int32),
                   jax.ShapeDtypeStruct((n_tiles, 1, LANES), jnp.int32),
                   jax.ShapeDtypeStruct((1, LANES), F32)),
        grid=(n_steps,),
        in_specs=[tile(LANES)],
        out_specs=(tile(TOP_K), tile(TOP_K), tile(TOP_K), per_tile, per_tile,
                   pl.BlockSpec((1, LANES), lambda i: (0, 0))),
        scratch_shapes=[pltpu.VMEM((1, LANES), F32)],
        compiler_params=_cparams(("arbitrary",)),
        name="route",
    )(logits)


F32_TILE = 8
DISP_ROWS = TOP_K * 256 + N_EXPERTS * 2 * F32_TILE
CARRY_ROWS = N_EXPERTS * F32_TILE
DISP_WIN = 64
XS_COLS = D_MODEL // 2


def _pack_pairs(x):
    lo = lax.bitcast_convert_type(x[:, :XS_COLS], jnp.uint32)
    hi = lax.bitcast_convert_type(x[:, XS_COLS:], jnp.uint32)
    return (hi & jnp.uint32(0xFFFF0000)) | (lo >> 16)


def _unpack_pairs(w):
    lo = lax.bitcast_convert_type(w << 16, F32).astype(BF16)
    hi = lax.bitcast_convert_type(w & jnp.uint32(0xFFFF0000), F32).astype(BF16)
    return lo, hi


def _dispatch_kernel(src_ref, dst_ref, nch_ref, head_ref, nwin_ref, nextra_ref, used_ref, pend_ref, npad_ref,
                     e_ref, rank_ref, off_ref, h_ref, xs_hbm, outbuf, carry, zeros, sems, sem_z, *, td):
    i = pl.program_id(0)
    n_steps = pl.num_programs(0)
    slot = i % 2

    def zero_blk(row):
        return pltpu.make_async_copy(zeros, xs_hbm.at[pl.ds(pl.multiple_of(row, MOE_BLOCK), MOE_BLOCK)], sem_z)

    @pl.when(i == 0)
    def _():
        zeros[...] = jnp.zeros_like(zeros)
        carry[...] = jnp.zeros_like(carry)
        outbuf[...] = jnp.zeros_like(outbuf)
        first_unused = pend_ref[N_EXPERTS - 1] // MOE_BLOCK
        n_blocks = xs_hbm.shape[0] // MOE_BLOCK

        def start(e, c):
            @pl.when(npad_ref[e] > 0)
            def _():
                zero_blk(pend_ref[e] - MOE_BLOCK).start()

            @pl.when(npad_ref[e] > MOE_BLOCK)
            def _():
                zero_blk(pend_ref[e] - 2 * MOE_BLOCK).start()
            return c

        def wait(e, c):
            @pl.when(npad_ref[e] > 0)
            def _():
                zero_blk(pend_ref[e] - MOE_BLOCK).wait()

            @pl.when(npad_ref[e] > MOE_BLOCK)
            def _():
                zero_blk(pend_ref[e] - 2 * MOE_BLOCK).wait()
            return c

        def start_tail(b, c):
            zero_blk(b * MOE_BLOCK).start()
            return c

        def wait_tail(b, c):
            zero_blk(b * MOE_BLOCK).wait()
            return c

        lax.fori_loop(0, N_EXPERTS, start, 0)
        lax.fori_loop(first_unused, n_blocks, start_tail, 0)
        lax.fori_loop(0, N_EXPERTS, wait, 0)
        lax.fori_loop(first_unused, n_blocks, wait_tail, 0)

    def window_copies(tile, sl, act):
        for e in range(N_EXPERTS):
            n_ch = nch_ref[tile * N_EXPERTS + e]
            src0 = pl.multiple_of(src_ref[tile * N_EXPERTS + e], F32_TILE)
            dst0 = pl.multiple_of(dst_ref[tile * N_EXPERTS + e], F32_TILE)

            @pl.when(n_ch > 0)
            def _():
                act(pltpu.make_async_copy(outbuf.at[sl, pl.ds(src0, DISP_WIN)],
                                          xs_hbm.at[pl.ds(dst0, DISP_WIN)], sems.at[sl]))

            def per_chunk(ci, c2):
                src = pl.multiple_of(src0 + ci * F32_TILE, F32_TILE)
                dst = pl.multiple_of(dst0 + ci * F32_TILE, F32_TILE)
                act(pltpu.make_async_copy(outbuf.at[sl, pl.ds(src, F32_TILE)],
                                          xs_hbm.at[pl.ds(dst, F32_TILE)], sems.at[sl]))
                return c2

            lax.fori_loop(DISP_WIN // F32_TILE, n_ch, per_chunk, 0)

    lane_td = lax.broadcasted_iota(jnp.int32, (td, LANES), 1)
    off_f = off_ref[0].astype(F32)
    pos_cols = jnp.full((td, LANES), -1.0, F32)
    for k in range(TOP_K):
        off_k = jnp.sum(jnp.where(lane_td == e_ref[:, k:k + 1], off_f, 0.0), axis=1, keepdims=True)
        pos_cols = jnp.where(lane_td == k, rank_ref[:, k:k + 1].astype(F32) + off_k, pos_cols)
    pos_t = jnp.transpose(pos_cols).astype(jnp.int32)
    pos = [jnp.broadcast_to(pos_t[k:k + 1, :], (GROUP, td)) for k in range(TOP_K)]
    row_io = lax.broadcasted_iota(jnp.int32, (GROUP, td), 0)
    h_bf = h_ref[...].astype(BF16)
    def select_rows(rc):
        r = row_io + rc * GROUP
        tok = jnp.zeros((GROUP, td), F32)
        for k in range(TOP_K):
            tok = jnp.where(r == pos[k], 1.0, tok)
        outbuf[slot, rc * GROUP:(rc + 1) * GROUP, :] = _pack_pairs(_dot(tok.astype(BF16), h_bf))

    for rc in range(DISP_ROWS // GROUP):
        if rc * GROUP < TOP_K * td:
            select_rows(rc)
        else:
            pl.when(used_ref[i] > rc * GROUP)(functools.partial(select_rows, rc))

    sub_io = lax.broadcasted_iota(jnp.int32, (F32_TILE, XS_COLS), 0)
    for e in range(N_EXPERTS):
        es = slice(e * F32_TILE, (e + 1) * F32_TILE)
        n_ch = nch_ref[i * N_EXPERTS + e]
        first = pl.multiple_of(src_ref[i * N_EXPERTS + e], F32_TILE)
        outbuf[slot, pl.ds(first, F32_TILE), :] = jnp.where(
            sub_io < head_ref[i * N_EXPERTS + e], carry[es, :], outbuf[slot, pl.ds(first, F32_TILE), :])
        last = pl.multiple_of(first + jnp.maximum(n_ch - 1, 0) * F32_TILE, F32_TILE)
        carry[es, :] = jnp.where(n_ch > 0, outbuf[slot, pl.ds(last, F32_TILE), :], carry[es, :])

    def wait_copies(tile, sl):
        def wait_window(c, carry_):
            pltpu.make_async_copy(outbuf.at[sl, pl.ds(0, DISP_WIN)], xs_hbm.at[pl.ds(0, DISP_WIN)],
                                  sems.at[sl]).wait()
            return carry_

        def wait_chunk(c, carry_):
            pltpu.make_async_copy(outbuf.at[sl, pl.ds(0, F32_TILE)], xs_hbm.at[pl.ds(0, F32_TILE)],
                                  sems.at[sl]).wait()
            return carry_

        lax.fori_loop(0, nwin_ref[tile], wait_window, 0)
        lax.fori_loop(0, nextra_ref[tile], wait_chunk, 0)

    @pl.when(i > 0)
    def _():
        wait_copies(i - 1, 1 - slot)

    window_copies(i, slot, lambda cp: cp.start())

    @pl.when(i == n_steps - 1)
    def _():
        wait_copies(i, slot)


def _dispatch(chunk_src, chunk_dst, n_chunks, head, n_windows, n_extra, used_rows, pend, npad, e_idx, rank,
              col_off, h2, n_slots, td):
    t_tok = h2.shape[0]
    assert DISP_ROWS >= TOP_K * td + N_EXPERTS * 2 * (F32_TILE - 1) + F32_TILE and DISP_ROWS % GROUP == 0
    kern = functools.partial(_dispatch_kernel, td=td)
    tok4 = lambda: pl.BlockSpec((td, TOP_K), lambda i, *_: (i, 0))
    return pl.pallas_call(
        kern,
        out_shape=jax.ShapeDtypeStruct((n_slots, XS_COLS), jnp.uint32),
        grid_spec=pltpu.PrefetchScalarGridSpec(
            num_scalar_prefetch=9,
            grid=(t_tok // td,),
            in_specs=[tok4(), tok4(),
                      pl.BlockSpec((1, 1, LANES), lambda i, *_: (i, 0, 0)),
                      pl.BlockSpec((td, D_MODEL), lambda i, *_: (i, 0))],
            out_specs=pl.BlockSpec(memory_space=pl.ANY),
            scratch_shapes=[pltpu.VMEM((2, DISP_ROWS + DISP_WIN, XS_COLS), jnp.uint32),
                            pltpu.VMEM((CARRY_ROWS, XS_COLS), jnp.uint32),
                            pltpu.VMEM((MOE_BLOCK, XS_COLS), jnp.uint32),
                            pltpu.SemaphoreType.DMA((2,)),
                            pltpu.SemaphoreType.DMA]),
        compiler_params=_cparams(("arbitrary",)),
        name="dispatch",
    )(chunk_src, chunk_dst, n_chunks, head, n_windows, n_extra, used_rows, pend, npad, e_idx, rank, col_off, h2)


def _experts_kernel(be_ref, nu_ref, xs_ref, wg_ref, bg_ref, wu_ref, bu_ref, wd_ref, bd_ref, o_ref,
                    wg_bf, wu_bf, wd_bf, act):
    i = pl.program_id(0)
    last = nu_ref[0] - 1

    @pl.when((i == 0) | ((i <= last) & (be_ref[jnp.minimum(i, last)] != be_ref[jnp.maximum(jnp.minimum(i, last) - 1, 0)])))
    def _():
        wg_bf[...] = wg_ref[0].astype(BF16)
        wu_bf[...] = wu_ref[0].astype(BF16)
        wd_bf[...] = wd_ref[0].astype(BF16)

    @pl.when(i <= last)
    def _():
        x_lo, x_hi = _unpack_pairs(xs_ref[...])

        def proj(w_bf, fs):
            return _dot(x_lo, w_bf[:XS_COLS, fs]) + _dot(x_hi, w_bf[XS_COLS:, fs])

        for f in range(D_FF // GROUP):
            fs = slice(f * GROUP, (f + 1) * GROUP)
            gt = jnp.minimum(proj(wg_bf, fs) + bg_ref[0, :, fs], SWIGLU_LIMIT)
            up = jnp.clip(proj(wu_bf, fs) + bu_ref[0, :, fs], -SWIGLU_LIMIT, SWIGLU_LIMIT)
            act[:, fs] = ((up + 1.0) * (gt * _sigmoid(SWIGLU_ALPHA * gt))).astype(BF16)
        o_ref[...] = (_dot(act[...], wd_bf[...]) + bd_ref[0]).astype(BF16)

    @pl.when(i > last)
    def _():
        o_ref[...] = jnp.zeros_like(o_ref)


def _experts(block_e, n_used, xs, wl, layer):
    n_slots = xs.shape[0]
    n_blocks = n_slots // MOE_BLOCK

    def expert(i, be, nu):
        return layer * N_EXPERTS + be[jnp.minimum(i, nu[0] - 1)]

    wspec = lambda: pl.BlockSpec((1, D_MODEL, D_FF), lambda i, be, nu: (expert(i, be, nu), 0, 0))
    bspec = lambda: pl.BlockSpec((1, 1, D_FF), lambda i, be, nu: (expert(i, be, nu), 0, 0))
    return pl.pallas_call(
        _experts_kernel,
        out_shape=jax.ShapeDtypeStruct((n_slots, D_MODEL), BF16),
        grid_spec=pltpu.PrefetchScalarGridSpec(
            num_scalar_prefetch=2,
            grid=(n_blocks,),
            in_specs=[pl.BlockSpec((MOE_BLOCK, XS_COLS), lambda i, be, nu: (jnp.minimum(i, nu[0] - 1), 0)),
                      wspec(), bspec(), wspec(), bspec(), wspec(), bspec()],
            out_specs=pl.BlockSpec((MOE_BLOCK, D_MODEL), lambda i, be, nu: (i, 0)),
            scratch_shapes=[pltpu.VMEM((D_MODEL, D_FF), BF16), pltpu.VMEM((D_MODEL, D_FF), BF16),
                            pltpu.VMEM((D_FF, D_MODEL), BF16), pltpu.VMEM((MOE_BLOCK, D_FF), BF16)]),
        compiler_params=_cparams(("arbitrary",)),
        name="experts",
    )(block_e, n_used, xs, wl["exp_gate_w"], wl["exp_gate_b"], wl["exp_up_w"], wl["exp_up_b"],
      wl["exp_down_w"], wl["exp_down_b"])


ROW_TILE = 16
SEG_ROWS = TOP_K * 256 + N_EXPERTS * 2 * ROW_TILE
K_CHUNK = 512
BIG_CHUNK = 2 * ROW_TILE


def _combine_kernel(src_ref, dst_ref, nch_ref, tot_ref, big_ref, small_ref, e_ref, rank_ref, gate_ref, off_ref, x1_ref, mod_ref,
                    fg_ref, ys_hbm, o_ref, segbuf, gmat, acc, sems, *, tc, final):
    j = pl.program_id(0)
    n_tiles = pl.num_programs(0)
    slot = j % 2

    def chunks(tile, sl, act):
        def per_expert(e, c):
            src0 = src_ref[tile * N_EXPERTS + e]
            dst0 = dst_ref[tile * N_EXPERTS + e]
            n_ch = nch_ref[tile * N_EXPERTS + e]

            def per_pair(ci, c2):
                src = pl.multiple_of(src0 + ci * BIG_CHUNK, ROW_TILE)
                dst = pl.multiple_of(dst0 + ci * BIG_CHUNK, ROW_TILE)
                act(pltpu.make_async_copy(ys_hbm.at[pl.ds(src, BIG_CHUNK)],
                                          segbuf.at[sl, pl.ds(dst, BIG_CHUNK)], sems.at[sl]))
                return c2

            lax.fori_loop(0, n_ch // 2, per_pair, 0)

            @pl.when(n_ch % 2 == 1)
            def _():
                src = pl.multiple_of(src0 + (n_ch - 1) * ROW_TILE, ROW_TILE)
                dst = pl.multiple_of(dst0 + (n_ch - 1) * ROW_TILE, ROW_TILE)
                act(pltpu.make_async_copy(ys_hbm.at[pl.ds(src, ROW_TILE)],
                                          segbuf.at[sl, pl.ds(dst, ROW_TILE)], sems.at[sl]))
            return c

        lax.fori_loop(0, N_EXPERTS, per_expert, 0)

    @pl.when(j == 0)
    def _():
        segbuf[...] = jnp.zeros_like(segbuf)
        chunks(0, 0, lambda cp: cp.start())

    @pl.when(j + 1 < n_tiles)
    def _():
        chunks(j + 1, 1 - slot, lambda cp: cp.start())

    lane_tc = lax.broadcasted_iota(jnp.int32, (tc, LANES), 1)
    off_f = off_ref[0].astype(F32)
    cols = []
    for k in range(TOP_K):
        off_k = jnp.sum(jnp.where(lane_tc == e_ref[:, k:k + 1], off_f, 0.0), axis=1, keepdims=True)
        cols.append(rank_ref[:, k:k + 1] + off_k.astype(jnp.int32))
    colb = [jnp.broadcast_to(c, (tc, LANES)) for c in cols]
    gateb = [jnp.broadcast_to(gate_ref[:, k:k + 1], (tc, LANES)) for k in range(TOP_K)]

    def build(p0, p1):
        for p in range(p0, p1):
            col_io = lane_tc + p * LANES
            g = jnp.zeros((tc, LANES), F32)
            for k in reversed(range(TOP_K)):
                g = jnp.where(col_io == colb[k], gateb[k], g)
            gmat[:, p * LANES:(p + 1) * LANES] = g.astype(BF16)

    def wait_big(c, carry):
        pltpu.make_async_copy(ys_hbm.at[pl.ds(0, BIG_CHUNK)], segbuf.at[slot, pl.ds(0, BIG_CHUNK)],
                              sems.at[slot]).wait()
        return carry

    def wait_small(c, carry):
        pltpu.make_async_copy(ys_hbm.at[pl.ds(0, ROW_TILE)], segbuf.at[slot, pl.ds(0, ROW_TILE)],
                              sems.at[slot]).wait()
        return carry

    always = TOP_K * tc
    build(0, always // LANES)
    lax.fori_loop(0, big_ref[j], wait_big, 0)
    lax.fori_loop(0, small_ref[j], wait_small, 0)
    acc[...] = _dot(gmat[:, 0:always], segbuf[slot, 0:always, :])
    used = tot_ref[j] * ROW_TILE
    for c0 in range(always, SEG_ROWS, K_CHUNK):
        @pl.when(used > c0)
        def _():
            build(c0 // LANES, (c0 + K_CHUNK) // LANES)
            acc[...] = acc[...] + _dot(gmat[:, c0:c0 + K_CHUNK], segbuf[slot, c0:c0 + K_CHUNK, :])
    moe = acc[...]

    x2 = x1_ref[...] + mod_ref[0, 5:6, :] * moe
    if final:
        ms = jnp.mean(x2 * x2, axis=-1, keepdims=True)
        x2 = x2 * lax.rsqrt(ms + RMS_EPS) * fg_ref[...]
    o_ref[...] = x2


def _combine(chunk_src, chunk_dst, n_chunks, tile_chunks, tile_big, tile_small, e_idx, rank, gates, col_off, x1, mod, final_g, ys,
             seq_len, tc, final):
    t_tok = x1.shape[0]
    assert SEG_ROWS >= TOP_K * tc + N_EXPERTS * 2 * (ROW_TILE - 1) and (SEG_ROWS - TOP_K * tc) % K_CHUNK == 0
    kern = functools.partial(_combine_kernel, tc=tc, final=final)
    tok4 = lambda: pl.BlockSpec((tc, TOP_K), lambda i, *_: (i, 0))
    return pl.pallas_call(
        kern,
        out_shape=jax.ShapeDtypeStruct((t_tok, D_MODEL), F32),
        grid_spec=pltpu.PrefetchScalarGridSpec(
            num_scalar_prefetch=6,
            grid=(t_tok // tc,),
            in_specs=[tok4(), tok4(), tok4(),
                      pl.BlockSpec((1, 1, LANES), lambda i, *_: (i, 0, 0)),
                      pl.BlockSpec((tc, D_MODEL), lambda i, *_: (i, 0)),
                      pl.BlockSpec((1, 8, D_MODEL), lambda i, *_: ((i * tc) // seq_len, 0, 0)),
                      pl.BlockSpec((1, D_MODEL), lambda i, *_: (0, 0)),
                      pl.BlockSpec(memory_space=pl.ANY)],
            out_specs=pl.BlockSpec((tc, D_MODEL), lambda i, *_: (i, 0)),
            scratch_shapes=[pltpu.VMEM((2, SEG_ROWS, D_MODEL), BF16),
                            pltpu.VMEM((tc, SEG_ROWS), BF16),
                            pltpu.VMEM((tc, D_MODEL), F32),
                            pltpu.SemaphoreType.DMA((2,))]),
        compiler_params=_cparams(("arbitrary",)),
        name="combine",
    )(chunk_src, chunk_dst, n_chunks, tile_chunks, tile_big, tile_small, e_idx, rank, gates, col_off, x1, mod,
      final_g, ys)


def _dft_tables(n):
    j = jnp.arange(n, dtype=jnp.int32)
    ang = ((j[:, None] * j[None, :]) % n).astype(F32) * (2.0 * math.pi / n)
    scale = 1.0 / math.sqrt(n)
    return jnp.cos(ang) * scale, jnp.sin(ang) * scale


def _block_diag(blocks):
    n, r, c = blocks.shape
    rows = lax.broadcasted_iota(jnp.int32, (n * r, n * c), 0) // r
    cols = lax.broadcasted_iota(jnp.int32, (n * r, n * c), 1) // c
    return jnp.where(rows == cols, jnp.tile(blocks.reshape(n * r, c), (1, n)), 0)


def _prep_weights(w):
    depth = w["w_in"].shape[0]
    c64, s64 = _dft_tables(HEAD_DIM)
    n_heads = GROUP // HEAD_DIM
    chan_dft = jnp.concatenate([_block_diag(jnp.stack([c64] * n_heads)),
                                _block_diag(jnp.stack([s64] * n_heads))], axis=1).astype(BF16)
    avg = _block_diag(jnp.full((n_heads, HEAD_DIM, HEAD_DIM), 1.0 / HEAD_DIM, F32)).astype(BF16)
    stacked = dict(
        exp_gate_w=w["exp_gate_w"].reshape(depth * N_EXPERTS, D_MODEL, D_FF),
        exp_up_w=w["exp_up_w"].reshape(depth * N_EXPERTS, D_MODEL, D_FF),
        exp_down_w=w["exp_down_w"].reshape(depth * N_EXPERTS, D_FF, D_MODEL),
        exp_gate_b=w["exp_gate_b"].reshape(depth * N_EXPERTS, 1, D_FF),
        exp_up_b=w["exp_up_b"].reshape(depth * N_EXPERTS, 1, D_FF),
        exp_down_b=w["exp_down_b"].reshape(depth * N_EXPERTS, 1, D_MODEL))
    layers = []
    for l in range(depth):
        layers.append(dict(
            stacked,
            norm1_g=w["norm1_g"][l].reshape(1, D_MODEL),
            norm2_g=w["norm2_g"][l].reshape(1, D_MODEL),
            w_in=w["w_in"][l].astype(BF16),
            fourier_w=w["fourier_w"][l].astype(BF16),
            conv31_w=jnp.pad(w["conv31_w"][l], ((0, 1), (0, 0))),
            conv31_b=w["conv31_b"][l].reshape(1, GROUP),
            gn_g=w["gn_g"][l].reshape(1, GROUP),
            gn_b=w["gn_b"][l].reshape(1, GROUP),
            avg=avg,
            pw_w=w["pw_w"][l].astype(BF16),
            pool_w=_block_diag(w["pool_w"][l]).astype(BF16),
            pool_scale=w["pool_scale"][l].reshape(1, GROUP),
            conv3_w=jnp.pad(w["conv3_w"][l], ((0, 5), (0, 0))),
            sconv_out_w=w["sconv_out_w"][l].astype(BF16),
            w_o=w["w_o"][l].astype(BF16),
            router_w=jnp.pad(w["router_w"][l], ((0, 0), (0, LANES - N_EXPERTS))).astype(BF16),
            router_b=jnp.pad(w["router_b"][l], (0, LANES - N_EXPERTS)).reshape(1, LANES),
        ))
    return chan_dft, layers


def _tiles(bsz, s):
    t_seq = min(512, s)
    t_dft_m = min(s, max(128, (8 * 1024 * 1024) // (bsz * GROUP * 4)))
    t_dft_k = min(2048, s // 2)
    return dict(seq=t_seq, dft_m=t_dft_m, dft_k=t_dft_k, fold=min(512, s // 2), tok=min(256, bsz * s))


def _moe(h2, logits, x1, mod, final_g, wl, layer, seq_len, tiles, final):
    t_tok = h2.shape[0]
    tok = tiles["tok"]
    n_assign = t_tok * TOP_K
    n_blocks = -(-(n_assign + N_EXPERTS * DISP_WIN) // MOE_BLOCK) + N_EXPERTS
    n_slots = n_blocks * MOE_BLOCK
    e_idx, rank, gates, base, tile_cnt, counts = _route(logits, tok)
    counts = counts[0, :N_EXPERTS].astype(jnp.int32)
    padded = jnp.where(counts > 0, (counts + DISP_WIN + MOE_BLOCK - 1) // MOE_BLOCK * MOE_BLOCK, 0)
    pend = jnp.cumsum(padded)
    pstart = pend - padded
    npad = padded - counts
    block_start = jnp.arange(n_blocks, dtype=jnp.int32) * MOE_BLOCK
    block_e = jnp.minimum(jnp.sum(block_start[:, None] >= pend[None, :], axis=1), N_EXPERTS - 1).astype(jnp.int32)
    n_used = (pend[-1:] // MOE_BLOCK).astype(jnp.int32)
    seg_lo = pstart[None, :] + base[:, 0, :N_EXPERTS]
    seg_n = tile_cnt[:, 0, :N_EXPERTS]
    win_lo = seg_lo - (seg_lo & (ROW_TILE - 1))
    win_rows = jnp.where(seg_n > 0, -(-(seg_lo + seg_n - win_lo) // ROW_TILE) * ROW_TILE, 0)
    win_dst = jnp.cumsum(win_rows, axis=1) - win_rows
    col_off = jnp.pad(win_dst - win_lo + pstart[None, :], ((0, 0), (0, LANES - N_EXPERTS)))
    col_off = col_off.reshape(-1, 1, LANES).astype(jnp.int32)
    flat = lambda a: a.reshape(-1).astype(jnp.int32)
    head = seg_lo & (F32_TILE - 1)
    dwin_lo = seg_lo - head
    dwin_rows = jnp.where(seg_n > 0, -(-(seg_lo + seg_n - dwin_lo) // F32_TILE) * F32_TILE, 0)
    dwin_src = jnp.cumsum(dwin_rows, axis=1) - dwin_rows
    dcol_off = jnp.pad(dwin_src - dwin_lo + pstart[None, :], ((0, 0), (0, LANES - N_EXPERTS)))
    dcol_off = dcol_off.reshape(-1, 1, LANES).astype(jnp.int32)
    d_chunks = dwin_rows // F32_TILE
    xs = _dispatch(flat(dwin_src), flat(dwin_lo), flat(d_chunks), flat(jnp.where(seg_n > 0, head, 0)),
                   flat(jnp.sum(d_chunks > 0, axis=1)),
                   flat(jnp.sum(jnp.maximum(d_chunks - DISP_WIN // F32_TILE, 0), axis=1)),
                   flat(jnp.sum(dwin_rows, axis=1)),
                   pend.astype(jnp.int32), npad.astype(jnp.int32), e_idx, rank, dcol_off, h2, n_slots, tok)
    ys = _experts(block_e, n_used, xs, wl, layer)
    w_chunks = win_rows // ROW_TILE
    return _combine(flat(win_lo), flat(win_dst), flat(w_chunks), flat(jnp.sum(w_chunks, axis=1)),
                    flat(jnp.sum(w_chunks // 2, axis=1)), flat(jnp.sum(w_chunks % 2, axis=1)),
                    e_idx, rank, gates, col_off, x1, mod, final_g, ys, seq_len, tok, final)


def _encoder(x, c, w, chan_dft, layers, tables, tiles=None):
    bsz, s, _ = x.shape
    tiles = tiles or _tiles(bsz, s)
    cos_t, nsin_t = tables
    depth = len(layers)
    final_g = w["final_g"].reshape(1, D_MODEL)
    ada_b = w["ada_b"].reshape(depth, 1, 6 * D_MODEL)
    for l, wl in enumerate(layers):
        mod = _ada(c, w["ada_w"], ada_b, l).reshape(bsz, 6, D_MODEL)
        mod = jnp.pad(mod, ((0, 0), (0, 2), (0, 0)))
        p, u = _inproj(x, mod, wl["norm1_g"], wl["w_in"], chan_dft, tiles["seq"])
        y = _seqdft(cos_t, nsin_t, p, _fold(p, tiles["fold"]), tiles["dft_m"], tiles["dft_k"])
        x1, h2, logits = _mix(y, u, x, mod, wl, tiles["seq"])
        x = _moe(h2.reshape(bsz * s, D_MODEL), logits.reshape(bsz * s, LANES),
                 x1.reshape(bsz * s, D_MODEL), mod, final_g, wl, l, s, tiles,
                 final=(l == depth - 1)).reshape(bsz, s, D_MODEL)
    return x


def _seq_tables(s):
    k = jnp.arange(s // 2, dtype=jnp.int32)[None, :]

    def cos_sin(j):
        ang = ((j[:, None] * k) % s).astype(F32) * (2.0 * math.pi / s)
        return jnp.cos(ang), jnp.sin(ang)

    ca, sa = cos_sin(jnp.arange(s // LANES, dtype=jnp.int32) * LANES)
    cb, sb = cos_sin(jnp.arange(LANES, dtype=jnp.int32))
    scale = 1.0 / math.sqrt(s)
    cos_t = (ca[:, None, :] * cb[None, :, :] - sa[:, None, :] * sb[None, :, :]) * scale
    nsin_t = (sa[:, None, :] * cb[None, :, :] + ca[:, None, :] * sb[None, :, :]) * (-scale)
    return cos_t.reshape(s, s // 2).astype(BF16), nsin_t.reshape(s, s // 2).astype(BF16)


def kernel(x_prompt, x_sample, c_prompt, c_sample, norm1_g, norm2_g, ada_w, ada_b, w_in, fourier_w, conv31_w, conv31_b, gn_g, gn_b, pw_w, pool_w, pool_scale, conv3_w, sconv_out_w, w_o, router_w, router_b, exp_gate_w, exp_gate_b, exp_up_w, exp_up_b, exp_down_w, exp_down_b, final_g):
    w = dict(norm1_g=norm1_g, norm2_g=norm2_g, ada_w=ada_w, ada_b=ada_b, w_in=w_in, fourier_w=fourier_w,
             conv31_w=conv31_w, conv31_b=conv31_b, gn_g=gn_g, gn_b=gn_b, pw_w=pw_w, pool_w=pool_w,
             pool_scale=pool_scale, conv3_w=conv3_w, sconv_out_w=sconv_out_w, w_o=w_o, router_w=router_w,
             router_b=router_b, exp_gate_w=exp_gate_w, exp_gate_b=exp_gate_b, exp_up_w=exp_up_w,
             exp_up_b=exp_up_b, exp_down_w=exp_down_w, exp_down_b=exp_down_b, final_g=final_g)
    chan_dft, layers = _prep_weights(w)
    y_prompt = _encoder(x_prompt, c_prompt, w, chan_dft, layers, _seq_tables(x_prompt.shape[1]))
    y_sample = _encoder(x_sample, c_sample, w, chan_dft, layers, _seq_tables(x_sample.shape[1]))
    return (y_prompt, y_sample)
```

```python
import functools
import math

import jax
import jax.numpy as jnp
from jax import lax
from jax.experimental import pallas as pl
from jax.experimental.pallas import tpu as pltpu

D_MODEL = 1024
GROUP = 256
HEAD_DIM = 64
CONV31 = 31
POOL_WINDOWS = (2, 4, 8, 16)
N_EXPERTS = 32
TOP_K = 4
D_FF = 1024
SWIGLU_ALPHA = 1.702
SWIGLU_LIMIT = 7.0
MOE_BLOCK = 512
RMS_EPS = 1e-6
GN_EPS = 1e-5
HALO = 16
LANES = 128
VMEM_LIMIT = 48 * 1024 * 1024

F32 = jnp.float32
BF16 = jnp.bfloat16


def _cparams(sem):
    return pltpu.CompilerParams(dimension_semantics=sem, vmem_limit_bytes=VMEM_LIMIT)


def _dot(a, b):
    return jnp.dot(a, b, preferred_element_type=F32)


def _split_bf16(x):
    hi = x.astype(BF16)
    lo = (x - hi.astype(F32)).astype(BF16)
    return hi, lo


def _dot_hilo(x, w_bf16):
    hi, lo = _split_bf16(x)
    return _dot(hi, w_bf16) + _dot(lo, w_bf16)


def _sigmoid(x):
    return 1.0 / (1.0 + jnp.exp(-x))


def _ada_kernel(c_ref, w_ref, b_ref, o_ref):
    c = c_ref[...]
    cs = c * _sigmoid(c)
    c_hi, c_lo = _split_bf16(cs)
    w_hi, w_lo = _split_bf16(w_ref[0])
    o_ref[...] = _dot(c_hi, w_hi) + _dot(c_lo, w_hi) + _dot(c_hi, w_lo) + b_ref[0]


def _ada(c, ada_w, ada_b, layer):
    bsz = c.shape[0]
    n_out = ada_w.shape[2]
    return pl.pallas_call(
        _ada_kernel,
        out_shape=jax.ShapeDtypeStruct((bsz, n_out), F32),
        grid=(n_out // D_MODEL,),
        in_specs=[
            pl.BlockSpec((bsz, D_MODEL), lambda j: (0, 0)),
            pl.BlockSpec((1, D_MODEL, D_MODEL), lambda j: (layer, 0, j)),
            pl.BlockSpec((1, 1, D_MODEL), lambda j: (layer, 0, j)),
        ],
        out_specs=pl.BlockSpec((bsz, D_MODEL), lambda j: (0, j)),
        compiler_params=_cparams(("arbitrary",)),
        name="ada",
    )(c, ada_w, ada_b)


def _rms_mod(x, g, scale, shift):
    ms = jnp.mean(x * x, axis=-1, keepdims=True)
    y = x * lax.rsqrt(ms + RMS_EPS) * g
    return y * (1.0 + scale) + shift


def _inproj_kernel(x_ref, mod_ref, g_ref, w_ref, cs_ref, p_ref, u_ref):
    x = x_ref[0]
    h = _rms_mod(x, g_ref[...], mod_ref[0, 1:2, :], mod_ref[0, 0:1, :]).astype(BF16)
    a = _dot(h, w_ref[:, 0:GROUP]).astype(BF16)
    p_ref[0] = _dot(a, cs_ref[...]).astype(BF16)
    for j in range(6):
        u_ref[0, :, j * GROUP:(j + 1) * GROUP] = _dot(
            h, w_ref[:, (j + 1) * GROUP:(j + 2) * GROUP]).astype(BF16)


def _inproj(x, mod, g, w_in, chan_dft, tm):
    bsz, s, _ = x.shape
    return pl.pallas_call(
        _inproj_kernel,
        out_shape=(jax.ShapeDtypeStruct((bsz, s, 2 * GROUP), BF16),
                   jax.ShapeDtypeStruct((bsz, s, 6 * GROUP), BF16)),
        grid=(bsz, s // tm),
        in_specs=[
            pl.BlockSpec((1, tm, D_MODEL), lambda b, i: (b, i, 0)),
            pl.BlockSpec((1, 8, D_MODEL), lambda b, i: (b, 0, 0)),
            pl.BlockSpec((1, D_MODEL), lambda b, i: (0, 0)),
            pl.BlockSpec((D_MODEL, 7 * GROUP), lambda b, i: (0, 0)),
            pl.BlockSpec((GROUP, 2 * GROUP), lambda b, i: (0, 0)),
        ],
        out_specs=(pl.BlockSpec((1, tm, 2 * GROUP), lambda b, i: (b, i, 0)),
                   pl.BlockSpec((1, tm, 6 * GROUP), lambda b, i: (b, i, 0))),
        compiler_params=_cparams(("arbitrary", "arbitrary")),
        name="inproj",
    )(x, mod, g, w_in, chan_dft)


def _fold_kernel(cur_ref, mir_ref, nxt_ref, e_ref, *, tf):
    kt = pl.program_id(1)
    r_io = lax.broadcasted_iota(jnp.int32, (tf, tf), 0)
    c_io = lax.broadcasted_iota(jnp.int32, (tf, tf), 1)
    flip = (c_io == tf - r_io).astype(BF16)
    rev = _dot(flip, mir_ref[0])
    cur = cur_ref[0].astype(F32)
    lane = lax.broadcasted_iota(jnp.int32, (1, 2 * GROUP), 1)
    first_tile_row0 = jnp.where(lane < GROUP, 0.0, cur[0:1, :])
    row0 = jnp.where(kt == 0, first_tile_row0, nxt_ref[0, 0:1, :].astype(F32))
    row = lax.broadcasted_iota(jnp.int32, (tf, 2 * GROUP), 0)
    rev = jnp.where(row == 0, row0, rev)
    sign = jnp.where(lax.broadcasted_iota(jnp.int32, (tf, 2 * GROUP), 1) < GROUP, 1.0, -1.0)
    e_ref[0] = (cur + sign * rev).astype(BF16)


def _fold(p, tf):
    bsz, s, _ = p.shape
    n_t = s // tf
    kern = functools.partial(_fold_kernel, tf=tf)
    return pl.pallas_call(
        kern,
        out_shape=jax.ShapeDtypeStruct((bsz, s // 2, 2 * GROUP), BF16),
        grid=(bsz, n_t // 2),
        in_specs=[
            pl.BlockSpec((1, tf, 2 * GROUP), lambda b, k: (b, k, 0)),
            pl.BlockSpec((1, tf, 2 * GROUP), lambda b, k: (b, n_t - 1 - k, 0)),
            pl.BlockSpec((1, 8, 2 * GROUP), lambda b, k: (b, ((n_t - k) % n_t) * (tf // 8), 0)),
        ],
        out_specs=pl.BlockSpec((1, tf, 2 * GROUP), lambda b, k: (b, k, 0)),
        compiler_params=_cparams(("arbitrary", "arbitrary")),
        name="fold",
    )(p, p, p)


def _seqdft_kernel(c_ref, s_ref, e_ref, mid_ref, o_ref, acc_ref, *, tm, seq_len):
    i = pl.program_id(0)
    k = pl.program_id(1)
    b = pl.program_id(2)
    contrib = _dot(c_ref[...], e_ref[0, :, 0:GROUP]) + _dot(s_ref[...], e_ref[0, :, GROUP:2 * GROUP])

    @pl.when(k == 0)
    def _():
        acc_ref[b] = contrib

    @pl.when(k > 0)
    def _():
        acc_ref[b] = acc_ref[b] + contrib

    @pl.when(k == pl.num_programs(1) - 1)
    def _():
        j = lax.broadcasted_iota(jnp.int32, (tm, GROUP), 0) + i * tm
        sign = jnp.where((j & 1) == 0, 1.0, -1.0) * (1.0 / math.sqrt(seq_len))
        o_ref[b] = (acc_ref[b] + sign * mid_ref[0, 0:1, 0:GROUP].astype(F32)).astype(BF16)


def _seqdft(cos_t, nsin_t, p, e, tm, tk):
    bsz, s, _ = p.shape
    half = s // 2
    kern = functools.partial(_seqdft_kernel, tm=tm, seq_len=s)
    return pl.pallas_call(
        kern,
        out_shape=jax.ShapeDtypeStruct((bsz, s, GROUP), BF16),
        grid=(s // tm, half // tk, bsz),
        in_specs=[
            pl.BlockSpec((tm, tk), lambda i, k, b: (i, k)),
            pl.BlockSpec((tm, tk), lambda i, k, b: (i, k)),
            pl.BlockSpec((1, tk, 2 * GROUP), lambda i, k, b: (b, k, 0)),
            pl.BlockSpec((1, 8, 2 * GROUP), lambda i, k, b: (b, half // 8, 0)),
        ],
        out_specs=pl.BlockSpec((bsz, tm, GROUP), lambda i, k, b: (0, i, 0)),
        scratch_shapes=[pltpu.VMEM((bsz, tm, GROUP), F32)],
        compiler_params=_cparams(("arbitrary", "arbitrary", "arbitrary")),
        name="seqdft",
    )(cos_t, nsin_t, e, p)


U_V, U_G, U_P, U_BG, U_CG, U_XV = (j * GROUP for j in range(6))


def _mix_kernel(y_ref, u_ref, up_ref, un_ref, x_ref, mod_ref,
                fw_ref, c31w_ref, c31b_ref, gng_ref, gnb_ref, avg_ref, pww_ref,
                poolw_ref, pscale_ref, c3w_ref, sow_ref, wo_ref, n2g_ref, rw_ref, rb_ref,
                x1_ref, h2_ref, lg_ref,
                zext, zsh, pext, s2, s4, s8, s16, qext, cat, *, tq, seq_len):
    i = pl.program_id(1)
    has_prev = (i > 0).astype(F32)
    has_next = (i < pl.num_programs(1) - 1).astype(F32)
    rows = tq + 2 * HALO

    def cols(ref, c0):
        return ref[0, :, c0:c0 + GROUP].astype(F32)

    def fill_ext(dst, fn):
        dst[0:HALO, :] = fn(up_ref) * has_prev
        dst[HALO:HALO + tq, :] = fn(u_ref)
        dst[HALO + tq:rows, :] = fn(un_ref) * has_next

    cat[:, 0:GROUP] = _dot(y_ref[0], fw_ref[...]).astype(BF16)

    fill_ext(zext, lambda r: cols(r, U_V) * _sigmoid(cols(r, U_G)))
    conv = jnp.broadcast_to(c31b_ref[...], (tq, GROUP))
    first = HALO - CONV31 // 2
    for res in range(8):
        taps = [j for j in range(CONV31) if (first + j) % 8 == res]
        span = (first + taps[-1]) // 8 * 8 + tq
        zsh[0:span, :] = zext[pl.ds(res, span), :]
        for j in taps:
            q = (first + j) // 8 * 8
            conv = conv + c31w_ref[j:j + 1, :] * zsh[q:q + tq, :]
    mu = _dot_hilo(conv, avg_ref[...])
    dev = conv - mu
    var = _dot_hilo(dev * dev, avg_ref[...])
    zn = dev * lax.rsqrt(var + GN_EPS) * gng_ref[...] + gnb_ref[...]
    cat[:, GROUP:2 * GROUP] = _dot((zn * _sigmoid(zn)).astype(BF16), pww_ref[...]).astype(BF16)

    fill_ext(pext, lambda r: cols(r, U_P))
    n = rows - 8
    zeros8 = jnp.zeros((8, GROUP), F32)
    s2[0:n, :] = pext[0:n, :] + pext[pl.ds(1, n), :]
    s2[n:rows, :] = zeros8
    s4[0:n, :] = s2[0:n, :] + s2[pl.ds(2, n), :]
    s4[n:rows, :] = zeros8
    s8[0:n, :] = s4[0:n, :] + s4[pl.ds(4, n), :]
    s8[n:rows, :] = zeros8
    s16[0:n, :] = s8[0:n, :] + s8[pl.ds(8, n), :]
    lane = lax.broadcasted_iota(jnp.int32, (tq, GROUP), 1)
    pos = lax.broadcasted_iota(jnp.int32, (tq, GROUP), 0) + i * tq
    win = jnp.where(lane < HEAD_DIM, s2[pl.ds(HALO - 1, tq), :],
                    jnp.where(lane < 2 * HEAD_DIM, s4[pl.ds(HALO - 2, tq), :],
                              jnp.where(lane < 3 * HEAD_DIM, s8[pl.ds(HALO - 4, tq), :],
                                        s16[pl.ds(HALO - 8, tq), :])))
    half = jnp.where(lane < HEAD_DIM, 1,
                     jnp.where(lane < 2 * HEAD_DIM, 2, jnp.where(lane < 3 * HEAD_DIM, 4, 8)))
    cnt = jnp.minimum(pos + half, seq_len) - jnp.maximum(pos - half, 0)
    dpool = win / cnt.astype(F32) - pext[HALO:HALO + tq, :]
    cat[:, 2 * GROUP:3 * GROUP] = (_dot(dpool.astype(BF16), poolw_ref[...]) * pscale_ref[...]).astype(BF16)

    fill_ext(qext, lambda r: cols(r, U_CG) * cols(r, U_XV))
    c3 = (c3w_ref[0:1, :] * qext[pl.ds(HALO - 1, tq), :]
          + c3w_ref[1:2, :] * qext[HALO:HALO + tq, :]
          + c3w_ref[2:3, :] * qext[pl.ds(HALO + 1, tq), :])
    cat[:, 3 * GROUP:4 * GROUP] = _dot((cols(u_ref, U_BG) * c3).astype(BF16), sow_ref[...]).astype(BF16)

    mixed = _dot(cat[...], wo_ref[...])
    x1 = x_ref[0] + mod_ref[0, 2:3, :] * mixed
    x1_ref[0] = x1
    h2 = _rms_mod(x1, n2g_ref[...], mod_ref[0, 4:5, :], mod_ref[0, 3:4, :])
    h2_ref[0] = h2
    lg_ref[0] = _dot(h2.astype(BF16), rw_ref[...]) + rb_ref[...]


def _mix(y, u, x, mod, wl, tq):
    bsz, s, _ = x.shape
    rows = tq + 2 * HALO
    hb = tq // HALO
    n_hblk = s // HALO
    full = lambda shape: pl.BlockSpec(shape, lambda b, i: tuple(0 for _ in shape))
    kern = functools.partial(_mix_kernel, tq=tq, seq_len=s)
    return pl.pallas_call(
        kern,
        out_shape=(jax.ShapeDtypeStruct((bsz, s, D_MODEL), F32),
                   jax.ShapeDtypeStruct((bsz, s, D_MODEL), F32),
                   jax.ShapeDtypeStruct((bsz, s, LANES), F32)),
        grid=(bsz, s // tq),
        in_specs=[
            pl.BlockSpec((1, tq, GROUP), lambda b, i: (b, i, 0)),
            pl.BlockSpec((1, tq, 6 * GROUP), lambda b, i: (b, i, 0)),
            pl.BlockSpec((1, HALO, 6 * GROUP), lambda b, i: (b, jnp.maximum(i * hb - 1, 0), 0)),
            pl.BlockSpec((1, HALO, 6 * GROUP), lambda b, i: (b, jnp.minimum((i + 1) * hb, n_hblk - 1), 0)),
            pl.BlockSpec((1, tq, D_MODEL), lambda b, i: (b, i, 0)),
            pl.BlockSpec((1, 8, D_MODEL), lambda b, i: (b, 0, 0)),
            full((GROUP, GROUP)),
            full((32, GROUP)),
            full((1, GROUP)),
            full((1, GROUP)),
            full((1, GROUP)),
            full((GROUP, GROUP)),
            full((GROUP, GROUP)),
            full((GROUP, GROUP)),
            full((1, GROUP)),
            full((8, GROUP)),
            full((GROUP, GROUP)),
            full((D_MODEL, D_MODEL)),
            full((1, D_MODEL)),
            full((D_MODEL, LANES)),
            full((1, LANES)),
        ],
        out_specs=(pl.BlockSpec((1, tq, D_MODEL), lambda b, i: (b, i, 0)),
                   pl.BlockSpec((1, tq, D_MODEL), lambda b, i: (b, i, 0)),
                   pl.BlockSpec((1, tq, LANES), lambda b, i: (b, i, 0))),
        scratch_shapes=[pltpu.VMEM((rows, GROUP), F32) for _ in range(8)]
        + [pltpu.VMEM((tq, D_MODEL), BF16)],
        compiler_params=_cparams(("arbitrary", "arbitrary")),
        name="mix",
    )(y, u, u, u, x, mod, wl["fourier_w"], wl["conv31_w"], wl["conv31_b"], wl["gn_g"], wl["gn_b"],
      wl["avg"], wl["pw_w"], wl["pool_w"], wl["pool_scale"], wl["conv3_w"], wl["sconv_out_w"],
      wl["w_o"], wl["norm2_g"], wl["router_w"], wl["router_b"])


def _route_kernel(lg_ref, e_ref, rank_ref, gate_ref, base_ref, tcnt_ref, cnt_ref, carry, *, tr, sub):
    i = pl.program_id(0)

    @pl.when(i == 0)
    def _():
        carry[...] = jnp.zeros_like(carry)

    lane = lax.broadcasted_iota(jnp.int32, (tr, LANES), 1).astype(F32)
    neg = jnp.float32(-jnp.inf)
    work = jnp.where(lane < N_EXPERTS, lg_ref[...], neg)
    vals, idxs, hots = [], [], []
    for _ in range(TOP_K):
        m = jnp.max(work, axis=1, keepdims=True)
        idx = jnp.min(jnp.where(work == m, lane, float(LANES)), axis=1, keepdims=True)
        hot = lane == idx
        vals.append(m)
        idxs.append(idx.astype(jnp.int32))
        hots.append(hot)
        work = jnp.where(hot, neg, work)
    exps = [jnp.exp(v - vals[0]) for v in vals]
    denom = exps[0] + exps[1] + exps[2] + exps[3]
    member = sum(h.astype(F32) for h in hots)
    r_io = lax.broadcasted_iota(jnp.int32, (tr, tr), 0)
    c_io = lax.broadcasted_iota(jnp.int32, (tr, tr), 1)
    lower = (c_io < r_io).astype(BF16)
    before = _dot(lower, member.astype(BF16)) + carry[...]
    col = lax.broadcasted_iota(jnp.int32, (tr, TOP_K), 1)
    e_out = jnp.zeros((tr, TOP_K), jnp.int32)
    r_out = jnp.zeros((tr, TOP_K), jnp.int32)
    g_out = jnp.zeros((tr, TOP_K), F32)
    for k in range(TOP_K):
        rk = jnp.sum(jnp.where(hots[k], before, 0.0), axis=1, keepdims=True).astype(jnp.int32)
        e_out = jnp.where(col == k, idxs[k], e_out)
        r_out = jnp.where(col == k, rk, r_out)
        g_out = jnp.where(col == k, exps[k] / denom, g_out)
    e_ref[...] = e_out
    rank_ref[...] = r_out
    gate_ref[...] = g_out
    running = carry[...]
    for q in range(tr // sub):
        cnt_q = jnp.sum(member[q * sub:(q + 1) * sub, :], axis=0, keepdims=True)
        base_ref[q] = running.astype(jnp.int32)
        tcnt_ref[q] = cnt_q.astype(jnp.int32)
        running = running + cnt_q
    carry[...] = running
    cnt_ref[...] = running


def _route(logits, sub):
    t_tok = logits.shape[0]
    tr = min(2 * sub, t_tok)
    n_steps = t_tok // tr
    n_tiles = t_tok // sub
    kern = functools.partial(_route_kernel, tr=tr, sub=sub)
    tile = lambda w: pl.BlockSpec((tr, w), lambda i: (i, 0))
    per_tile = pl.BlockSpec((tr // sub, 1, LANES), lambda i: (i, 0, 0))
    return pl.pallas_call(
        kern,
        out_shape=(jax.ShapeDtypeStruct((t_tok, TOP_K), jnp.int32),
                   jax.ShapeDtypeStruct((t_tok, TOP_K), jnp.int32),
                   jax.ShapeDtypeStruct((t_tok, TOP_K), F32),
                   jax.ShapeDtypeStruct((n_tiles, 1, LANES), jnp.int32),
                   jax.ShapeDtypeStruct((n_tiles, 1, LANES), jnp.int32),
                   jax.ShapeDtypeStruct((1, LANES), F32)),
        grid=(n_steps,),
        in_specs=[tile(LANES)],
        out_specs=(tile(TOP_K), tile(TOP_K), tile(TOP_K), per_tile, per_tile,
                   pl.BlockSpec((1, LANES), lambda i: (0, 0))),
        scratch_shapes=[pltpu.VMEM((1, LANES), F32)],
        compiler_params=_cparams(("arbitrary",)),
        name="route",
    )(logits)


F32_TILE = 8
DISP_ROWS = TOP_K * 256 + N_EXPERTS * 2 * F32_TILE
CARRY_ROWS = N_EXPERTS * F32_TILE
DISP_WIN = 64
XS_COLS = D_MODEL // 2


def _pack_pairs(x):
    lo = lax.bitcast_convert_type(x[:, :XS_COLS], jnp.uint32)
    hi = lax.bitcast_convert_type(x[:, XS_COLS:], jnp.uint32)
    return (hi & jnp.uint32(0xFFFF0000)) | (lo >> 16)


def _unpack_pairs(w):
    lo = lax.bitcast_convert_type(w << 16, F32).astype(BF16)
    hi = lax.bitcast_convert_type(w & jnp.uint32(0xFFFF0000), F32).astype(BF16)
    return lo, hi


def _dispatch_kernel(src_ref, dst_ref, nch_ref, head_ref, nwin_ref, nextra_ref, used_ref, pend_ref, npad_ref,
                     e_ref, rank_ref, off_ref, h_ref, xs_hbm, outbuf, carry, zeros, sems, sem_z, *, td):
    i = pl.program_id(0)
    n_steps = pl.num_programs(0)
    slot = i % 2

    def zero_blk(row):
        return pltpu.make_async_copy(zeros, xs_hbm.at[pl.ds(pl.multiple_of(row, MOE_BLOCK), MOE_BLOCK)], sem_z)

    @pl.when(i == 0)
    def _():
        zeros[...] = jnp.zeros_like(zeros)
        carry[...] = jnp.zeros_like(carry)
        outbuf[...] = jnp.zeros_like(outbuf)
        first_unused = pend_ref[N_EXPERTS - 1] // MOE_BLOCK
        n_blocks = xs_hbm.shape[0] // MOE_BLOCK

        def start(e, c):
            @pl.when(npad_ref[e] > 0)
            def _():
                zero_blk(pend_ref[e] - MOE_BLOCK).start()

            @pl.when(npad_ref[e] > MOE_BLOCK)
            def _():
                zero_blk(pend_ref[e] - 2 * MOE_BLOCK).start()
            return c

        def wait(e, c):
            @pl.when(npad_ref[e] > 0)
            def _():
                zero_blk(pend_ref[e] - MOE_BLOCK).wait()

            @pl.when(npad_ref[e] > MOE_BLOCK)
            def _():
                zero_blk(pend_ref[e] - 2 * MOE_BLOCK).wait()
            return c

        def start_tail(b, c):
            zero_blk(b * MOE_BLOCK).start()
            return c

        def wait_tail(b, c):
            zero_blk(b * MOE_BLOCK).wait()
            return c

        lax.fori_loop(0, N_EXPERTS, start, 0)
        lax.fori_loop(first_unused, n_blocks, start_tail, 0)
        lax.fori_loop(0, N_EXPERTS, wait, 0)
        lax.fori_loop(first_unused, n_blocks, wait_tail, 0)

    def window_copies(tile, sl, act):
        for e in range(N_EXPERTS):
            n_ch = nch_ref[tile * N_EXPERTS + e]
            src0 = pl.multiple_of(src_ref[tile * N_EXPERTS + e], F32_TILE)
            dst0 = pl.multiple_of(dst_ref[tile * N_EXPERTS + e], F32_TILE)

            @pl.when(n_ch > 0)
            def _():
                act(pltpu.make_async_copy(outbuf.at[sl, pl.ds(src0, DISP_WIN)],
                                          xs_hbm.at[pl.ds(dst0, DISP_WIN)], sems.at[sl]))

            def per_chunk(ci, c2):
                src = pl.multiple_of(src0 + ci * F32_TILE, F32_TILE)
                dst = pl.multiple_of(dst0 + ci * F32_TILE, F32_TILE)
                act(pltpu.make_async_copy(outbuf.at[sl, pl.ds(src, F32_TILE)],
                                          xs_hbm.at[pl.ds(dst, F32_TILE)], sems.at[sl]))
                return c2

            lax.fori_loop(DISP_WIN // F32_TILE, n_ch, per_chunk, 0)

    lane_td = lax.broadcasted_iota(jnp.int32, (td, LANES), 1)
    off_f = off_ref[0].astype(F32)
    pos_cols = jnp.full((td, LANES), -1.0, F32)
    for k in range(TOP_K):
        off_k = jnp.sum(jnp.where(lane_td == e_ref[:, k:k + 1], off_f, 0.0), axis=1, keepdims=True)
        pos_cols = jnp.where(lane_td == k, rank_ref[:, k:k + 1].astype(F32) + off_k, pos_cols)
    pos_t = jnp.transpose(pos_cols).astype(jnp.int32)
    pos = [jnp.broadcast_to(pos_t[k:k + 1, :], (GROUP, td)) for k in range(TOP_K)]
    row_io = lax.broadcasted_iota(jnp.int32, (GROUP, td), 0)
    h_bf = h_ref[...].astype(BF16)
    def select_rows(rc):
        r = row_io + rc * GROUP
        tok = jnp.zeros((GROUP, td), F32)
        for k in range(TOP_K):
            tok = jnp.where(r == pos[k], 1.0, tok)
        outbuf[slot, rc * GROUP:(rc + 1) * GROUP, :] = _pack_pairs(_dot(tok.astype(BF16), h_bf))

    for rc in range(DISP_ROWS // GROUP):
        if rc * GROUP < TOP_K * td:
            select_rows(rc)
        else:
            pl.when(used_ref[i] > rc * GROUP)(functools.partial(select_rows, rc))

    sub_io = lax.broadcasted_iota(jnp.int32, (F32_TILE, XS_COLS), 0)
    for e in range(N_EXPERTS):
        es = slice(e * F32_TILE, (e + 1) * F32_TILE)
        n_ch = nch_ref[i * N_EXPERTS + e]
        first = pl.multiple_of(src_ref[i * N_EXPERTS + e], F32_TILE)
        outbuf[slot, pl.ds(first, F32_TILE), :] = jnp.where(
            sub_io < head_ref[i * N_EXPERTS + e], carry[es, :], outbuf[slot, pl.ds(first, F32_TILE), :])
        last = pl.multiple_of(first + jnp.maximum(n_ch - 1, 0) * F32_TILE, F32_TILE)
        carry[es, :] = jnp.where(n_ch > 0, outbuf[slot, pl.ds(last, F32_TILE), :], carry[es, :])

    def wait_copies(tile, sl):
        def wait_window(c, carry_):
            pltpu.make_async_copy(outbuf.at[sl, pl.ds(0, DISP_WIN)], xs_hbm.at[pl.ds(0, DISP_WIN)],
                                  sems.at[sl]).wait()
            return carry_

        def wait_chunk(c, carry_):
            pltpu.make_async_copy(outbuf.at[sl, pl.ds(0, F32_TILE)], xs_hbm.at[pl.ds(0, F32_TILE)],
                                  sems.at[sl]).wait()
            return carry_

        lax.fori_loop(0, nwin_ref[tile], wait_window, 0)
        lax.fori_loop(0, nextra_ref[tile], wait_chunk, 0)

    @pl.when(i > 0)
    def _():
        wait_copies(i - 1, 1 - slot)

    window_copies(i, slot, lambda cp: cp.start())

    @pl.when(i == n_steps - 1)
    def _():
        wait_copies(i, slot)


def _dispatch(chunk_src, chunk_dst, n_chunks, head, n_windows, n_extra, used_rows, pend, npad, e_idx, rank,
              col_off, h2, n_slots, td):
    t_tok = h2.shape[0]
    assert DISP_ROWS >= TOP_K * td + N_EXPERTS * 2 * (F32_TILE - 1) + F32_TILE and DISP_ROWS % GROUP == 0
    kern = functools.partial(_dispatch_kernel, td=td)
    tok4 = lambda: pl.BlockSpec((td, TOP_K), lambda i, *_: (i, 0))
    return pl.pallas_call(
        kern,
        out_shape=jax.ShapeDtypeStruct((n_slots, XS_COLS), jnp.uint32),
        grid_spec=pltpu.PrefetchScalarGridSpec(
            num_scalar_prefetch=9,
            grid=(t_tok // td,),
            in_specs=[tok4(), tok4(),
                      pl.BlockSpec((1, 1, LANES), lambda i, *_: (i, 0, 0)),
                      pl.BlockSpec((td, D_MODEL), lambda i, *_: (i, 0))],
            out_specs=pl.BlockSpec(memory_space=pl.ANY),
            scratch_shapes=[pltpu.VMEM((2, DISP_ROWS + DISP_WIN, XS_COLS), jnp.uint32),
                            pltpu.VMEM((CARRY_ROWS, XS_COLS), jnp.uint32),
                            pltpu.VMEM((MOE_BLOCK, XS_COLS), jnp.uint32),
                            pltpu.SemaphoreType.DMA((2,)),
                            pltpu.SemaphoreType.DMA]),
        compiler_params=_cparams(("arbitrary",)),
        name="dispatch",
    )(chunk_src, chunk_dst, n_chunks, head, n_windows, n_extra, used_rows, pend, npad, e_idx, rank, col_off, h2)


def _experts_kernel(be_ref, nu_ref, xs_ref, wg_ref, bg_ref, wu_ref, bu_ref, wd_ref, bd_ref, o_ref, act):
    i = pl.program_id(0)

    @pl.when(i < nu_ref[0])
    def _():
        x_lo, x_hi = _unpack_pairs(xs_ref[...])

        def proj(w_ref):
            return _dot(x_lo, w_ref[0, :XS_COLS, :]) + _dot(x_hi, w_ref[0, XS_COLS:, :])

        gt = jnp.minimum(proj(wg_ref) + bg_ref[0], SWIGLU_LIMIT)
        up = jnp.clip(proj(wu_ref) + bu_ref[0], -SWIGLU_LIMIT, SWIGLU_LIMIT)
        act[...] = ((up + 1.0) * (gt * _sigmoid(SWIGLU_ALPHA * gt))).astype(BF16)
        o_ref[...] = (_dot(act[...], wd_ref[0]) + bd_ref[0]).astype(BF16)

    @pl.when(i >= nu_ref[0])
    def _():
        o_ref[...] = jnp.zeros_like(o_ref)


def _experts(block_e, n_used, xs, wl, layer):
    n_slots = xs.shape[0]
    n_blocks = n_slots // MOE_BLOCK

    def expert(i, be, nu):
        return layer * N_EXPERTS + be[jnp.minimum(i, nu[0] - 1)]

    wspec = lambda: pl.BlockSpec((1, D_MODEL, D_FF), lambda i, be, nu: (expert(i, be, nu), 0, 0))
    bspec = lambda: pl.BlockSpec((1, 1, D_FF), lambda i, be, nu: (expert(i, be, nu), 0, 0))
    return pl.pallas_call(
        _experts_kernel,
        out_shape=jax.ShapeDtypeStruct((n_slots, D_MODEL), BF16),
        grid_spec=pltpu.PrefetchScalarGridSpec(
            num_scalar_prefetch=2,
            grid=(n_blocks,),
            in_specs=[pl.BlockSpec((MOE_BLOCK, XS_COLS), lambda i, be, nu: (jnp.minimum(i, nu[0] - 1), 0)),
                      wspec(), bspec(), wspec(), bspec(), wspec(), bspec()],
            out_specs=pl.BlockSpec((MOE_BLOCK, D_MODEL), lambda i, be, nu: (i, 0)),
            scratch_shapes=[pltpu.VMEM((MOE_BLOCK, D_FF), BF16)]),
        compiler_params=_cparams(("arbitrary",)),
        name="experts",
    )(block_e, n_used, xs, wl["exp_gate_w"], wl["exp_gate_b"], wl["exp_up_w"], wl["exp_up_b"],
      wl["exp_down_w"], wl["exp_down_b"])


ROW_TILE = 16
SEG_ROWS = TOP_K * 256 + N_EXPERTS * 2 * ROW_TILE
K_CHUNK = 512
BIG_CHUNK = 2 * ROW_TILE


def _combine_kernel(src_ref, dst_ref, nch_ref, tot_ref, big_ref, small_ref, e_ref, rank_ref, gate_ref, off_ref, x1_ref, mod_ref,
                    fg_ref, ys_hbm, o_ref, segbuf, gmat, acc, sems, *, tc, final):
    j = pl.program_id(0)
    n_tiles = pl.num_programs(0)
    slot = j % 2

    def chunks(tile, sl, act):
        def per_expert(e, c):
            src0 = src_ref[tile * N_EXPERTS + e]
            dst0 = dst_ref[tile * N_EXPERTS + e]
            n_ch = nch_ref[tile * N_EXPERTS + e]

            def per_pair(ci, c2):
                src = pl.multiple_of(src0 + ci * BIG_CHUNK, ROW_TILE)
                dst = pl.multiple_of(dst0 + ci * BIG_CHUNK, ROW_TILE)
                act(pltpu.make_async_copy(ys_hbm.at[pl.ds(src, BIG_CHUNK)],
                                          segbuf.at[sl, pl.ds(dst, BIG_CHUNK)], sems.at[sl]))
                return c2

            lax.fori_loop(0, n_ch // 2, per_pair, 0)

            @pl.when(n_ch % 2 == 1)
            def _():
                src = pl.multiple_of(src0 + (n_ch - 1) * ROW_TILE, ROW_TILE)
                dst = pl.multiple_of(dst0 + (n_ch - 1) * ROW_TILE, ROW_TILE)
                act(pltpu.make_async_copy(ys_hbm.at[pl.ds(src, ROW_TILE)],
                                          segbuf.at[sl, pl.ds(dst, ROW_TILE)], sems.at[sl]))
            return c

        lax.fori_loop(0, N_EXPERTS, per_expert, 0)

    @pl.when(j == 0)
    def _():
        segbuf[...] = jnp.zeros_like(segbuf)
        chunks(0, 0, lambda cp: cp.start())

    @pl.when(j + 1 < n_tiles)
    def _():
        chunks(j + 1, 1 - slot, lambda cp: cp.start())

    lane_tc = lax.broadcasted_iota(jnp.int32, (tc, LANES), 1)
    off_f = off_ref[0].astype(F32)
    cols = []
    for k in range(TOP_K):
        off_k = jnp.sum(jnp.where(lane_tc == e_ref[:, k:k + 1], off_f, 0.0), axis=1, keepdims=True)
        cols.append(rank_ref[:, k:k + 1] + off_k.astype(jnp.int32))
    colb = [jnp.broadcast_to(c, (tc, LANES)) for c in cols]
    gateb = [jnp.broadcast_to(gate_ref[:, k:k + 1], (tc, LANES)) for k in range(TOP_K)]

    def build(p0, p1):
        for p in range(p0, p1):
            col_io = lane_tc + p * LANES
            g = jnp.zeros((tc, LANES), F32)
            for k in reversed(range(TOP_K)):
                g = jnp.where(col_io == colb[k], gateb[k], g)
            gmat[:, p * LANES:(p + 1) * LANES] = g.astype(BF16)

    def wait_big(c, carry):
        pltpu.make_async_copy(ys_hbm.at[pl.ds(0, BIG_CHUNK)], segbuf.at[slot, pl.ds(0, BIG_CHUNK)],
                              sems.at[slot]).wait()
        return carry

    def wait_small(c, carry):
        pltpu.make_async_copy(ys_hbm.at[pl.ds(0, ROW_TILE)], segbuf.at[slot, pl.ds(0, ROW_TILE)],
                              sems.at[slot]).wait()
        return carry

    always = TOP_K * tc
    build(0, always // LANES)
    lax.fori_loop(0, big_ref[j], wait_big, 0)
    lax.fori_loop(0, small_ref[j], wait_small, 0)
    acc[...] = _dot(gmat[:, 0:always], segbuf[slot, 0:always, :])
    used = tot_ref[j] * ROW_TILE
    for c0 in range(always, SEG_ROWS, K_CHUNK):
        @pl.when(used > c0)
        def _():
            build(c0 // LANES, (c0 + K_CHUNK) // LANES)
            acc[...] = acc[...] + _dot(gmat[:, c0:c0 + K_CHUNK], segbuf[slot, c0:c0 + K_CHUNK, :])
    moe = acc[...]

    x2 = x1_ref[...] + mod_ref[0, 5:6, :] * moe
    if final:
        ms = jnp.mean(x2 * x2, axis=-1, keepdims=True)
        x2 = x2 * lax.rsqrt(ms + RMS_EPS) * fg_ref[...]
    o_ref[...] = x2


def _combine(chunk_src, chunk_dst, n_chunks, tile_chunks, tile_big, tile_small, e_idx, rank, gates, col_off, x1, mod, final_g, ys,
             seq_len, tc, final):
    t_tok = x1.shape[0]
    assert SEG_ROWS >= TOP_K * tc + N_EXPERTS * 2 * (ROW_TILE - 1) and (SEG_ROWS - TOP_K * tc) % K_CHUNK == 0
    kern = functools.partial(_combine_kernel, tc=tc, final=final)
    tok4 = lambda: pl.BlockSpec((tc, TOP_K), lambda i, *_: (i, 0))
    return pl.pallas_call(
        kern,
        out_shape=jax.ShapeDtypeStruct((t_tok, D_MODEL), F32),
        grid_spec=pltpu.PrefetchScalarGridSpec(
            num_scalar_prefetch=6,
            grid=(t_tok // tc,),
            in_specs=[tok4(), tok4(), tok4(),
                      pl.BlockSpec((1, 1, LANES), lambda i, *_: (i, 0, 0)),
                      pl.BlockSpec((tc, D_MODEL), lambda i, *_: (i, 0)),
                      pl.BlockSpec((1, 8, D_MODEL), lambda i, *_: ((i * tc) // seq_len, 0, 0)),
                      pl.BlockSpec((1, D_MODEL), lambda i, *_: (0, 0)),
                      pl.BlockSpec(memory_space=pl.ANY)],
            out_specs=pl.BlockSpec((tc, D_MODEL), lambda i, *_: (i, 0)),
            scratch_shapes=[pltpu.VMEM((2, SEG_ROWS, D_MODEL), BF16),
                            pltpu.VMEM((tc, SEG_ROWS), BF16),
                            pltpu.VMEM((tc, D_MODEL), F32),
                            pltpu.SemaphoreType.DMA((2,))]),
        compiler_params=_cparams(("arbitrary",)),
        name="combine",
    )(chunk_src, chunk_dst, n_chunks, tile_chunks, tile_big, tile_small, e_idx, rank, gates, col_off, x1, mod,
      final_g, ys)


def _dft_tables(n):
    j = jnp.arange(n, dtype=jnp.int32)
    ang = ((j[:, None] * j[None, :]) % n).astype(F32) * (2.0 * math.pi / n)
    scale = 1.0 / math.sqrt(n)
    return jnp.cos(ang) * scale, jnp.sin(ang) * scale


def _block_diag(blocks):
    n, r, c = blocks.shape
    rows = lax.broadcasted_iota(jnp.int32, (n * r, n * c), 0) // r
    cols = lax.broadcasted_iota(jnp.int32, (n * r, n * c), 1) // c
    return jnp.where(rows == cols, jnp.tile(blocks.reshape(n * r, c), (1, n)), 0)


def _prep_weights(w):
    depth = w["w_in"].shape[0]
    c64, s64 = _dft_tables(HEAD_DIM)
    n_heads = GROUP // HEAD_DIM
    chan_dft = jnp.concatenate([_block_diag(jnp.stack([c64] * n_heads)),
                                _block_diag(jnp.stack([s64] * n_heads))], axis=1).astype(BF16)
    avg = _block_diag(jnp.full((n_heads, HEAD_DIM, HEAD_DIM), 1.0 / HEAD_DIM, F32)).astype(BF16)
    stacked = dict(
        exp_gate_w=w["exp_gate_w"].astype(BF16).reshape(depth * N_EXPERTS, D_MODEL, D_FF),
        exp_up_w=w["exp_up_w"].astype(BF16).reshape(depth * N_EXPERTS, D_MODEL, D_FF),
        exp_down_w=w["exp_down_w"].astype(BF16).reshape(depth * N_EXPERTS, D_FF, D_MODEL),
        exp_gate_b=w["exp_gate_b"].reshape(depth * N_EXPERTS, 1, D_FF),
        exp_up_b=w["exp_up_b"].reshape(depth * N_EXPERTS, 1, D_FF),
        exp_down_b=w["exp_down_b"].reshape(depth * N_EXPERTS, 1, D_MODEL))
    layers = []
    for l in range(depth):
        layers.append(dict(
            stacked,
            norm1_g=w["norm1_g"][l].reshape(1, D_MODEL),
            norm2_g=w["norm2_g"][l].reshape(1, D_MODEL),
            w_in=w["w_in"][l].astype(BF16),
            fourier_w=w["fourier_w"][l].astype(BF16),
            conv31_w=jnp.pad(w["conv31_w"][l], ((0, 1), (0, 0))),
            conv31_b=w["conv31_b"][l].reshape(1, GROUP),
            gn_g=w["gn_g"][l].reshape(1, GROUP),
            gn_b=w["gn_b"][l].reshape(1, GROUP),
            avg=avg,
            pw_w=w["pw_w"][l].astype(BF16),
            pool_w=_block_diag(w["pool_w"][l]).astype(BF16),
            pool_scale=w["pool_scale"][l].reshape(1, GROUP),
            conv3_w=jnp.pad(w["conv3_w"][l], ((0, 5), (0, 0))),
            sconv_out_w=w["sconv_out_w"][l].astype(BF16),
            w_o=w["w_o"][l].astype(BF16),
            router_w=jnp.pad(w["router_w"][l], ((0, 0), (0, LANES - N_EXPERTS))).astype(BF16),
            router_b=jnp.pad(w["router_b"][l], (0, LANES - N_EXPERTS)).reshape(1, LANES),
        ))
    return chan_dft, layers


def _tiles(bsz, s):
    t_seq = min(512, s)
    t_dft_m = min(s, max(128, (8 * 1024 * 1024) // (bsz * GROUP * 4)))
    t_dft_k = min(2048, s // 2)
    return dict(seq=t_seq, dft_m=t_dft_m, dft_k=t_dft_k, fold=min(512, s // 2), tok=min(256, bsz * s))


def _moe(h2, logits, x1, mod, final_g, wl, layer, seq_len, tiles, final):
    t_tok = h2.shape[0]
    tok = tiles["tok"]
    n_assign = t_tok * TOP_K
    n_blocks = -(-(n_assign + N_EXPERTS * DISP_WIN) // MOE_BLOCK) + N_EXPERTS
    n_slots = n_blocks * MOE_BLOCK
    e_idx, rank, gates, base, tile_cnt, counts = _route(logits, tok)
    counts = counts[0, :N_EXPERTS].astype(jnp.int32)
    padded = jnp.where(counts > 0, (counts + DISP_WIN + MOE_BLOCK - 1) // MOE_BLOCK * MOE_BLOCK, 0)
    pend = jnp.cumsum(padded)
    pstart = pend - padded
    npad = padded - counts
    block_start = jnp.arange(n_blocks, dtype=jnp.int32) * MOE_BLOCK
    block_e = jnp.minimum(jnp.sum(block_start[:, None] >= pend[None, :], axis=1), N_EXPERTS - 1).astype(jnp.int32)
    n_used = (pend[-1:] // MOE_BLOCK).astype(jnp.int32)
    seg_lo = pstart[None, :] + base[:, 0, :N_EXPERTS]
    seg_n = tile_cnt[:, 0, :N_EXPERTS]
    win_lo = seg_lo - (seg_lo & (ROW_TILE - 1))
    win_rows = jnp.where(seg_n > 0, -(-(seg_lo + seg_n - win_lo) // ROW_TILE) * ROW_TILE, 0)
    win_dst = jnp.cumsum(win_rows, axis=1) - win_rows
    col_off = jnp.pad(win_dst - win_lo + pstart[None, :], ((0, 0), (0, LANES - N_EXPERTS)))
    col_off = col_off.reshape(-1, 1, LANES).astype(jnp.int32)
    flat = lambda a: a.reshape(-1).astype(jnp.int32)
    head = seg_lo & (F32_TILE - 1)
    dwin_lo = seg_lo - head
    dwin_rows = jnp.where(seg_n > 0, -(-(seg_lo + seg_n - dwin_lo) // F32_TILE) * F32_TILE, 0)
    dwin_src = jnp.cumsum(dwin_rows, axis=1) - dwin_rows
    dcol_off = jnp.pad(dwin_src - dwin_lo + pstart[None, :], ((0, 0), (0, LANES - N_EXPERTS)))
    dcol_off = dcol_off.reshape(-1, 1, LANES).astype(jnp.int32)
    d_chunks = dwin_rows // F32_TILE
    xs = _dispatch(flat(dwin_src), flat(dwin_lo), flat(d_chunks), flat(jnp.where(seg_n > 0, head, 0)),
                   flat(jnp.sum(d_chunks > 0, axis=1)),
                   flat(jnp.sum(jnp.maximum(d_chunks - DISP_WIN // F32_TILE, 0), axis=1)),
                   flat(jnp.sum(dwin_rows, axis=1)),
                   pend.astype(jnp.int32), npad.astype(jnp.int32), e_idx, rank, dcol_off, h2, n_slots, tok)
    ys = _experts(block_e, n_used, xs, wl, layer)
    w_chunks = win_rows // ROW_TILE
    return _combine(flat(win_lo), flat(win_dst), flat(w_chunks), flat(jnp.sum(w_chunks, axis=1)),
                    flat(jnp.sum(w_chunks // 2, axis=1)), flat(jnp.sum(w_chunks % 2, axis=1)),
                    e_idx, rank, gates, col_off, x1, mod, final_g, ys, seq_len, tok, final)


def _encoder(x, c, w, chan_dft, layers, tables, tiles=None):
    bsz, s, _ = x.shape
    tiles = tiles or _tiles(bsz, s)
    cos_t, nsin_t = tables
    depth = len(layers)
    final_g = w["final_g"].reshape(1, D_MODEL)
    ada_b = w["ada_b"].reshape(depth, 1, 6 * D_MODEL)
    for l, wl in enumerate(layers):
        mod = _ada(c, w["ada_w"], ada_b, l).reshape(bsz, 6, D_MODEL)
        mod = jnp.pad(mod, ((0, 0), (0, 2), (0, 0)))
        p, u = _inproj(x, mod, wl["norm1_g"], wl["w_in"], chan_dft, tiles["seq"])
        y = _seqdft(cos_t, nsin_t, p, _fold(p, tiles["fold"]), tiles["dft_m"], tiles["dft_k"])
        x1, h2, logits = _mix(y, u, x, mod, wl, tiles["seq"])
        x = _moe(h2.reshape(bsz * s, D_MODEL), logits.reshape(bsz * s, LANES),
                 x1.reshape(bsz * s, D_MODEL), mod, final_g, wl, l, s, tiles,
                 final=(l == depth - 1)).reshape(bsz, s, D_MODEL)
    return x


def _seq_tables(s):
    k = jnp.arange(s // 2, dtype=jnp.int32)[None, :]

    def cos_sin(j):
        ang = ((j[:, None] * k) % s).astype(F32) * (2.0 * math.pi / s)
        return jnp.cos(ang), jnp.sin(ang)

    ca, sa = cos_sin(jnp.arange(s // LANES, dtype=jnp.int32) * LANES)
    cb, sb = cos_sin(jnp.arange(LANES, dtype=jnp.int32))
    scale = 1.0 / math.sqrt(s)
    cos_t = (ca[:, None, :] * cb[None, :, :] - sa[:, None, :] * sb[None, :, :]) * scale
    nsin_t = (sa[:, None, :] * cb[None, :, :] + ca[:, None, :] * sb[None, :, :]) * (-scale)
    return cos_t.reshape(s, s // 2).astype(BF16), nsin_t.reshape(s, s // 2).astype(BF16)


def kernel(x_prompt, x_sample, c_prompt, c_sample, norm1_g, norm2_g, ada_w, ada_b, w_in, fourier_w, conv31_w, conv31_b, gn_g, gn_b, pw_w, pool_w, pool_scale, conv3_w, sconv_out_w, w_o, router_w, router_b, exp_gate_w, exp_gate_b, exp_up_w, exp_up_b, exp_down_w, exp_down_b, final_g):
    w = dict(norm1_g=norm1_g, norm2_g=norm2_g, ada_w=ada_w, ada_b=ada_b, w_in=w_in, fourier_w=fourier_w,
             conv31_w=conv31_w, conv31_b=conv31_b, gn_g=gn_g, gn_b=gn_b, pw_w=pw_w, pool_w=pool_w,
             pool_scale=pool_scale, conv3_w=conv3_w, sconv_out_w=sconv_out_w, w_o=w_o, router_w=router_w,
             router_b=router_b, exp_gate_w=exp_gate_w, exp_gate_b=exp_gate_b, exp_up_w=exp_up_w,
             exp_up_b=exp_up_b, exp_down_w=exp_down_w, exp_down_b=exp_down_b, final_g=final_g)
    chan_dft, layers = _prep_weights(w)
    y_prompt = _encoder(x_prompt, c_prompt, w, chan_dft, layers, _seq_tables(x_prompt.shape[1]))
    y_sample = _encoder(x_sample, c_sample, w, chan_dft, layers, _seq_tables(x_sample.shape[1]))
    return (y_prompt, y_sample)
```
